```python
import jax, jax.numpy as jnp
from jax import lax
import numpy as np

D_MODEL = 1024
BATCH = 2
SEQ = 8192
DEPTH = 1

HEAD_DIM = 64
N_RWKV_HEADS = 8
RWKV_WIDTH = N_RWKV_HEADS * HEAD_DIM
DECAY_LORA = 64
AAA_LORA = 64
GATE_LORA = 128
GN_EPS = 64e-5
N_Q_HEADS = 8
N_KV_HEADS = 2
Q_PER_KV = N_Q_HEADS // N_KV_HEADS
ATTN_WIDTH = N_Q_HEADS * HEAD_DIM
KV_WIDTH = N_KV_HEADS * HEAD_DIM
WINDOW = 128
BLOCK = 128
RWKV_COLS = 3 * RWKV_WIDTH + DECAY_LORA + AAA_LORA + GATE_LORA
ATTN_COLS = ATTN_WIDTH + 2 * KV_WIDTH
IN_COLS = RWKV_COLS + ATTN_COLS
MIX_WIDTH = RWKV_WIDTH + ATTN_WIDTH
D_FF = 2816
NORM_EPS = 1e-5

kernel_name = 'hymba_rwkv7_swa_sink_macaron'


def rms_norm(x, g):
    xf = x.astype(jnp.float32)
    y = xf * lax.rsqrt(jnp.mean(xf * xf, axis=-1, keepdims=True) + NORM_EPS)
    return (y * g.astype(jnp.float32)).astype(x.dtype)


def swiglu(x, w_gate, w_up, w_down):
    return (jax.nn.silu(x @ w_gate) * (x @ w_up)) @ w_down


def rwkv7_time_mix(p, shift_mix, w0, w2, a0, a2, g2, k_k, k_a, r_k, ln_w, ln_b):
    f32 = jnp.float32
    B, S, _ = p.shape
    H, N = N_RWKV_HEADS, HEAD_DIM
    p = p.astype(f32)
    p_prev = jnp.pad(p, ((0, 0), (1, 0), (0, 0)))[:, :-1]
    p = p + (p_prev - p) * shift_mix.astype(f32)
    cuts = [RWKV_WIDTH, 2 * RWKV_WIDTH, 3 * RWKV_WIDTH,
            3 * RWKV_WIDTH + DECAY_LORA, 3 * RWKV_WIDTH + DECAY_LORA + AAA_LORA]
    r, k, v, w_lo, a_lo, g_lo = jnp.split(p, cuts, axis=-1)
    w = -jax.nn.softplus(-(w0.astype(f32) + jnp.tanh(w_lo) @ w2.astype(f32))) - 0.5
    a = jax.nn.sigmoid(a0.astype(f32) + a_lo @ a2.astype(f32))
    g = jax.nn.sigmoid(g_lo) @ g2.astype(f32)
    kk = (k * k_k.astype(f32)).reshape(B, S, H, N)
    kk = kk / jnp.maximum(jnp.sqrt(jnp.sum(kk * kk, axis=-1, keepdims=True)), 1e-12)
    k = k * (1.0 + (a - 1.0) * k_a.astype(f32))
    rh, kh, vh, ah = [t.reshape(B, S, H, N) for t in (r, k, v, a)]
    decay = jnp.exp(-jnp.exp(w)).reshape(B, S, H, N)
    bh = kk * ah

    def step(state, inp):
        r_t, w_t, k_t, v_t, kk_t, b_t = inp
        sa = jnp.einsum('bhvk,bhk->bhv', state, -kk_t)
        state = (state * w_t[:, :, None, :] + sa[..., None] * b_t[:, :, None, :]
                 + v_t[..., None] * k_t[:, :, None, :])
        return state, jnp.einsum('bhvk,bhk->bhv', state, r_t)

    seq_first = lambda t: jnp.moveaxis(t, 1, 0)
    s0 = jnp.zeros((B, H, N, N), f32)
    _, o = lax.scan(step, s0, tuple(seq_first(t) for t in (rh, decay, kh, vh, kk, bh)))
    o = jnp.moveaxis(o, 0, 1)
    mu = jnp.mean(o, axis=-1, keepdims=True)
    var = jnp.mean(jnp.square(o - mu), axis=-1, keepdims=True)
    o = ((o - mu) * lax.rsqrt(var + GN_EPS)).reshape(B, S, RWKV_WIDTH)
    o = o * ln_w.astype(f32) + ln_b.astype(f32)
    bonus = jnp.sum(rh * kh * r_k.astype(f32), axis=-1, keepdims=True) * vh
    return (o + bonus.reshape(B, S, RWKV_WIDTH)) * g


def sliding_window_sink_attention(q, k, v, sinks):
    f32 = jnp.float32
    B, S, _ = q.shape
    nb = S // BLOCK
    qb = q.astype(f32).reshape(B, nb, BLOCK, N_KV_HEADS, Q_PER_KV, HEAD_DIM)
    kb = k.astype(f32).reshape(B, nb, BLOCK, N_KV_HEADS, HEAD_DIM)
    vb = v.astype(f32).reshape(B, nb, BLOCK, N_KV_HEADS, HEAD_DIM)

    def with_prev_block(t):
        prev = jnp.pad(t, ((0, 0), (1, 0), (0, 0), (0, 0), (0, 0)))[:, :-1]
        return jnp.concatenate([prev, t], axis=2)

    kb, vb = with_prev_block(kb), with_prev_block(vb)
    scores = jnp.einsum('bnqhgd,bnkhd->bnhgqk', qb, kb) * (HEAD_DIM ** -0.5)
    qi = jnp.arange(BLOCK)[:, None]
    kj = jnp.arange(2 * BLOCK)[None, :]
    dist = qi + BLOCK - kj
    band = (dist >= 0) & (dist < WINDOW)
    valid = band[None] & ((jnp.arange(nb)[:, None, None] > 0) | (kj[None] >= BLOCK))
    scores = jnp.where(valid[None, :, None, None], scores, -jnp.inf)
    sink = jnp.broadcast_to(sinks.astype(f32).reshape(N_KV_HEADS, Q_PER_KV, 1, 1),
                            scores.shape[:-1] + (1,))
    probs = jax.nn.softmax(jnp.concatenate([scores, sink], axis=-1), axis=-1)[..., :-1]
    out = jnp.einsum('bnhgqk,bnkhd->bnqhgd', probs, vb)
    return out.reshape(B, S, ATTN_WIDTH)


def setup_inputs(seed: int = 0) -> dict:
    key = jax.random.key(seed)
    ks = jax.random.split(key, 26)
    L, D, F, R = DEPTH, D_MODEL, D_FF, RWKV_WIDTH
    nrm = lambda i, shape, scale: scale * jax.random.normal(ks[i], shape, jnp.float32)
    chan = jnp.arange(R, dtype=jnp.float32) / (R - 1)
    w0_base = -5.5 + 5.0 * chan ** 0.85
    return {
        'x': nrm(0, (BATCH, SEQ, D), 1.0),
        'norm_ffn1': 1.0 + nrm(1, (L, D), 0.01),
        'ffn1_gate': nrm(2, (L, D, F), D ** -0.5),
        'ffn1_up': nrm(3, (L, D, F), D ** -0.5),
        'ffn1_down': nrm(4, (L, F, D), F ** -0.5),
        'norm_mix': 1.0 + nrm(5, (L, D), 0.01),
        'w_in': nrm(6, (L, D, IN_COLS), D ** -0.5),
        'b_in_attn': nrm(7, (L, ATTN_COLS), 0.01),
        'rwkv_shift_mix': jax.random.uniform(ks[8], (L, RWKV_COLS), jnp.float32),
        'rwkv_w0': w0_base[None, :] + nrm(9, (L, R), 0.1),
        'rwkv_w2': nrm(10, (L, DECAY_LORA, R), 0.1 * DECAY_LORA ** -0.5),
        'rwkv_a0': nrm(11, (L, R), 0.1),
        'rwkv_a2': nrm(12, (L, AAA_LORA, R), AAA_LORA ** -0.5),
        'rwkv_g2': nrm(13, (L, GATE_LORA, R), GATE_LORA ** -0.5),
        'rwkv_k_k': 0.85 + nrm(14, (L, R), 0.05),
        'rwkv_k_a': 1.0 + nrm(15, (L, R), 0.05),
        'rwkv_r_k': nrm(16, (L, N_RWKV_HEADS, HEAD_DIM), 0.1),
        'rwkv_ln_w': 1.0 + nrm(17, (L, R), 0.01),
        'rwkv_ln_b': nrm(18, (L, R), 0.01),
        'attn_sinks': nrm(19, (L, N_Q_HEADS), 1.0),
        'w_out': nrm(20, (L, MIX_WIDTH, D), 0.5 * MIX_WIDTH ** -0.5),
        'norm_ffn2': 1.0 + nrm(21, (L, D), 0.01),
        'ffn2_gate': nrm(22, (L, D, F), D ** -0.5),
        'ffn2_up': nrm(23, (L, D, F), D ** -0.5),
        'ffn2_down': nrm(24, (L, F, D), F ** -0.5),
        'norm_final': 1.0 + nrm(25, (D,), 0.01),
    }


def reference(x, norm_ffn1, ffn1_gate, ffn1_up, ffn1_down, norm_mix, w_in, b_in_attn,
              rwkv_shift_mix, rwkv_w0, rwkv_w2, rwkv_a0, rwkv_a2, rwkv_g2, rwkv_k_k, rwkv_k_a,
              rwkv_r_k, rwkv_ln_w, rwkv_ln_b, attn_sinks, w_out, norm_ffn2, ffn2_gate, ffn2_up,
              ffn2_down, norm_final):
    for l in range(DEPTH):
        h = rms_norm(x, norm_ffn1[l])
        x = x + 0.5 * swiglu(h, ffn1_gate[l], ffn1_up[l], ffn1_down[l])
        h = rms_norm(x, norm_mix[l])
        p = h @ w_in[l]
        p_rwkv = p[..., :RWKV_COLS]
        p_attn = p[..., RWKV_COLS:] + b_in_attn[l]
        q, k, v = jnp.split(p_attn, [ATTN_WIDTH, ATTN_WIDTH + KV_WIDTH], axis=-1)
        o_rwkv = rwkv7_time_mix(p_rwkv, rwkv_shift_mix[l], rwkv_w0[l], rwkv_w2[l], rwkv_a0[l],
                                rwkv_a2[l], rwkv_g2[l], rwkv_k_k[l], rwkv_k_a[l], rwkv_r_k[l],
                                rwkv_ln_w[l], rwkv_ln_b[l])
        o_attn = sliding_window_sink_attention(q, k, v, attn_sinks[l])
        mixed = jnp.concatenate([o_rwkv.astype(x.dtype), o_attn.astype(x.dtype)], axis=-1)
        x = x + mixed @ w_out[l]
        h = rms_norm(x, norm_ffn2[l])
        x = x + 0.5 * swiglu(h, ffn2_gate[l], ffn2_up[l], ffn2_down[l])
    return rms_norm(x, norm_final)
```

```python
import functools

import jax
import jax.numpy as jnp
from jax import lax
from jax.experimental import pallas as pl
from jax.experimental.pallas import tpu as pltpu

F32 = jnp.float32
BF16 = jnp.bfloat16
HI = lax.Precision.HIGHEST

HEAD_DIM = 64
LANES = 128
CHUNK = 64
N_RWKV_HEADS = 8
RWKV_WIDTH = N_RWKV_HEADS * HEAD_DIM
N_PAIRS = RWKV_WIDTH // LANES
DECAY_LORA = 64
AAA_LORA = 64
GATE_LORA = 128
RWKV_COLS = 3 * RWKV_WIDTH + DECAY_LORA + AAA_LORA + GATE_LORA
N_Q_HEADS = 8
N_KV_HEADS = 2
ATTN_WIDTH = N_Q_HEADS * HEAD_DIM
KV_WIDTH = N_KV_HEADS * HEAD_DIM
ATTN_COLS = ATTN_WIDTH + 2 * KV_WIDTH
WINDOW = 128
GN_EPS = 64e-5
NORM_EPS = 1e-5
VMEM_LIMIT = 56 * 1024 * 1024


def _dot(a, b, prec=None):
    return jnp.dot(a, b, preferred_element_type=F32, precision=prec)


def _dot_nt(a, b, prec=None):
    return lax.dot_general(a, b, (((1,), (1,)), ((), ())),
                           preferred_element_type=F32, precision=prec)


def _rms(x, g):
    ms = jnp.mean(x * x, axis=-1, keepdims=True)
    return x * lax.rsqrt(ms + NORM_EPS) * g


def _low_half(shape):
    return (lax.broadcasted_iota(jnp.int32, shape, 1) & (LANES - 1)) < HEAD_DIM


def _bd(x):
    lo = _low_half(x.shape)
    return jnp.concatenate([jnp.where(lo, x, 0.0), jnp.where(lo, 0.0, x)], axis=0)


def _ffn_kernel(*refs, has_mix, final_norm, nf):
    refs = list(refs)
    x_ref = refs.pop(0)
    if has_mix:
        orw_ref, oat_ref, wor_ref, woa_ref = refs[:4]
        refs = refs[4:]
    g_ref, wg_ref, wu_ref, wd_ref = refs[:4]
    refs = refs[4:]
    if final_norm:
        gf_ref = refs.pop(0)
    out_ref, xres, h_s, acc = refs
    j = pl.program_id(1)

    @pl.when(j == 0)
    def _():
        x = x_ref[...]
        if has_mix:
            x = (x + _dot(orw_ref[...].astype(BF16), wor_ref[...])
                 + _dot(oat_ref[...].astype(BF16), woa_ref[...]))
        xres[...] = x
        h_s[...] = _rms(x, g_ref[...]).astype(BF16)
        acc[...] = jnp.zeros_like(acc)

    h = h_s[...]
    gate = _dot(h, wg_ref[...])
    up = _dot(h, wu_ref[...])
    act = (gate * jax.nn.sigmoid(gate) * up).astype(BF16)
    acc[...] += _dot(act, wd_ref[...])

    @pl.when(j == nf - 1)
    def _():
        y = xres[...] + 0.5 * acc[...]
        if final_norm:
            y = _rms(y, gf_ref[...])
        out_ref[...] = y


def _ffn(x, norm, w_gate, w_up, w_down, mix=None, final_norm=None, tm=1024, tf=256):
    n, d = x.shape
    f = w_gate.shape[1]
    nt, nf = n // tm, f // tf
    row = lambda i, j: (i, 0)
    const = lambda i, j: (0, 0)
    args = [x]
    specs = [pl.BlockSpec((tm, d), row)]
    if mix is not None:
        o_rwkv, o_attn, w_out = mix
        wo = w_out.astype(BF16)
        args += [o_rwkv, o_attn, wo[:RWKV_WIDTH], wo[RWKV_WIDTH:]]
        specs += [pl.BlockSpec((tm, RWKV_WIDTH), row), pl.BlockSpec((tm, ATTN_WIDTH), row),
                  pl.BlockSpec((RWKV_WIDTH, d), const), pl.BlockSpec((ATTN_WIDTH, d), const)]
    args += [norm.reshape(1, d), w_gate.astype(BF16), w_up.astype(BF16), w_down.astype(BF16)]
    specs += [pl.BlockSpec((1, d), const),
              pl.BlockSpec((d, tf), lambda i, j: (0, j)),
              pl.BlockSpec((d, tf), lambda i, j: (0, j)),
              pl.BlockSpec((tf, d), lambda i, j: (j, 0))]
    if final_norm is not None:
        args.append(final_norm.reshape(1, d))
        specs.append(pl.BlockSpec((1, d), const))
    return pl.pallas_call(
        functools.partial(_ffn_kernel, has_mix=mix is not None,
                          final_norm=final_norm is not None, nf=nf),
        grid=(nt, nf),
        in_specs=specs,
        out_specs=pl.BlockSpec((tm, d), row),
        out_shape=jax.ShapeDtypeStruct((n, d), F32),
        scratch_shapes=[pltpu.VMEM((tm, d), F32), pltpu.VMEM((tm, d), BF16),
                        pltpu.VMEM((tm, d), F32)],
        compiler_params=pltpu.CompilerParams(
            dimension_semantics=("parallel", "arbitrary"), vmem_limit_bytes=VMEM_LIMIT),
        name="ffn_mix" if mix is not None else "ffn",
    )(*args)


def _inproj_kernel(x_ref, g_ref, w_ref, b_ref, prw_ref, pat_ref):
    h = _rms(x_ref[...], g_ref[...]).astype(BF16)
    p = _dot(h, w_ref[...]) + b_ref[...]
    prw_ref[...] = p[:, :RWKV_COLS]
    pat_ref[...] = p[:, RWKV_COLS:]


def _in_proj(x, norm, w_in, b_attn, tm=512):
    n, d = x.shape
    cols = w_in.shape[1]
    bias = jnp.concatenate([jnp.zeros((RWKV_COLS,), F32), b_attn.astype(F32)]).reshape(1, cols)
    row = lambda i: (i, 0)
    const = lambda i: (0, 0)
    return pl.pallas_call(
        _inproj_kernel,
        grid=(n // tm,),
        in_specs=[pl.BlockSpec((tm, d), row), pl.BlockSpec((1, d), const),
                  pl.BlockSpec((d, cols), const), pl.BlockSpec((1, cols), const)],
        out_specs=[pl.BlockSpec((tm, RWKV_COLS), row), pl.BlockSpec((tm, ATTN_COLS), row)],
        out_shape=[jax.ShapeDtypeStruct((n, RWKV_COLS), F32),
                   jax.ShapeDtypeStruct((n, ATTN_COLS), F32)],
        compiler_params=pltpu.CompilerParams(
            dimension_semantics=("parallel",), vmem_limit_bytes=VMEM_LIMIT),
        name="in_proj",
    )(x, norm.reshape(1, d), w_in.astype(BF16), bias)


def _rwkv_prep_kernel(p_ref, pprev_ref, mix_ref, w0_ref, w2_ref, a0_ref, a2_ref, g2_ref,
                      kk_ref, ka_ref, rk_ref, e_ref, ltri_ref,
                      m_ref, n_ref, r_ref, o_ref, bonus_ref, gate_ref, ps_ref,
                      *, tiles_per_seq, n_chunks):
    i = pl.program_id(0)
    p = p_ref[...]
    prev_row = jnp.where(i % tiles_per_seq == 0, 0.0, pprev_ref[7:8, :])
    row = lax.broadcasted_iota(jnp.int32, p.shape, 0)
    p_prev = jnp.where(row == 0, prev_row, pltpu.roll(p, 1, axis=0))
    ps_ref[...] = p + (p_prev - p) * mix_ref[...]

    C = CHUNK
    rowi = lax.broadcasted_iota(jnp.int32, (2 * C, 2 * LANES), 0)
    lane_s = lax.broadcasted_iota(jnp.int32, (2 * C, 2 * LANES), 1) & (C - 1)
    t_idx = rowi & (C - 1)
    keep_a1 = lane_s < t_idx + jnp.where(rowi < C, 0, 1)
    r64 = lax.broadcasted_iota(jnp.int32, (C, LANES), 0)
    l64 = lax.broadcasted_iota(jnp.int32, (C, LANES), 1)
    eye_pack = jnp.where((l64 & (C - 1)) == r64, 1.0, 0.0).astype(F32)
    r128 = lax.broadcasted_iota(jnp.int32, (LANES, 2 * LANES), 0)
    l128 = lax.broadcasted_iota(jnp.int32, (LANES, 2 * LANES), 1) & (LANES - 1)
    same_head = (r128 < HEAD_DIM) == (l128 < HEAD_DIM)
    diag128 = (lax.broadcasted_iota(jnp.int32, (LANES, LANES), 0)
               == lax.broadcasted_iota(jnp.int32, (LANES, LANES), 1))
    zeros_sq = jnp.zeros((LANES, LANES), F32)
    zeros_c = jnp.zeros((C, LANES), F32)

    def chunk_body(c, carry):
        r0 = pl.multiple_of(c * C, C)
        rows = pl.ds(r0, C)
        pc = ps_ref[rows, :]
        W = RWKV_WIDTH
        r = pc[:, 0:W]
        k = pc[:, W:2 * W]
        v = pc[:, 2 * W:3 * W]
        wa = pc[:, 3 * W:3 * W + LANES]
        gl = pc[:, 3 * W + LANES:3 * W + 2 * LANES]
        z = w0_ref[...] + _dot(jnp.tanh(wa), w2_ref[...], HI)
        nz = -z
        w = -(jnp.maximum(nz, 0.0) + jnp.log1p(jnp.exp(-jnp.abs(nz)))) - 0.5
        lw = -jnp.exp(w)
        a = jax.nn.sigmoid(a0_ref[...] + _dot(wa, a2_ref[...], HI))
        gate_ref[rows, :] = _dot(jax.nn.sigmoid(gl), g2_ref[...], HI)
        kk = k * kk_ref[...]
        ss = _dot(kk * kk, e_ref[...], HI)
        kk = kk / jnp.maximum(jnp.sqrt(ss), 1e-12)
        k = k * (1.0 + (a - 1.0) * ka_ref[...])
        b = kk * a
        bonus_ref[rows, :] = _dot(r * k * rk_ref[...], e_ref[...], HI) * v
        cs = _dot(ltri_ref[...], lw, HI)
        cs_end = cs[C - 1:C, :]
        e_neg = jnp.exp(-cs)
        e_end = jnp.exp(cs_end - cs)
        at_all = -kk * jnp.exp(cs - lw)
        bt_all = b * e_neg
        kt_all = k * e_neg
        rt_all = r * jnp.exp(cs)
        bh_all = b * e_end
        kh_all = k * e_end
        gam_all = jnp.exp(cs_end)

        for j in range(N_PAIRS):
            sl = slice(j * LANES, (j + 1) * LANES)
            at, bt, kt, rt = at_all[:, sl], bt_all[:, sl], kt_all[:, sl], rt_all[:, sl]
            bh, kh, vv, gam = bh_all[:, sl], kh_all[:, sl], v[:, sl], gam_all[:, sl]
            a1 = _dot_nt(jnp.concatenate([at, rt], axis=0),
                         jnp.concatenate([_bd(bt), _bd(kt)], axis=0), HI)
            a1 = jnp.where(keep_a1, a1, 0.0)
            aab, aak = a1[:C, :LANES], a1[:C, LANES:]
            arb, ark = a1[C:, :LANES], a1[C:, LANES:]
            pw = aab
            tinv = eye_pack + aab
            for _ in range(5):
                pw = _dot(pw, _bd(pw), HI)
                tinv = tinv + _dot(tinv, _bd(pw), HI)
            akv = _dot(aak, _bd(vv), HI)
            wu = _dot(tinv, _bd(jnp.concatenate([at, akv], axis=1)), HI)
            wmat, umat = wu[:, :LANES], wu[:, LANES:]
            lhs_t = jnp.concatenate([bh, kh], axis=0)
            rhs = jnp.concatenate([wu, jnp.concatenate([zeros_c, vv], axis=1)], axis=0)
            mn = _dot(lhs_t.T, rhs, HI)
            mn = jnp.where(same_head, mn, 0.0)
            m_bd = mn[:, :LANES] + jnp.where(diag128, gam, 0.0)
            n_bd = mn[:, LANES:]
            m_ref[rows, sl] = m_bd[:HEAD_DIM] + m_bd[HEAD_DIM:]
            n_ref[rows, sl] = n_bd[:HEAD_DIM] + n_bd[HEAD_DIM:]
            ro_rhs = jnp.concatenate(
                [_bd(wu), jnp.concatenate([zeros_sq, _bd(vv)], axis=1)], axis=0)
            ro = _dot(jnp.concatenate([arb, ark], axis=1), ro_rhs, HI)
            r_ref[rows, sl] = rt + ro[:, :LANES]
            o_ref[rows, sl] = ro[:, LANES:]
        return carry

    lax.fori_loop(0, n_chunks, chunk_body, 0)


def _rwkv_prep(p_rwkv, seq_len, shift_mix, w0, w2, a0, a2, g2, k_k, k_a, r_k, tt=256):
    n = p_rwkv.shape[0]
    W = RWKV_WIDTH
    head = jnp.arange(W) // HEAD_DIM
    e_mat = (head[:, None] == head[None, :]).astype(F32)
    ltri = (jnp.arange(CHUNK)[:, None] >= jnp.arange(CHUNK)[None, :]).astype(F32)
    w2p = jnp.concatenate([w2.astype(F32), jnp.zeros((AAA_LORA, W), F32)], axis=0)
    a2p = jnp.concatenate([jnp.zeros((DECAY_LORA, W), F32), a2.astype(F32)], axis=0)
    row = lambda i: (i, 0)
    const = lambda i: (0, 0)
    vec = lambda t: t.astype(F32).reshape(1, -1)
    out_sd = jax.ShapeDtypeStruct((n, W), F32)
    return pl.pallas_call(
        functools.partial(_rwkv_prep_kernel, tiles_per_seq=seq_len // tt, n_chunks=tt // CHUNK),
        grid=(n // tt,),
        in_specs=[pl.BlockSpec((tt, RWKV_COLS), row),
                  pl.BlockSpec((8, RWKV_COLS), lambda i: (jnp.maximum(i * (tt // 8) - 1, 0), 0)),
                  pl.BlockSpec((1, RWKV_COLS), const),
                  pl.BlockSpec((1, W), const), pl.BlockSpec((LANES, W), const),
                  pl.BlockSpec((1, W), const), pl.BlockSpec((LANES, W), const),
                  pl.BlockSpec((GATE_LORA, W), const),
                  pl.BlockSpec((1, W), const), pl.BlockSpec((1, W), const),
                  pl.BlockSpec((1, W), const),
                  pl.BlockSpec((W, W), const), pl.BlockSpec((CHUNK, CHUNK), const)],
        out_specs=[pl.BlockSpec((tt, W), row)] * 6,
        out_shape=[out_sd] * 6,
        scratch_shapes=[pltpu.VMEM((tt, RWKV_COLS), F32)],
        compiler_params=pltpu.CompilerParams(
            dimension_semantics=("parallel",), vmem_limit_bytes=VMEM_LIMIT),
        name="rwkv_prep",
    )(p_rwkv, p_rwkv, vec(shift_mix), vec(w0), w2p, vec(a0), a2p, g2.astype(F32),
      vec(k_k), vec(k_a), vec(r_k), e_mat, ltri)


def _rwkv_scan_kernel(m_ref, n_ref, r_ref, o_ref, bonus_ref, gate_ref, lnw_ref, lnb_ref, e_ref,
                      out_ref, state, obuf, *, n_batch, n_chunks):
    @pl.when(pl.program_id(0) == 0)
    def _():
        state[...] = jnp.zeros_like(state)

    C = CHUNK
    for c in range(n_chunks):
        rows = slice(c * C, (c + 1) * C)
        for b in range(n_batch):
            for j in range(N_PAIRS):
                sl = slice(j * LANES, (j + 1) * LANES)
                s0 = state[b * N_PAIRS + j]
                lhs = jnp.concatenate([m_ref[b, rows, sl], r_ref[b, rows, sl]], axis=0)
                res = _dot(lhs, _bd(s0), HI)
                state[b * N_PAIRS + j] = res[:C] + n_ref[b, rows, sl]
                obuf[b, rows, sl] = res[C:] + o_ref[b, rows, sl]

    e_mean = e_ref[...]
    for b in range(n_batch):
        o = obuf[b]
        mu = _dot(o, e_mean, HI)
        d = o - mu
        var = _dot(d * d, e_mean, HI)
        y = d * lax.rsqrt(var + GN_EPS) * lnw_ref[...] + lnb_ref[...]
        out_ref[b] = (y + bonus_ref[b]) * gate_ref[b]


def _rwkv_scan(mats, n_batch, seq_len, ln_w, ln_b, tt=256):
    W = RWKV_WIDTH
    head = jnp.arange(W) // HEAD_DIM
    e_mean = (head[:, None] == head[None, :]).astype(F32) / HEAD_DIM
    mats = [t.reshape(n_batch, seq_len, W) for t in mats]
    blk = pl.BlockSpec((n_batch, tt, W), lambda i: (0, i, 0))
    const = lambda i: (0, 0)
    out = pl.pallas_call(
        functools.partial(_rwkv_scan_kernel, n_batch=n_batch, n_chunks=tt // CHUNK),
        grid=(seq_len // tt,),
        in_specs=[blk] * 6 + [pl.BlockSpec((1, W), const), pl.BlockSpec((1, W), const),
                              pl.BlockSpec((W, W), const)],
        out_specs=blk,
        out_shape=jax.ShapeDtypeStruct((n_batch, seq_len, W), F32),
        scratch_shapes=[pltpu.VMEM((n_batch * N_PAIRS, CHUNK, LANES), F32),
                        pltpu.VMEM((n_batch, tt, W), F32)],
        compiler_params=pltpu.CompilerParams(
            dimension_semantics=("arbitrary",), vmem_limit_bytes=VMEM_LIMIT),
        name="rwkv_scan",
    )(*mats, ln_w.astype(F32).reshape(1, W), ln_b.astype(F32).reshape(1, W), e_mean)
    return out.reshape(n_batch * seq_len, W)


def _attn_kernel(sink_ref, q_ref, kp_ref, kc_ref, vp_ref, vc_ref, out_ref):
    nb = pl.program_id(1)
    BQ = WINDOW
    k2 = jnp.concatenate([kp_ref[...], kc_ref[...]], axis=0)
    v2 = jnp.concatenate([vp_ref[...], vc_ref[...]], axis=0)
    lo = _low_half(k2.shape)
    k2r = pltpu.roll(k2, HEAD_DIM, axis=1)
    v2r = pltpu.roll(v2, HEAD_DIM, axis=1)
    k_lo = [jnp.where(lo, k2, 0.0), jnp.where(lo, k2r, 0.0)]
    k_hi = [jnp.where(lo, 0.0, k2r), jnp.where(lo, 0.0, k2)]
    v_lo = [jnp.where(lo, v2, 0.0), jnp.where(lo, v2r, 0.0)]
    v_hi = [jnp.where(lo, 0.0, v2r), jnp.where(lo, 0.0, v2)]

    qi = lax.broadcasted_iota(jnp.int32, (BQ, 2 * BQ), 0)
    kj = lax.broadcasted_iota(jnp.int32, (BQ, 2 * BQ), 1)
    dist = qi + BQ - kj
    k_min = jnp.where(nb > 0, 0, BQ)
    valid = (dist >= 0) & (dist < WINDOW) & (kj >= k_min)
    scale = HEAD_DIM ** -0.5

    for j in range(N_Q_HEADS // 2):
        g = (2 * j) // (N_Q_HEADS // N_KV_HEADS)
        q = q_ref[:, j * LANES:(j + 1) * LANES]
        s = _dot_nt(q, jnp.concatenate([k_lo[g], k_hi[g]], axis=0)) * scale
        probs = []
        for hh in range(2):
            sink = sink_ref[2 * j + hh]
            sh = jnp.where(valid, s[:, hh * 2 * BQ:(hh + 1) * 2 * BQ], -jnp.inf)
            m = jnp.maximum(jnp.max(sh, axis=-1, keepdims=True), sink)
            e = jnp.exp(sh - m)
            denom = jnp.sum(e, axis=-1, keepdims=True) + jnp.exp(sink - m)
            probs.append(e / denom)
        pv = _dot(jnp.concatenate(probs, axis=1),
                  jnp.concatenate([v_lo[g], v_hi[g]], axis=0))
        out_ref[:, j * LANES:(j + 1) * LANES] = pv


def _attn(p_attn, n_batch, seq_len, sinks):
    n = p_attn.shape[0]
    nb = seq_len // WINDOW
    kcol = ATTN_WIDTH // LANES
    vcol = kcol + KV_WIDTH // LANES
    cur = lambda b, i: b * nb + i
    prev = lambda b, i: b * nb + jnp.maximum(i - 1, 0)
    return pl.pallas_call(
        _attn_kernel,
        grid=(n_batch, nb),
        in_specs=[pl.BlockSpec(memory_space=pltpu.SMEM),
                  pl.BlockSpec((WINDOW, ATTN_WIDTH), lambda b, i: (cur(b, i), 0)),
                  pl.BlockSpec((WINDOW, KV_WIDTH), lambda b, i: (prev(b, i), kcol)),
                  pl.BlockSpec((WINDOW, KV_WIDTH), lambda b, i: (cur(b, i), kcol)),
                  pl.BlockSpec((WINDOW, KV_WIDTH), lambda b, i: (prev(b, i), vcol)),
                  pl.BlockSpec((WINDOW, KV_WIDTH), lambda b, i: (cur(b, i), vcol))],
        out_specs=pl.BlockSpec((WINDOW, ATTN_WIDTH), lambda b, i: (cur(b, i), 0)),
        out_shape=jax.ShapeDtypeStruct((n, ATTN_WIDTH), F32),
        compiler_params=pltpu.CompilerParams(
            dimension_semantics=("parallel", "arbitrary"), vmem_limit_bytes=VMEM_LIMIT),
        name="swa_attn",
    )(sinks.astype(F32), p_attn, p_attn, p_attn, p_attn, p_attn)


def kernel(x, norm_ffn1, ffn1_gate, ffn1_up, ffn1_down, norm_mix, w_in, b_in_attn, rwkv_shift_mix, rwkv_w0, rwkv_w2, rwkv_a0, rwkv_a2, rwkv_g2, rwkv_k_k, rwkv_k_a, rwkv_r_k, rwkv_ln_w, rwkv_ln_b, attn_sinks, w_out, norm_ffn2, ffn2_gate, ffn2_up, ffn2_down, norm_final):
    n_batch, seq_len, d = x.shape
    depth = w_in.shape[0]
    h = x.reshape(n_batch * seq_len, d)
    for l in range(depth):
        h = _ffn(h, norm_ffn1[l], ffn1_gate[l], ffn1_up[l], ffn1_down[l])
        p_rwkv, p_attn = _in_proj(h, norm_mix[l], w_in[l], b_in_attn[l])
        mats = _rwkv_prep(p_rwkv, seq_len, rwkv_shift_mix[l], rwkv_w0[l], rwkv_w2[l],
                          rwkv_a0[l], rwkv_a2[l], rwkv_g2[l], rwkv_k_k[l], rwkv_k_a[l],
                          rwkv_r_k[l])
        o_rwkv = _rwkv_scan(mats, n_batch, seq_len, rwkv_ln_w[l], rwkv_ln_b[l])
        o_attn = _attn(p_attn, n_batch, seq_len, attn_sinks[l])
        h = _ffn(h, norm_ffn2[l], ffn2_gate[l], ffn2_up[l], ffn2_down[l],
                 mix=(o_rwkv, o_attn, w_out[l]),
                 final_norm=norm_final if l == depth - 1 else None)
    return h.reshape(n_batch, seq_len, d)
```

```python
import functools

import jax
import jax.numpy as jnp
from jax import lax
from jax.experimental import pallas as pl
from jax.experimental.pallas import tpu as pltpu

F32 = jnp.float32
BF16 = jnp.bfloat16
HI = lax.Precision.HIGHEST

HEAD_DIM = 64
LANES = 128
CHUNK = 64
N_RWKV_HEADS = 8
RWKV_WIDTH = N_RWKV_HEADS * HEAD_DIM
N_PAIRS = RWKV_WIDTH // LANES
DECAY_LORA = 64
AAA_LORA = 64
GATE_LORA = 128
RWKV_COLS = 3 * RWKV_WIDTH + DECAY_LORA + AAA_LORA + GATE_LORA
N_Q_HEADS = 8
N_KV_HEADS = 2
ATTN_WIDTH = N_Q_HEADS * HEAD_DIM
KV_WIDTH = N_KV_HEADS * HEAD_DIM
ATTN_COLS = ATTN_WIDTH + 2 * KV_WIDTH
WINDOW = 128
GN_EPS = 64e-5
NORM_EPS = 1e-5
VMEM_LIMIT = 56 * 1024 * 1024


def _dot(a, b, prec=None):
    return jnp.dot(a, b, preferred_element_type=F32, precision=prec)


def _dot_nt(a, b, prec=None):
    return lax.dot_general(a, b, (((1,), (1,)), ((), ())),
                           preferred_element_type=F32, precision=prec)


def _rms(x, g):
    ms = jnp.mean(x * x, axis=-1, keepdims=True)
    return x * lax.rsqrt(ms + NORM_EPS) * g


def _low_half(shape):
    return (lax.broadcasted_iota(jnp.int32, shape, 1) & (LANES - 1)) < HEAD_DIM


def _bd(x):
    lo = _low_half(x.shape)
    return jnp.concatenate([jnp.where(lo, x, 0.0), jnp.where(lo, 0.0, x)], axis=0)


def _ffn_kernel(*refs, has_mix, final_norm, nf):
    refs = list(refs)
    x_ref = refs.pop(0)
    if has_mix:
        orw_ref, oat_ref, wor_ref, woa_ref = refs[:4]
        refs = refs[4:]
    g_ref, wg_ref, wu_ref, wd_ref = refs[:4]
    refs = refs[4:]
    if final_norm:
        gf_ref = refs.pop(0)
    out_ref, xres, h_s, acc = refs
    j = pl.program_id(1)

    @pl.when(j == 0)
    def _():
        x = x_ref[...]
        if has_mix:
            x = (x + _dot(orw_ref[...].astype(BF16), wor_ref[...])
                 + _dot(oat_ref[...].astype(BF16), woa_ref[...]))
        xres[...] = x
        h_s[...] = _rms(x, g_ref[...]).astype(BF16)
        acc[...] = jnp.zeros_like(acc)

    h = h_s[...]
    gate = _dot(h, wg_ref[...])
    up = _dot(h, wu_ref[...])
    act = (gate * jax.nn.sigmoid(gate) * up).astype(BF16)
    acc[...] += _dot(act, wd_ref[...])

    @pl.when(j == nf - 1)
    def _():
        y = xres[...] + 0.5 * acc[...]
        if final_norm:
            y = _rms(y, gf_ref[...])
        out_ref[...] = y


def _ffn(x, norm, w_gate, w_up, w_down, mix=None, final_norm=None, tm=1024, tf=256):
    n, d = x.shape
    f = w_gate.shape[1]
    nt, nf = n // tm, f // tf
    row = lambda i, j: (i, 0)
    const = lambda i, j: (0, 0)
    args = [x]
    specs = [pl.BlockSpec((tm, d), row)]
    if mix is not None:
        o_rwkv, o_attn, w_out = mix
        wo = w_out.astype(BF16)
        args += [o_rwkv, o_attn, wo[:RWKV_WIDTH], wo[RWKV_WIDTH:]]
        specs += [pl.BlockSpec((tm, RWKV_WIDTH), row), pl.BlockSpec((tm, ATTN_WIDTH), row),
                  pl.BlockSpec((RWKV_WIDTH, d), const), pl.BlockSpec((ATTN_WIDTH, d), const)]
    args += [norm.reshape(1, d), w_gate.astype(BF16), w_up.astype(BF16), w_down.astype(BF16)]
    specs += [pl.BlockSpec((1, d), const),
              pl.BlockSpec((d, tf), lambda i, j: (0, j)),
              pl.BlockSpec((d, tf), lambda i, j: (0, j)),
              pl.BlockSpec((tf, d), lambda i, j: (j, 0))]
    if final_norm is not None:
        args.append(final_norm.reshape(1, d))
        specs.append(pl.BlockSpec((1, d), const))
    return pl.pallas_call(
        functools.partial(_ffn_kernel, has_mix=mix is not None,
                          final_norm=final_norm is not None, nf=nf),
        grid=(nt, nf),
        in_specs=specs,
        out_specs=pl.BlockSpec((tm, d), row),
        out_shape=jax.ShapeDtypeStruct((n, d), F32),
        scratch_shapes=[pltpu.VMEM((tm, d), F32), pltpu.VMEM((tm, d), BF16),
                        pltpu.VMEM((tm, d), F32)],
        compiler_params=pltpu.CompilerParams(
            dimension_semantics=("parallel", "arbitrary"), vmem_limit_bytes=VMEM_LIMIT),
        name="ffn_mix" if mix is not None else "ffn",
    )(*args)


def _inproj_kernel(x_ref, g_ref, w_ref, b_ref, prw_ref, pat_ref):
    h = _rms(x_ref[...], g_ref[...]).astype(BF16)
    p = _dot(h, w_ref[...]) + b_ref[...]
    prw_ref[...] = p[:, :RWKV_COLS]
    pat_ref[...] = p[:, RWKV_COLS:]


def _in_proj(x, norm, w_in, b_attn, tm=512):
    n, d = x.shape
    cols = w_in.shape[1]
    bias = jnp.concatenate([jnp.zeros((RWKV_COLS,), F32), b_attn.astype(F32)]).reshape(1, cols)
    row = lambda i: (i, 0)
    const = lambda i: (0, 0)
    return pl.pallas_call(
        _inproj_kernel,
        grid=(n // tm,),
        in_specs=[pl.BlockSpec((tm, d), row), pl.BlockSpec((1, d), const),
                  pl.BlockSpec((d, cols), const), pl.BlockSpec((1, cols), const)],
        out_specs=[pl.BlockSpec((tm, RWKV_COLS), row), pl.BlockSpec((tm, ATTN_COLS), row)],
        out_shape=[jax.ShapeDtypeStruct((n, RWKV_COLS), F32),
                   jax.ShapeDtypeStruct((n, ATTN_COLS), F32)],
        compiler_params=pltpu.CompilerParams(
            dimension_semantics=("parallel",), vmem_limit_bytes=VMEM_LIMIT),
        name="in_proj",
    )(x, norm.reshape(1, d), w_in.astype(BF16), bias)


def _rwkv_prep_kernel(p_ref, pprev_ref, mix_ref, w0_ref, w2_ref, a0_ref, a2_ref, g2_ref,
                      kk_ref, ka_ref, rk_ref, e_ref, ltri_ref,
                      m_ref, n_ref, r_ref, o_ref, bonus_ref, gate_ref, ps_ref,
                      *, tiles_per_seq, n_chunks):
    i = pl.program_id(0)
    p = p_ref[...]
    prev_row = jnp.where(i % tiles_per_seq == 0, 0.0, pprev_ref[7:8, :])
    row = lax.broadcasted_iota(jnp.int32, p.shape, 0)
    p_prev = jnp.where(row == 0, prev_row, pltpu.roll(p, 1, axis=0))
    ps_ref[...] = p + (p_prev - p) * mix_ref[...]

    C = CHUNK
    rowi = lax.broadcasted_iota(jnp.int32, (2 * C, 2 * LANES), 0)
    lane_s = lax.broadcasted_iota(jnp.int32, (2 * C, 2 * LANES), 1) & (C - 1)
    t_idx = rowi & (C - 1)
    keep_a1 = lane_s < t_idx + jnp.where(rowi < C, 0, 1)
    r64 = lax.broadcasted_iota(jnp.int32, (C, LANES), 0)
    l64 = lax.broadcasted_iota(jnp.int32, (C, LANES), 1)
    eye_pack = jnp.where((l64 & (C - 1)) == r64, 1.0, 0.0).astype(F32)
    r128 = lax.broadcasted_iota(jnp.int32, (LANES, 2 * LANES), 0)
    l128 = lax.broadcasted_iota(jnp.int32, (LANES, 2 * LANES), 1) & (LANES - 1)
    same_head = (r128 < HEAD_DIM) == (l128 < HEAD_DIM)
    diag128 = (lax.broadcasted_iota(jnp.int32, (LANES, LANES), 0)
               == lax.broadcasted_iota(jnp.int32, (LANES, LANES), 1))
    zeros_sq = jnp.zeros((LANES, LANES), F32)
    zeros_c = jnp.zeros((C, LANES), F32)

    def chunk_body(c, carry):
        r0 = pl.multiple_of(c * C, C)
        rows = pl.ds(r0, C)
        pc = ps_ref[rows, :]
        W = RWKV_WIDTH
        r = pc[:, 0:W]
        k = pc[:, W:2 * W]
        v = pc[:, 2 * W:3 * W]
        wa = pc[:, 3 * W:3 * W + LANES]
        gl = pc[:, 3 * W + LANES:3 * W + 2 * LANES]
        z = w0_ref[...] + _dot(jnp.tanh(wa), w2_ref[...], HI)
        nz = -z
        w = -(jnp.maximum(nz, 0.0) + jnp.log1p(jnp.exp(-jnp.abs(nz)))) - 0.5
        lw = -jnp.exp(w)
        a = jax.nn.sigmoid(a0_ref[...] + _dot(wa, a2_ref[...], HI))
        gate_ref[rows, :] = _dot(jax.nn.sigmoid(gl), g2_ref[...], HI)
        kk = k * kk_ref[...]
        ss = _dot(kk * kk, e_ref[...], HI)
        kk = kk / jnp.maximum(jnp.sqrt(ss), 1e-12)
        k = k * (1.0 + (a - 1.0) * ka_ref[...])
        b = kk * a
        bonus_ref[rows, :] = _dot(r * k * rk_ref[...], e_ref[...], HI) * v
        cs = _dot(ltri_ref[...], lw, HI)
        cs_end = cs[C - 1:C, :]
        e_neg = jnp.exp(-cs)
        e_end = jnp.exp(cs_end - cs)
        at_all = -kk * jnp.exp(cs - lw)
        bt_all = b * e_neg
        kt_all = k * e_neg
        rt_all = r * jnp.exp(cs)
        bh_all = b * e_end
        kh_all = k * e_end
        gam_all = jnp.exp(cs_end)

        for j in range(N_PAIRS):
            sl = slice(j * LANES, (j + 1) * LANES)
            at, bt, kt, rt = at_all[:, sl], bt_all[:, sl], kt_all[:, sl], rt_all[:, sl]
            bh, kh, vv, gam = bh_all[:, sl], kh_all[:, sl], v[:, sl], gam_all[:, sl]
            a1 = _dot_nt(jnp.concatenate([at, rt], axis=0),
                         jnp.concatenate([_bd(bt), _bd(kt)], axis=0), HI)
            a1 = jnp.where(keep_a1, a1, 0.0)
            aab, aak = a1[:C, :LANES], a1[:C, LANES:]
            arb, ark = a1[C:, :LANES], a1[C:, LANES:]
            pw = aab
            tinv = eye_pack + aab
            for _ in range(5):
                pw = _dot(pw, _bd(pw), HI)
                tinv = tinv + _dot(tinv, _bd(pw), HI)
            akv = _dot(aak, _bd(vv), HI)
            wu = _dot(tinv, _bd(jnp.concatenate([at, akv], axis=1)), HI)
            wmat, umat = wu[:, :LANES], wu[:, LANES:]
            lhs_t = jnp.concatenate([bh, kh], axis=0)
            rhs = jnp.concatenate([wu, jnp.concatenate([zeros_c, vv], axis=1)], axis=0)
            mn = _dot(lhs_t.T, rhs, HI)
            mn = jnp.where(same_head, mn, 0.0)
            m_bd = mn[:, :LANES] + jnp.where(diag128, gam, 0.0)
            n_bd = mn[:, LANES:]
            m_ref[rows, sl] = m_bd[:HEAD_DIM] + m_bd[HEAD_DIM:]
            n_ref[rows, sl] = n_bd[:HEAD_DIM] + n_bd[HEAD_DIM:]
            ro_rhs = jnp.concatenate(
                [_bd(wu), jnp.concatenate([zeros_sq, _bd(vv)], axis=1)], axis=0)
            ro = _dot(jnp.concatenate([arb, ark], axis=1), ro_rhs, HI)
            r_ref[rows, sl] = rt + ro[:, :LANES]
            o_ref[rows, sl] = ro[:, LANES:]
        return carry

    lax.fori_loop(0, n_chunks, chunk_body, 0)


def _rwkv_prep(p_rwkv, seq_len, shift_mix, w0, w2, a0, a2, g2, k_k, k_a, r_k, tt=256):
    n = p_rwkv.shape[0]
    W = RWKV_WIDTH
    head = jnp.arange(W) // HEAD_DIM
    e_mat = (head[:, None] == head[None, :]).astype(F32)
    ltri = (jnp.arange(CHUNK)[:, None] >= jnp.arange(CHUNK)[None, :]).astype(F32)
    w2p = jnp.concatenate([w2.astype(F32), jnp.zeros((AAA_LORA, W), F32)], axis=0)
    a2p = jnp.concatenate([jnp.zeros((DECAY_LORA, W), F32), a2.astype(F32)], axis=0)
    row = lambda i: (i, 0)
    const = lambda i: (0, 0)
    vec = lambda t: t.astype(F32).reshape(1, -1)
    out_sd = jax.ShapeDtypeStruct((n, W), F32)
    return pl.pallas_call(
        functools.partial(_rwkv_prep_kernel, tiles_per_seq=seq_len // tt, n_chunks=tt // CHUNK),
        grid=(n // tt,),
        in_specs=[pl.BlockSpec((tt, RWKV_COLS), row),
                  pl.BlockSpec((8, RWKV_COLS), lambda i: (jnp.maximum(i * (tt // 8) - 1, 0), 0)),
                  pl.BlockSpec((1, RWKV_COLS), const),
                  pl.BlockSpec((1, W), const), pl.BlockSpec((LANES, W), const),
                  pl.BlockSpec((1, W), const), pl.BlockSpec((LANES, W), const),
                  pl.BlockSpec((GATE_LORA, W), const),
                  pl.BlockSpec((1, W), const), pl.BlockSpec((1, W), const),
                  pl.BlockSpec((1, W), const),
                  pl.BlockSpec((W, W), const), pl.BlockSpec((CHUNK, CHUNK), const)],
        out_specs=[pl.BlockSpec((tt, W), row)] * 6,
        out_shape=[out_sd] * 6,
        scratch_shapes=[pltpu.VMEM((tt, RWKV_COLS), F32)],
        compiler_params=pltpu.CompilerParams(
            dimension_semantics=("parallel",), vmem_limit_bytes=VMEM_LIMIT),
        name="rwkv_prep",
    )(p_rwkv, p_rwkv, vec(shift_mix), vec(w0), w2p, vec(a0), a2p, g2.astype(F32),
      vec(k_k), vec(k_a), vec(r_k), e_mat, ltri)


def _split2(x):
    hi = x.astype(BF16)
    return hi, (x - hi.astype(F32)).astype(BF16)


def _seg_sum(x, e128):
    outs = []
    for j in range(x.shape[1] // LANES):
        hi, lo = _split2(x[:, j * LANES:(j + 1) * LANES])
        outs.append(_dot(hi, e128) + _dot(lo, e128))
    return jnp.concatenate(outs, axis=1)


def _rwkv_prep2_kernel(p_ref, pprev_ref, mix_ref, w0_ref, w2_ref, a0_ref, a2_ref, g2_ref,
                       kk_ref, ka_ref, rk_ref, e_ref, lcum_ref,
                       m_ref, n_ref, r_ref, o_ref, bonus_ref, gate_ref,
                       at_s, bt_s, kt_s, rt_s, bh_s, kh_s, v_s, gam_s,
                       *, tiles_per_seq, n_chunks):
    C = CHUNK
    W = RWKV_WIDTH
    i = pl.program_id(0)
    p = p_ref[...]
    prev_row = jnp.where(i % tiles_per_seq == 0, 0.0, pprev_ref[7:8, :])
    row = lax.broadcasted_iota(jnp.int32, p.shape, 0)
    p_prev = jnp.where(row == 0, prev_row, pltpu.roll(p, 1, axis=0))
    ps = p + (p_prev - p) * mix_ref[...]
    r = ps[:, 0:W]
    k = ps[:, W:2 * W]
    v = ps[:, 2 * W:3 * W]
    wa = ps[:, 3 * W:3 * W + LANES]
    gl = ps[:, 3 * W + LANES:3 * W + 2 * LANES]
    z = w0_ref[...] + _dot(jnp.tanh(wa).astype(BF16), w2_ref[...])
    nz = -z
    w = -(jnp.maximum(nz, 0.0) + jnp.log1p(jnp.exp(-jnp.abs(nz)))) - 0.5
    lw = -jnp.exp(w)
    a = jax.nn.sigmoid(a0_ref[...] + _dot(wa.astype(BF16), a2_ref[...]))
    gate_ref[...] = _dot(jax.nn.sigmoid(gl).astype(BF16), g2_ref[...])
    e128 = e_ref[...]
    kk = k * kk_ref[...]
    kk = kk / jnp.maximum(jnp.sqrt(_seg_sum(kk * kk, e128)), 1e-12)
    k = k * (1.0 + (a - 1.0) * ka_ref[...])
    b = kk * a
    bonus_ref[...] = _seg_sum(r * k * rk_ref[...], e128) * v
    v_s[...] = v.astype(BF16)
    lcum = lcum_ref[...]
    lw_hi, lw_rest = lw.astype(BF16), None
    lw_rest = lw - lw_hi.astype(F32)
    lw_mid, lw_lo = _split2(lw_rest)
    cs = _dot(lcum, lw_hi) + _dot(lcum, lw_mid) + _dot(lcum, lw_lo)
    for c in range(n_chunks):
        rows = slice(c * C, (c + 1) * C)
        cs_c = cs[rows]
        cs_end = cs[(c + 1) * C - 1:(c + 1) * C, :]
        e_neg = jnp.exp(-cs_c)
        e_end = jnp.exp(cs_end - cs_c)
        rt = r[rows] * jnp.exp(cs_c)
        at_s[rows, :] = (-kk[rows] * jnp.exp(cs_c - lw[rows])).astype(BF16)
        bt_s[rows, :] = (b[rows] * e_neg).astype(BF16)
        kt_s[rows, :] = (k[rows] * e_neg).astype(BF16)
        rt_s[rows, :] = rt.astype(BF16)
        r_ref[rows, :] = rt
        bh_s[rows, :] = b[rows] * e_end
        kh_s[rows, :] = k[rows] * e_end
        gam_s[c:c + 1, :] = jnp.exp(cs_end)

    rowi = lax.broadcasted_iota(jnp.int32, (2 * C, 2 * LANES), 0)
    lane_s = lax.broadcasted_iota(jnp.int32, (2 * C, 2 * LANES), 1) & (C - 1)
    keep_a1 = lane_s < (rowi & (C - 1)) + jnp.where(rowi < C, 0, 1)
    r64 = lax.broadcasted_iota(jnp.int32, (C, LANES), 0)
    l64 = lax.broadcasted_iota(jnp.int32, (C, LANES), 1)
    eye_pack = jnp.where((l64 & (C - 1)) == r64, 1.0, 0.0).astype(F32)
    r128 = lax.broadcasted_iota(jnp.int32, (LANES, 2 * LANES), 0)
    l128 = lax.broadcasted_iota(jnp.int32, (LANES, 2 * LANES), 1) & (LANES - 1)
    same_head = (r128 < HEAD_DIM) == (l128 < HEAD_DIM)
    diag128 = (lax.broadcasted_iota(jnp.int32, (LANES, LANES), 0)
               == lax.broadcasted_iota(jnp.int32, (LANES, LANES), 1))
    zeros_sq = jnp.zeros((LANES, LANES), BF16)
    zeros_c = jnp.zeros((C, LANES), BF16)

    def prepare(chains):
        n = len(chains)
        ld = lambda ref: [ref[c * C:(c + 1) * C, j * LANES:(j + 1) * LANES] for c, j in chains]
        at, bt, kt, rt, vv = ld(at_s), ld(bt_s), ld(kt_s), ld(rt_s), ld(v_s)
        a1 = [jnp.where(keep_a1,
                        _dot_nt(jnp.concatenate([at[i], rt[i]], axis=0),
                                jnp.concatenate([_bd(bt[i]), _bd(kt[i])], axis=0)), 0.0)
              for i in range(n)]
        aab = [x[:C, :LANES] for x in a1]
        akv = [_dot(x[:C, LANES:].astype(BF16), _bd(y)).astype(BF16) for x, y in zip(a1, vv)]
        pwb = [x.astype(BF16) for x in aab]
        tinv = [eye_pack + x for x in aab]
        for _ in range(5):
            pwb = [_dot(x, _bd(x)).astype(BF16) for x in pwb]
            tinv = [t + _dot(t.astype(BF16), _bd(p)) for t, p in zip(tinv, pwb)]
        wu = [_dot(t.astype(BF16), _bd(jnp.concatenate([x, y], axis=1))).astype(BF16)
              for t, x, y in zip(tinv, at, akv)]
        for i, (c, j) in enumerate(chains):
            rows = slice(c * C, (c + 1) * C)
            sl = slice(j * LANES, (j + 1) * LANES)
            lhs_t = jnp.concatenate([bh_s[rows, sl], kh_s[rows, sl]], axis=0)
            rhs = jnp.concatenate([wu[i], jnp.concatenate([zeros_c, vv[i]], axis=1)], axis=0)
            mn = _dot(lhs_t.T.astype(BF16), rhs)
            mn = jnp.where(same_head, mn, 0.0)
            m_bd = mn[:, :LANES] + jnp.where(diag128, gam_s[c:c + 1, sl], 0.0)
            n_bd = mn[:, LANES:]
            m_ref[rows, sl] = m_bd[:HEAD_DIM] + m_bd[HEAD_DIM:]
            n_ref[rows, sl] = n_bd[:HEAD_DIM] + n_bd[HEAD_DIM:]
            ro_rhs = jnp.concatenate(
                [_bd(wu[i]), jnp.concatenate([zeros_sq, _bd(vv[i])], axis=1)], axis=0)
            ro = _dot(a1[i][C:].astype(BF16), ro_rhs)
            r_ref[rows, sl] = r_ref[rows, sl] + ro[:, :LANES]
            o_ref[rows, sl] = ro[:, LANES:]

    chains = [(c, j) for c in range(n_chunks) for j in range(N_PAIRS)]
    group = 8
    for g0 in range(0, len(chains), group):
        prepare(chains[g0:g0 + group])


def _rwkv_prep2(p_rwkv, seq_len, shift_mix, w0, w2, a0, a2, g2, k_k, k_a, r_k, tt=256):
    n = p_rwkv.shape[0]
    W = RWKV_WIDTH
    n_chunks = tt // CHUNK
    lane_head = jnp.arange(LANES) // HEAD_DIM
    e128 = (lane_head[:, None] == lane_head[None, :]).astype(BF16)
    t = jnp.arange(tt)
    lcum = ((t[:, None] >= t[None, :]) & (t[:, None] // CHUNK == t[None, :] // CHUNK)).astype(BF16)
    w2p = jnp.concatenate([w2, jnp.zeros((AAA_LORA, W), w2.dtype)], axis=0).astype(BF16)
    a2p = jnp.concatenate([jnp.zeros((DECAY_LORA, W), a2.dtype), a2], axis=0).astype(BF16)
    row = lambda i: (i, 0)
    const = lambda i: (0, 0)
    vec = lambda t: t.astype(F32).reshape(1, -1)
    out_sd = jax.ShapeDtypeStruct((n, W), F32)
    return pl.pallas_call(
        functools.partial(_rwkv_prep2_kernel, tiles_per_seq=seq_len // tt, n_chunks=n_chunks),
        grid=(n // tt,),
        in_specs=[pl.BlockSpec((tt, RWKV_COLS), row),
                  pl.BlockSpec((8, RWKV_COLS), lambda i: (jnp.maximum(i * (tt // 8) - 1, 0), 0)),
                  pl.BlockSpec((1, RWKV_COLS), const),
                  pl.BlockSpec((1, W), const), pl.BlockSpec((LANES, W), const),
                  pl.BlockSpec((1, W), const), pl.BlockSpec((LANES, W), const),
                  pl.BlockSpec((GATE_LORA, W), const),
                  pl.BlockSpec((1, W), const), pl.BlockSpec((1, W), const),
                  pl.BlockSpec((1, W), const),
                  pl.BlockSpec((LANES, LANES), const), pl.BlockSpec((tt, tt), const)],
        out_specs=[pl.BlockSpec((tt, W), row)] * 6,
        out_shape=[out_sd] * 6,
        scratch_shapes=[pltpu.VMEM((tt, W), BF16)] * 4 + [pltpu.VMEM((tt, W), F32)] * 2
                       + [pltpu.VMEM((tt, W), BF16), pltpu.VMEM((8, W), F32)],
        compiler_params=pltpu.CompilerParams(
            dimension_semantics=("parallel",), vmem_limit_bytes=VMEM_LIMIT),
        name="rwkv_prep",
    )(p_rwkv, p_rwkv, vec(shift_mix), vec(w0), w2p, vec(a0), a2p, g2.astype(BF16),
      vec(k_k), vec(k_a), vec(r_k), e128, lcum)


def _rwkv_scan_kernel(m_ref, n_ref, r_ref, o_ref, bonus_ref, gate_ref, lnw_ref, lnb_ref, e_ref,
                      out_ref, state, obuf, *, n_batch, n_chunks):
    @pl.when(pl.program_id(0) == 0)
    def _():
        state[...] = jnp.zeros_like(state)

    C = CHUNK
    for c in range(n_chunks):
        rows = slice(c * C, (c + 1) * C)
        for b in range(n_batch):
            for j in range(N_PAIRS):
                sl = slice(j * LANES, (j + 1) * LANES)
                s0 = state[b * N_PAIRS + j]
                lhs = jnp.concatenate([m_ref[b, rows, sl], r_ref[b, rows, sl]], axis=0)
                res = _dot(lhs.astype(BF16), _bd(s0.astype(BF16)))
                state[b * N_PAIRS + j] = res[:C] + n_ref[b, rows, sl]
                obuf[b, rows, sl] = res[C:] + o_ref[b, rows, sl]

    e_mean = e_ref[...]
    for b in range(n_batch):
        o = obuf[b]
        d = o - _seg_sum(o, e_mean)
        var = _seg_sum(d * d, e_mean)
        y = d * lax.rsqrt(var + GN_EPS) * lnw_ref[...] + lnb_ref[...]
        out_ref[b] = (y + bonus_ref[b]) * gate_ref[b]


def _rwkv_scan(mats, n_batch, seq_len, ln_w, ln_b, tt=256):
    W = RWKV_WIDTH
    lane_head = jnp.arange(LANES) // HEAD_DIM
    e_mean = ((lane_head[:, None] == lane_head[None, :]).astype(F32) / HEAD_DIM).astype(BF16)
    mats = [t.reshape(n_batch, seq_len, W) for t in mats]
    blk = pl.BlockSpec((n_batch, tt, W), lambda i: (0, i, 0))
    const = lambda i: (0, 0)
    out = pl.pallas_call(
        functools.partial(_rwkv_scan_kernel, n_batch=n_batch, n_chunks=tt // CHUNK),
        grid=(seq_len // tt,),
        in_specs=[blk] * 6 + [pl.BlockSpec((1, W), const), pl.BlockSpec((1, W), const),
                              pl.BlockSpec((LANES, LANES), const)],
        out_specs=blk,
        out_shape=jax.ShapeDtypeStruct((n_batch, seq_len, W), F32),
        scratch_shapes=[pltpu.VMEM((n_batch * N_PAIRS, CHUNK, LANES), F32),
                        pltpu.VMEM((n_batch, tt, W), F32)],
        compiler_params=pltpu.CompilerParams(
            dimension_semantics=("arbitrary",), vmem_limit_bytes=VMEM_LIMIT),
        name="rwkv_scan",
    )(*mats, ln_w.astype(F32).reshape(1, W), ln_b.astype(F32).reshape(1, W), e_mean)
    return out.reshape(n_batch * seq_len, W)


def _attn_kernel(sink_ref, q_ref, kp_ref, kc_ref, vp_ref, vc_ref, out_ref):
    nb = pl.program_id(1)
    BQ = WINDOW
    k2 = jnp.concatenate([kp_ref[...], kc_ref[...]], axis=0)
    v2 = jnp.concatenate([vp_ref[...], vc_ref[...]], axis=0)
    lo = _low_half(k2.shape)
    k2r = pltpu.roll(k2, HEAD_DIM, axis=1)
    v2r = pltpu.roll(v2, HEAD_DIM, axis=1)
    k_lo = [jnp.where(lo, k2, 0.0), jnp.where(lo, k2r, 0.0)]
    k_hi = [jnp.where(lo, 0.0, k2r), jnp.where(lo, 0.0, k2)]
    v_lo = [jnp.where(lo, v2, 0.0), jnp.where(lo, v2r, 0.0)]
    v_hi = [jnp.where(lo, 0.0, v2r), jnp.where(lo, 0.0, v2)]

    qi = lax.broadcasted_iota(jnp.int32, (BQ, 2 * BQ), 0)
    kj = lax.broadcasted_iota(jnp.int32, (BQ, 2 * BQ), 1)
    dist = qi + BQ - kj
    k_min = jnp.where(nb > 0, 0, BQ)
    valid = (dist >= 0) & (dist < WINDOW) & (kj >= k_min)
    scale = HEAD_DIM ** -0.5

    for j in range(N_Q_HEADS // 2):
        g = (2 * j) // (N_Q_HEADS // N_KV_HEADS)
        q = q_ref[:, j * LANES:(j + 1) * LANES]
        s = _dot_nt(q, jnp.concatenate([k_lo[g], k_hi[g]], axis=0)) * scale
        probs = []
        for hh in range(2):
            sink = sink_ref[2 * j + hh]
            sh = jnp.where(valid, s[:, hh * 2 * BQ:(hh + 1) * 2 * BQ], -jnp.inf)
            m = jnp.maximum(jnp.max(sh, axis=-1, keepdims=True), sink)
            e = jnp.exp(sh - m)
            denom = jnp.sum(e, axis=-1, keepdims=True) + jnp.exp(sink - m)
            probs.append(e / denom)
        pv = _dot(jnp.concatenate(probs, axis=1),
                  jnp.concatenate([v_lo[g], v_hi[g]], axis=0))
        out_ref[:, j * LANES:(j + 1) * LANES] = pv


def _attn(p_attn, n_batch, seq_len, sinks):
    n = p_attn.shape[0]
    nb = seq_len // WINDOW
    kcol = ATTN_WIDTH // LANES
    vcol = kcol + KV_WIDTH // LANES
    cur = lambda b, i: b * nb + i
    prev = lambda b, i: b * nb + jnp.maximum(i - 1, 0)
    return pl.pallas_call(
        _attn_kernel,
        grid=(n_batch, nb),
        in_specs=[pl.BlockSpec(memory_space=pltpu.SMEM),
                  pl.BlockSpec((WINDOW, ATTN_WIDTH), lambda b, i: (cur(b, i), 0)),
                  pl.BlockSpec((WINDOW, KV_WIDTH), lambda b, i: (prev(b, i), kcol)),
                  pl.BlockSpec((WINDOW, KV_WIDTH), lambda b, i: (cur(b, i), kcol)),
                  pl.BlockSpec((WINDOW, KV_WIDTH), lambda b, i: (prev(b, i), vcol)),
                  pl.BlockSpec((WINDOW, KV_WIDTH), lambda b, i: (cur(b, i), vcol))],
        out_specs=pl.BlockSpec((WINDOW, ATTN_WIDTH), lambda b, i: (cur(b, i), 0)),
        out_shape=jax.ShapeDtypeStruct((n, ATTN_WIDTH), F32),
        compiler_params=pltpu.CompilerParams(
            dimension_semantics=("parallel", "arbitrary"), vmem_limit_bytes=VMEM_LIMIT),
        name="swa_attn",
    )(sinks.astype(F32), p_attn, p_attn, p_attn, p_attn, p_attn)


def kernel(x, norm_ffn1, ffn1_gate, ffn1_up, ffn1_down, norm_mix, w_in, b_in_attn, rwkv_shift_mix, rwkv_w0, rwkv_w2, rwkv_a0, rwkv_a2, rwkv_g2, rwkv_k_k, rwkv_k_a, rwkv_r_k, rwkv_ln_w, rwkv_ln_b, attn_sinks, w_out, norm_ffn2, ffn2_gate, ffn2_up, ffn2_down, norm_final):
    n_batch, seq_len, d = x.shape
    depth = w_in.shape[0]
    h = x.reshape(n_batch * seq_len, d)
    for l in range(depth):
        h = _ffn(h, norm_ffn1[l], ffn1_gate[l], ffn1_up[l], ffn1_down[l])
        p_rwkv, p_attn = _in_proj(h, norm_mix[l], w_in[l], b_in_attn[l])
        mats = _rwkv_prep2(p_rwkv, seq_len, rwkv_shift_mix[l], rwkv_w0[l], rwkv_w2[l],
                          rwkv_a0[l], rwkv_a2[l], rwkv_g2[l], rwkv_k_k[l], rwkv_k_a[l],
                          rwkv_r_k[l])
        o_rwkv = _rwkv_scan(mats, n_batch, seq_len, rwkv_ln_w[l], rwkv_ln_b[l])
        o_attn = _attn(p_attn, n_batch, seq_len, attn_sinks[l])
        h = _ffn(h, norm_ffn2[l], ffn2_gate[l], ffn2_up[l], ffn2_down[l],
                 mix=(o_rwkv, o_attn, w_out[l]),
                 final_norm=norm_final if l == depth - 1 else None)
    return h.reshape(n_batch, seq_len, d)
```

```python
import functools
import math

import jax
import jax.numpy as jnp
from jax import lax
from jax.experimental import pallas as pl
from jax.experimental.pallas import tpu as pltpu

F32 = jnp.float32
BF16 = jnp.bfloat16

HEAD_DIM = 64
LANES = 128
SLAB = 256
HEADS_PER_SLAB = SLAB // HEAD_DIM
CHUNK = 64
N_RWKV_HEADS = 8
RWKV_WIDTH = N_RWKV_HEADS * HEAD_DIM
N_SLABS = RWKV_WIDTH // SLAB
DECAY_LORA = 64
AAA_LORA = 64
GATE_LORA = 128
RWKV_COLS = 3 * RWKV_WIDTH + DECAY_LORA + AAA_LORA + GATE_LORA
N_Q_HEADS = 8
N_KV_HEADS = 2
ATTN_WIDTH = N_Q_HEADS * HEAD_DIM
KV_WIDTH = N_KV_HEADS * HEAD_DIM
ATTN_COLS = ATTN_WIDTH + 2 * KV_WIDTH
WINDOW = 128
GN_EPS = 64e-5
NORM_EPS = 1e-5
VMEM_LIMIT = 56 * 1024 * 1024

FFN_ROWS = 1024
FFN_COLS = 256
INPROJ_ROWS = 512
RWKV_ROWS = 256
ATTN_ROWS = 512


def _dot(a, b):
    return jnp.dot(a, b, preferred_element_type=F32)


def _dot_nt(a, b):
    return lax.dot_general(a, b, (((1,), (1,)), ((), ())), preferred_element_type=F32)


def _rms(x, g):
    ms = jnp.mean(x * x, axis=-1, keepdims=True)
    return x * lax.rsqrt(ms + NORM_EPS) * g


def _lane_head(shape, heads):
    lane = lax.broadcasted_iota(jnp.int32, shape, 1)
    return (lane >> 6) & (heads - 1)


def _block_diag(x, heads=HEADS_PER_SLAB):
    hd = _lane_head(x.shape, heads)
    return jnp.concatenate([jnp.where(hd == h, x, 0.0) for h in range(heads)], axis=0)


def _seg_sum(x, seg):
    xb = x.astype(BF16)
    return jnp.concatenate(
        [_dot(xb[:, j * LANES:(j + 1) * LANES], seg) for j in range(x.shape[1] // LANES)], axis=1)


def _resident(shape):
    return pl.BlockSpec(shape, lambda *_: (0,) * len(shape), pipeline_mode=pl.Buffered(1))


def _ffn_kernel(*refs, has_mix, final_norm, nf):
    refs = list(refs)
    x_ref = refs.pop(0)
    if has_mix:
        orw_ref, oat_ref, wor_ref, woa_ref = refs[:4]
        refs = refs[4:]
    g_ref, wg_ref, wu_ref, wd_ref = refs[:4]
    refs = refs[4:]
    if final_norm:
        gf_ref = refs.pop(0)
    out_ref, h_s, acc = refs

    x = x_ref[...]
    if has_mix:
        x = x + _dot(orw_ref[...], wor_ref[...]) + _dot(oat_ref[...], woa_ref[...])
        out_ref[...] = x
    h_s[...] = _rms(x, g_ref[...]).astype(BF16)

    def down(f):
        h = h_s[...]
        gate = _dot(h, wg_ref[f])
        up = _dot(h, wu_ref[f])
        act = (gate * jax.nn.sigmoid(gate) * up).astype(BF16)
        return _dot(act, wd_ref[f])

    acc[...] = down(0)

    def body(f, carry):
        acc[...] += down(f)
        return carry

    lax.fori_loop(1, nf, body, 0)
    res = out_ref[...] if has_mix else x_ref[...]
    y = res + 0.5 * acc[...]
    if final_norm:
        y = _rms(y, gf_ref[...])
    out_ref[...] = y


def _ffn(x, norm, w_gate, w_up, w_down, mix=None, final_norm=None):
    n, d = x.shape
    f = w_gate.shape[1]
    tm, tf = FFN_ROWS, FFN_COLS
    nf = f // tf
    row = lambda i: (i, 0)
    wg = w_gate.astype(BF16).reshape(d, nf, tf).transpose(1, 0, 2)
    wu = w_up.astype(BF16).reshape(d, nf, tf).transpose(1, 0, 2)
    wd = w_down.astype(BF16).reshape(nf, tf, d)
    args = [x]
    specs = [pl.BlockSpec((tm, d), row)]
    if mix is not None:
        o_rwkv, o_attn, w_out = mix
        wo = w_out.astype(BF16)
        args += [o_rwkv, o_attn, wo[:RWKV_WIDTH], wo[RWKV_WIDTH:]]
        specs += [pl.BlockSpec((tm, RWKV_WIDTH), row), pl.BlockSpec((tm, ATTN_WIDTH), row),
                  _resident((RWKV_WIDTH, d)), _resident((ATTN_WIDTH, d))]
    args += [norm.reshape(1, d), wg, wu, wd]
    specs += [_resident((1, d)), _resident((nf, d, tf)), _resident((nf, d, tf)),
              _resident((nf, tf, d))]
    if final_norm is not None:
        args.append(final_norm.reshape(1, d))
        specs.append(_resident((1, d)))
    return pl.pallas_call(
        functools.partial(_ffn_kernel, has_mix=mix is not None,
                          final_norm=final_norm is not None, nf=nf),
        grid=(n // tm,),
        in_specs=specs,
        out_specs=pl.BlockSpec((tm, d), row),
        out_shape=jax.ShapeDtypeStruct((n, d), F32),
        scratch_shapes=[pltpu.VMEM((tm, d), BF16), pltpu.VMEM((tm, d), F32)],
        compiler_params=pltpu.CompilerParams(
            dimension_semantics=("parallel",), vmem_limit_bytes=VMEM_LIMIT),
        name="ffn_mix" if mix is not None else "ffn",
    )(*args)


def _inproj_kernel(x_ref, g_ref, w_ref, b_ref, mix_ref, prw_ref, pat_ref, last_row,
                   *, tiles_per_seq):
    h = _rms(x_ref[...], g_ref[...]).astype(BF16)
    p = _dot(h, w_ref[...]) + b_ref[...]
    pat_ref[...] = p[:, RWKV_COLS:]
    pr = p[:, :RWKV_COLS]
    prev_row = jnp.where(pl.program_id(0) % tiles_per_seq == 0, 0.0, last_row[0:1, :])
    row = lax.broadcasted_iota(jnp.int32, pr.shape, 0)
    p_prev = jnp.where(row == 0, prev_row, pltpu.roll(pr, 1, axis=0))
    prw_ref[...] = pr + (p_prev - pr) * mix_ref[...]
    last_row[0:1, :] = pr[pr.shape[0] - 1:, :]


def _in_proj(x, seq_len, norm, w_in, b_attn, shift_mix):
    n, d = x.shape
    cols = w_in.shape[1]
    tm = INPROJ_ROWS
    bias = jnp.concatenate([jnp.zeros((RWKV_COLS,), F32), b_attn.astype(F32)]).reshape(1, cols)
    row = lambda i: (i, 0)
    return pl.pallas_call(
        functools.partial(_inproj_kernel, tiles_per_seq=seq_len // tm),
        grid=(n // tm,),
        in_specs=[pl.BlockSpec((tm, d), row), _resident((1, d)), _resident((d, cols)),
                  _resident((1, cols)), _resident((1, RWKV_COLS))],
        out_specs=[pl.BlockSpec((tm, RWKV_COLS), row), pl.BlockSpec((tm, ATTN_COLS), row)],
        out_shape=[jax.ShapeDtypeStruct((n, RWKV_COLS), F32),
                   jax.ShapeDtypeStruct((n, ATTN_COLS), F32)],
        scratch_shapes=[pltpu.VMEM((8, RWKV_COLS), F32)],
        compiler_params=pltpu.CompilerParams(
            dimension_semantics=("arbitrary",), vmem_limit_bytes=VMEM_LIMIT),
        name="in_proj",
    )(x, norm.reshape(1, d), w_in.astype(BF16), bias, shift_mix.astype(F32).reshape(1, -1))


def _rwkv_prep_kernel(ps_ref, w0_ref, w2_ref, a0_ref, a2_ref, g2_ref, kk_ref, ka_ref, rk_ref,
                      seg_ref, lcum_ref,
                      m_ref, n_ref, r_ref, o_ref, bonus_ref, gate_ref,
                      at_s, bt_s, kt_s, rt_s, v_s, bh_s, kh_s, gam_s, *, n_chunks):
    C = CHUNK
    W = RWKV_WIDTH
    ps = ps_ref[...]
    r = ps[:, 0:W]
    k = ps[:, W:2 * W]
    v = ps[:, 2 * W:3 * W]
    wa = ps[:, 3 * W:3 * W + LANES]
    gl = ps[:, 3 * W + LANES:3 * W + 2 * LANES]
    z = w0_ref[...] + _dot(jnp.tanh(wa).astype(BF16), w2_ref[...])
    lw = -math.exp(-0.5) * jax.nn.sigmoid(z)
    a = jax.nn.sigmoid(a0_ref[...] + _dot(wa.astype(BF16), a2_ref[...]))
    gate_ref[...] = _dot(jax.nn.sigmoid(gl).astype(BF16), g2_ref[...])
    seg = seg_ref[...]
    kk = k * kk_ref[...]
    kk = kk * jnp.minimum(lax.rsqrt(_seg_sum(kk * kk, seg)), 1e12)
    k = k * (1.0 + (a - 1.0) * ka_ref[...])
    b = kk * a
    bonus_ref[...] = _seg_sum(r * k * rk_ref[...], seg) * v
    v_s[...] = v.astype(BF16)
    lcum = lcum_ref[...]
    lw_hi = lw.astype(BF16)
    lw_lo = (lw - lw_hi.astype(F32)).astype(BF16)
    cs = _dot(lcum, lw_hi) + _dot(lcum, lw_lo)
    for c in range(n_chunks):
        rows = slice(c * C, (c + 1) * C)
        cs_c = cs[rows]
        gam = jnp.exp(cs[(c + 1) * C - 1:(c + 1) * C, :])
        e_neg = jnp.exp(-cs_c)
        rt = r[rows] * jnp.exp(cs_c)
        bt = b[rows] * e_neg
        kt = k[rows] * e_neg
        at_s[rows, :] = (-kk[rows] * jnp.exp(cs_c - lw[rows])).astype(BF16)
        bt_s[rows, :] = bt.astype(BF16)
        kt_s[rows, :] = kt.astype(BF16)
        rt_s[rows, :] = rt.astype(BF16)
        r_ref[rows, :] = rt
        bh_s[rows, :] = bt * gam
        kh_s[rows, :] = kt * gam
        gam_s[c:c + 1, :] = gam

    rowi = lax.broadcasted_iota(jnp.int32, (2 * C, 2 * SLAB), 0)
    lane_s = lax.broadcasted_iota(jnp.int32, (2 * C, 2 * SLAB), 1) & (C - 1)
    keep_a1 = lane_s < (rowi & (C - 1)) + jnp.where(rowi < C, 0, 1)
    eye_pack = jnp.where((lax.broadcasted_iota(jnp.int32, (C, SLAB), 1) & (C - 1))
                         == lax.broadcasted_iota(jnp.int32, (C, SLAB), 0), 1.0, 0.0).astype(F32)
    same_head = ((lax.broadcasted_iota(jnp.int32, (SLAB, 2 * SLAB), 0) >> 6)
                 == _lane_head((SLAB, 2 * SLAB), HEADS_PER_SLAB))
    diag = (lax.broadcasted_iota(jnp.int32, (SLAB, SLAB), 0)
            == lax.broadcasted_iota(jnp.int32, (SLAB, SLAB), 1))
    zeros_sq = jnp.zeros((SLAB, SLAB), BF16)
    zeros_c = jnp.zeros((C, SLAB), BF16)

    def fold(x):
        out = x[:HEAD_DIM]
        for h in range(1, HEADS_PER_SLAB):
            out = out + x[h * HEAD_DIM:(h + 1) * HEAD_DIM]
        return out

    chains = [(c, s) for c in range(n_chunks) for s in range(N_SLABS)]
    ld = lambda ref: [ref[c * C:(c + 1) * C, s * SLAB:(s + 1) * SLAB] for c, s in chains]
    at, bt, kt, rt, vv = ld(at_s), ld(bt_s), ld(kt_s), ld(rt_s), ld(v_s)
    a1 = [jnp.where(keep_a1,
                    _dot_nt(jnp.concatenate([x, y], axis=0),
                            jnp.concatenate([_block_diag(p), _block_diag(q)], axis=0)), 0.0)
          for x, y, p, q in zip(at, rt, bt, kt)]
    aab = [x[:C, :SLAB] for x in a1]
    akv = [_dot(x[:C, SLAB:].astype(BF16), _block_diag(y)).astype(BF16) for x, y in zip(a1, vv)]
    pwb = [x.astype(BF16) for x in aab]
    tinv = [eye_pack + x for x in aab]
    for _ in range(5):
        pwb = [_dot(x, _block_diag(x)).astype(BF16) for x in pwb]
        tinv = [t + _dot(t.astype(BF16), _block_diag(p)) for t, p in zip(tinv, pwb)]
    wu = [_dot(t.astype(BF16), _block_diag(jnp.concatenate([x, y], axis=1))).astype(BF16)
          for t, x, y in zip(tinv, at, akv)]
    for i, (c, s) in enumerate(chains):
        rows = slice(c * C, (c + 1) * C)
        sl = slice(s * SLAB, (s + 1) * SLAB)
        lhs_t = jnp.concatenate([bh_s[rows, sl], kh_s[rows, sl]], axis=0)
        rhs = jnp.concatenate([wu[i], jnp.concatenate([zeros_c, vv[i]], axis=1)], axis=0)
        mn = _dot(lhs_t.T.astype(BF16), rhs)
        mn = jnp.where(same_head, mn, 0.0)
        m_ref[rows, sl] = fold(mn[:, :SLAB] + jnp.where(diag, gam_s[c:c + 1, sl], 0.0)).astype(BF16)
        n_ref[rows, sl] = fold(mn[:, SLAB:])
        ro_rhs = jnp.concatenate(
            [_block_diag(wu[i]), jnp.concatenate([zeros_sq, _block_diag(vv[i])], axis=1)], axis=0)
        ro = _dot(a1[i][C:].astype(BF16), ro_rhs)
        r_ref[rows, sl] = r_ref[rows, sl] + ro[:, :SLAB]
        o_ref[rows, sl] = ro[:, SLAB:]


def _rwkv_prep(p_shift, w0, w2, a0, a2, g2, k_k, k_a, r_k):
    n = p_shift.shape[0]
    W = RWKV_WIDTH
    tt = RWKV_ROWS
    lane_head = jnp.arange(LANES) // HEAD_DIM
    seg = (lane_head[:, None] == lane_head[None, :]).astype(BF16)
    t = jnp.arange(tt)
    lcum = ((t[:, None] >= t[None, :]) & (t[:, None] // CHUNK == t[None, :] // CHUNK)).astype(BF16)
    w2p = jnp.concatenate([w2, jnp.zeros((AAA_LORA, W), w2.dtype)], axis=0).astype(BF16)
    a2p = jnp.concatenate([jnp.zeros((DECAY_LORA, W), a2.dtype), a2], axis=0).astype(BF16)
    row = lambda i: (i, 0)
    vec = lambda t: t.astype(F32).reshape(1, -1)
    f32_out = jax.ShapeDtypeStruct((n, W), F32)
    return pl.pallas_call(
        functools.partial(_rwkv_prep_kernel, n_chunks=tt // CHUNK),
        grid=(n // tt,),
        in_specs=[pl.BlockSpec((tt, RWKV_COLS), row),
                  _resident((1, W)), _resident((LANES, W)), _resident((1, W)),
                  _resident((LANES, W)), _resident((GATE_LORA, W)),
                  _resident((1, W)), _resident((1, W)), _resident((1, W)),
                  _resident((LANES, LANES)), _resident((tt, tt))],
        out_specs=[pl.BlockSpec((tt, W), row)] * 6,
        out_shape=[jax.ShapeDtypeStruct((n, W), BF16), f32_out, f32_out, f32_out, f32_out,
                   f32_out],
        scratch_shapes=[pltpu.VMEM((tt, W), BF16)] * 5 + [pltpu.VMEM((tt, W), F32)] * 2
                       + [pltpu.VMEM((8, W), F32)],
        compiler_params=pltpu.CompilerParams(
            dimension_semantics=("parallel",), vmem_limit_bytes=VMEM_LIMIT),
        name="rwkv_prep",
    )(p_shift, vec(w0), w2p, vec(a0), a2p, g2.astype(BF16), vec(k_k), vec(k_a), vec(r_k),
      seg, lcum)


def _rwkv_scan_kernel(m_ref, n_ref, r_ref, o_ref, bonus_ref, gate_ref, lnw_ref, lnb_ref, seg_ref,
                      out_ref, state, obuf, *, n_batch, n_chunks):
    @pl.when(pl.program_id(0) == 0)
    def _():
        state[...] = jnp.zeros_like(state)

    C = CHUNK
    for c in range(n_chunks):
        rows = slice(c * C, (c + 1) * C)
        for b in range(n_batch):
            for s in range(N_SLABS):
                sl = slice(s * SLAB, (s + 1) * SLAB)
                s0 = state[b * N_SLABS + s]
                lhs = jnp.concatenate([m_ref[b, rows, sl], r_ref[b, rows, sl].astype(BF16)], axis=0)
                res = _dot(lhs, _block_diag(s0.astype(BF16)))
                state[b * N_SLABS + s] = res[:C] + n_ref[b, rows, sl]
                obuf[b, rows, sl] = res[C:] + o_ref[b, rows, sl]

    seg_mean = seg_ref[...]
    for b in range(n_batch):
        o = obuf[b]
        d = o - _seg_sum(o, seg_mean)
        var = _seg_sum(d * d, seg_mean)
        y = d * lax.rsqrt(var + GN_EPS) * lnw_ref[...] + lnb_ref[...]
        out_ref[b] = ((y + bonus_ref[b]) * gate_ref[b]).astype(BF16)


def _rwkv_scan(mats, n_batch, seq_len, ln_w, ln_b):
    W = RWKV_WIDTH
    tt = RWKV_ROWS
    lane_head = jnp.arange(LANES) // HEAD_DIM
    seg_mean = ((lane_head[:, None] == lane_head[None, :]).astype(F32) / HEAD_DIM).astype(BF16)
    mats = [t.reshape(n_batch, seq_len, W) for t in mats]
    blk = pl.BlockSpec((n_batch, tt, W), lambda i: (0, i, 0))
    out = pl.pallas_call(
        functools.partial(_rwkv_scan_kernel, n_batch=n_batch, n_chunks=tt // CHUNK),
        grid=(seq_len // tt,),
        in_specs=[blk] * 6 + [_resident((1, W)), _resident((1, W)), _resident((LANES, LANES))],
        out_specs=blk,
        out_shape=jax.ShapeDtypeStruct((n_batch, seq_len, W), BF16),
        scratch_shapes=[pltpu.VMEM((n_batch * N_SLABS, CHUNK, SLAB), F32),
                        pltpu.VMEM((n_batch, tt, W), F32)],
        compiler_params=pltpu.CompilerParams(
            dimension_semantics=("arbitrary",), vmem_limit_bytes=VMEM_LIMIT),
        name="rwkv_scan",
    )(*mats, ln_w.astype(F32).reshape(1, W), ln_b.astype(F32).reshape(1, W), seg_mean)
    return out.reshape(n_batch * seq_len, W)


def _attn_kernel(sink_ref, q_ref, kp_ref, kc_ref, vp_ref, vc_ref, out_ref, *, n_blocks):
    BQ = WINDOW
    first = pl.program_id(1) == 0
    kx = jnp.concatenate([kp_ref[...], kc_ref[...]], axis=0)
    vx = jnp.concatenate([vp_ref[...], vc_ref[...]], axis=0)
    lo = _lane_head(kx.shape, 2) == 0
    kxr = pltpu.roll(kx, HEAD_DIM, axis=1)
    vxr = pltpu.roll(vx, HEAD_DIM, axis=1)
    bf = lambda t: t.astype(BF16)
    k_lo = [bf(jnp.where(lo, kx, 0.0)), bf(jnp.where(lo, kxr, 0.0))]
    k_hi = [bf(jnp.where(lo, 0.0, kxr)), bf(jnp.where(lo, 0.0, kx))]
    v_lo = [bf(jnp.where(lo, vx, 0.0)), bf(jnp.where(lo, vxr, 0.0))]
    v_hi = [bf(jnp.where(lo, 0.0, vxr)), bf(jnp.where(lo, 0.0, vx))]

    qi = lax.broadcasted_iota(jnp.int32, (BQ, 2 * BQ), 0)
    kj = lax.broadcasted_iota(jnp.int32, (BQ, 2 * BQ), 1)
    dist = qi + BQ - kj
    band = (dist >= 0) & (dist < WINDOW)
    band_first = band & (kj >= jnp.where(first, BQ, 0))
    lo_out = _lane_head((BQ, LANES), 2) == 0
    scale = HEAD_DIM ** -0.5

    for qb in range(n_blocks):
        valid = band_first if qb == 0 else band
        krows = slice(qb * BQ, (qb + 2) * BQ)
        for j in range(N_Q_HEADS // 2):
            g = (2 * j) // (N_Q_HEADS // N_KV_HEADS)
            q = (q_ref[qb * BQ:(qb + 1) * BQ, j * LANES:(j + 1) * LANES] * scale).astype(BF16)
            s = _dot_nt(q, jnp.concatenate([k_lo[g][krows], k_hi[g][krows]], axis=0))
            es, inv = [], []
            for hh in range(2):
                sink = sink_ref[2 * j + hh]
                sh = jnp.where(valid, s[:, hh * 2 * BQ:(hh + 1) * 2 * BQ], -jnp.inf)
                m = jnp.maximum(jnp.max(sh, axis=-1, keepdims=True), sink)
                e = jnp.exp(sh - m)
                inv.append(1.0 / (jnp.sum(e, axis=-1, keepdims=True) + jnp.exp(sink - m)))
                es.append(e.astype(BF16))
            pv = _dot(jnp.concatenate(es, axis=1),
                      jnp.concatenate([v_lo[g][krows], v_hi[g][krows]], axis=0))
            out_ref[qb * BQ:(qb + 1) * BQ, j * LANES:(j + 1) * LANES] = (
                pv * jnp.where(lo_out, inv[0], inv[1])).astype(BF16)


def _attn(p_attn, n_batch, seq_len, sinks):
    n = p_attn.shape[0]
    rows = ATTN_ROWS
    n_blocks = rows // WINDOW
    steps = seq_len // rows
    kcol = ATTN_WIDTH // LANES
    vcol = kcol + KV_WIDTH // LANES
    cur = lambda b, i: b * steps + i
    prev = lambda b, i: (b * steps + i) * n_blocks - jnp.minimum(i, 1)
    return pl.pallas_call(
        functools.partial(_attn_kernel, n_blocks=n_blocks),
        grid=(n_batch, steps),
        in_specs=[pl.BlockSpec(memory_space=pltpu.SMEM),
                  pl.BlockSpec((rows, ATTN_WIDTH), lambda b, i: (cur(b, i), 0)),
                  pl.BlockSpec((WINDOW, KV_WIDTH), lambda b, i: (prev(b, i), kcol)),
                  pl.BlockSpec((rows, KV_WIDTH), lambda b, i: (cur(b, i), kcol)),
                  pl.BlockSpec((WINDOW, KV_WIDTH), lambda b, i: (prev(b, i), vcol)),
                  pl.BlockSpec((rows, KV_WIDTH), lambda b, i: (cur(b, i), vcol))],
        out_specs=pl.BlockSpec((rows, ATTN_WIDTH), lambda b, i: (cur(b, i), 0)),
        out_shape=jax.ShapeDtypeStruct((n, ATTN_WIDTH), BF16),
        compiler_params=pltpu.CompilerParams(
            dimension_semantics=("parallel", "arbitrary"), vmem_limit_bytes=VMEM_LIMIT),
        name="swa_attn",
    )(sinks.astype(F32), p_attn, p_attn, p_attn, p_attn, p_attn)


def kernel(x, norm_ffn1, ffn1_gate, ffn1_up, ffn1_down, norm_mix, w_in, b_in_attn, rwkv_shift_mix, rwkv_w0, rwkv_w2, rwkv_a0, rwkv_a2, rwkv_g2, rwkv_k_k, rwkv_k_a, rwkv_r_k, rwkv_ln_w, rwkv_ln_b, attn_sinks, w_out, norm_ffn2, ffn2_gate, ffn2_up, ffn2_down, norm_final):
    n_batch, seq_len, d = x.shape
    depth = w_in.shape[0]
    h = x.reshape(n_batch * seq_len, d)
    for l in range(depth):
        h = _ffn(h, norm_ffn1[l], ffn1_gate[l], ffn1_up[l], ffn1_down[l])
        p_shift, p_attn = _in_proj(h, seq_len, norm_mix[l], w_in[l], b_in_attn[l],
                                   rwkv_shift_mix[l])
        mats = _rwkv_prep(p_shift, rwkv_w0[l], rwkv_w2[l], rwkv_a0[l], rwkv_a2[l], rwkv_g2[l],
                          rwkv_k_k[l], rwkv_k_a[l], rwkv_r_k[l])
        o_rwkv = _rwkv_scan(mats, n_batch, seq_len, rwkv_ln_w[l], rwkv_ln_b[l])
        o_attn = _attn(p_attn, n_batch, seq_len, attn_sinks[l])
        h = _ffn(h, norm_ffn2[l], ffn2_gate[l], ffn2_up[l], ffn2_down[l],
                 mix=(o_rwkv, o_attn, w_out[l]),
                 final_norm=norm_final if l == depth - 1 else None)
    return h.reshape(n_batch, seq_len, d)
```

```python
import functools
import math

import jax
import jax.numpy as jnp
from jax import lax
from jax.experimental import pallas as pl
from jax.experimental.pallas import tpu as pltpu

F32 = jnp.float32
BF16 = jnp.bfloat16

HEAD_DIM = 64
LANES = 128
SLAB = 128
HEADS_PER_SLAB = SLAB // HEAD_DIM
CHUNK = 64
N_RWKV_HEADS = 8
RWKV_WIDTH = N_RWKV_HEADS * HEAD_DIM
N_SLABS = RWKV_WIDTH // SLAB
DECAY_LORA = 64
AAA_LORA = 64
GATE_LORA = 128
RWKV_COLS = 3 * RWKV_WIDTH + DECAY_LORA + AAA_LORA + GATE_LORA
N_Q_HEADS = 8
N_KV_HEADS = 2
ATTN_WIDTH = N_Q_HEADS * HEAD_DIM
KV_WIDTH = N_KV_HEADS * HEAD_DIM
ATTN_COLS = ATTN_WIDTH + 2 * KV_WIDTH
WINDOW = 128
GN_EPS = 64e-5
NORM_EPS = 1e-5
VMEM_LIMIT = 56 * 1024 * 1024

FFN_ROWS = 1024
FFN_COLS = 256
INPROJ_ROWS = 512
RWKV_ROWS = 256
ATTN_ROWS = 512


def _dot(a, b):
    return jnp.dot(a, b, preferred_element_type=F32)


def _dot_nt(a, b):
    return lax.dot_general(a, b, (((1,), (1,)), ((), ())), preferred_element_type=F32)


def _rms(x, g):
    ms = jnp.mean(x * x, axis=-1, keepdims=True)
    return x * lax.rsqrt(ms + NORM_EPS) * g


def _lane_head(shape, heads):
    lane = lax.broadcasted_iota(jnp.int32, shape, 1)
    return (lane >> 6) & (heads - 1)


def _block_diag(x, heads=HEADS_PER_SLAB):
    hd = _lane_head(x.shape, heads)
    return jnp.concatenate([jnp.where(hd == h, x, 0.0) for h in range(heads)], axis=0)


def _seg_sum(x, seg):
    xb = x.astype(BF16)
    return jnp.concatenate(
        [_dot(xb[:, j * LANES:(j + 1) * LANES], seg) for j in range(x.shape[1] // LANES)], axis=1)


def _resident(shape):
    return pl.BlockSpec(shape, lambda *_: (0,) * len(shape), pipeline_mode=pl.Buffered(1))


def _ffn_kernel(*refs, has_mix, final_norm, nf, tf):
    refs = list(refs)
    x_ref = refs.pop(0)
    if has_mix:
        orw_ref, oat_ref, wor_ref, woa_ref = refs[:4]
        refs = refs[4:]
    g_ref, wg_ref, wu_ref, wd_ref = refs[:4]
    refs = refs[4:]
    if final_norm:
        gf_ref = refs.pop(0)
    out_ref, h_s, acc = refs

    x = x_ref[...]
    if has_mix:
        x = x + _dot(orw_ref[...], wor_ref[...]) + _dot(oat_ref[...], woa_ref[...])
        out_ref[...] = x
    h_s[...] = _rms(x, g_ref[...]).astype(BF16)

    def down(f):
        cols = pl.ds(f * tf if isinstance(f, int) else pl.multiple_of(f * tf, tf), tf)
        h = h_s[...]
        gate = _dot(h, wg_ref[:, cols])
        up = _dot(h, wu_ref[:, cols])
        act = (gate * jax.nn.sigmoid(gate) * up).astype(BF16)
        return _dot(act, wd_ref[cols, :])

    acc[...] = down(0)

    def body(f, carry):
        acc[...] += down(f)
        return carry

    lax.fori_loop(1, nf, body, 0)
    res = out_ref[...] if has_mix else x_ref[...]
    y = res + 0.5 * acc[...]
    if final_norm:
        y = _rms(y, gf_ref[...])
    out_ref[...] = y


def _ffn(x, norm, w_gate, w_up, w_down, mix=None, final_norm=None):
    n, d = x.shape
    f = w_gate.shape[1]
    tm, tf = FFN_ROWS, FFN_COLS
    nf = f // tf
    row = lambda i: (i, 0)
    wg, wu, wd = w_gate.astype(BF16), w_up.astype(BF16), w_down.astype(BF16)
    args = [x]
    specs = [pl.BlockSpec((tm, d), row)]
    if mix is not None:
        o_rwkv, o_attn, w_out = mix
        wo = w_out.astype(BF16)
        args += [o_rwkv, o_attn, wo[:RWKV_WIDTH], wo[RWKV_WIDTH:]]
        specs += [pl.BlockSpec((tm, RWKV_WIDTH), row), pl.BlockSpec((tm, ATTN_WIDTH), row),
                  _resident((RWKV_WIDTH, d)), _resident((ATTN_WIDTH, d))]
    args += [norm.reshape(1, d), wg, wu, wd]
    specs += [_resident((1, d)), _resident((d, f)), _resident((d, f)), _resident((f, d))]
    if final_norm is not None:
        args.append(final_norm.reshape(1, d))
        specs.append(_resident((1, d)))
    return pl.pallas_call(
        functools.partial(_ffn_kernel, has_mix=mix is not None,
                          final_norm=final_norm is not None, nf=nf, tf=tf),
        grid=(n // tm,),
        in_specs=specs,
        out_specs=pl.BlockSpec((tm, d), row),
        out_shape=jax.ShapeDtypeStruct((n, d), F32),
        scratch_shapes=[pltpu.VMEM((tm, d), BF16), pltpu.VMEM((tm, d), F32)],
        compiler_params=pltpu.CompilerParams(
            dimension_semantics=("parallel",), vmem_limit_bytes=VMEM_LIMIT),
        name="ffn_mix" if mix is not None else "ffn",
    )(*args)


def _inproj_kernel(x_ref, g_ref, w_ref, b_ref, mix_ref, prw_ref, pat_ref, last_row,
                   *, tiles_per_seq):
    h = _rms(x_ref[...], g_ref[...]).astype(BF16)
    p = _dot(h, w_ref[...]) + b_ref[...]
    pat_ref[...] = p[:, RWKV_COLS:]
    pr = p[:, :RWKV_COLS]
    prev_row = jnp.where(pl.program_id(0) % tiles_per_seq == 0, 0.0, last_row[0:1, :])
    row = lax.broadcasted_iota(jnp.int32, pr.shape, 0)
    p_prev = jnp.where(row == 0, prev_row, pltpu.roll(pr, 1, axis=0))
    prw_ref[...] = pr + (p_prev - pr) * mix_ref[...]
    last_row[0:1, :] = pr[pr.shape[0] - 1:, :]


def _in_proj(x, seq_len, norm, w_in, b_attn, shift_mix):
    n, d = x.shape
    cols = w_in.shape[1]
    tm = INPROJ_ROWS
    bias = jnp.concatenate([jnp.zeros((RWKV_COLS,), F32), b_attn.astype(F32)]).reshape(1, cols)
    row = lambda i: (i, 0)
    return pl.pallas_call(
        functools.partial(_inproj_kernel, tiles_per_seq=seq_len // tm),
        grid=(n // tm,),
        in_specs=[pl.BlockSpec((tm, d), row), _resident((1, d)), _resident((d, cols)),
                  _resident((1, cols)), _resident((1, RWKV_COLS))],
        out_specs=[pl.BlockSpec((tm, RWKV_COLS), row), pl.BlockSpec((tm, ATTN_COLS), row)],
        out_shape=[jax.ShapeDtypeStruct((n, RWKV_COLS), F32),
                   jax.ShapeDtypeStruct((n, ATTN_COLS), F32)],
        scratch_shapes=[pltpu.VMEM((8, RWKV_COLS), F32)],
        compiler_params=pltpu.CompilerParams(
            dimension_semantics=("arbitrary",), vmem_limit_bytes=VMEM_LIMIT),
        name="in_proj",
    )(x, norm.reshape(1, d), w_in.astype(BF16), bias, shift_mix.astype(F32).reshape(1, -1))


def _rwkv_prep_kernel(ps_ref, w0_ref, w2_ref, a0_ref, a2_ref, g2_ref, kk_ref, ka_ref, rk_ref,
                      seg_ref, lcum_ref,
                      m_ref, n_ref, r_ref, o_ref, bonus_ref, gate_ref,
                      at_s, bt_s, kt_s, rt_s, v_s, bh_s, kh_s, gam_s, *, n_chunks):
    C = CHUNK
    W = RWKV_WIDTH
    ps = ps_ref[...]
    r = ps[:, 0:W]
    k = ps[:, W:2 * W]
    v = ps[:, 2 * W:3 * W]
    wa = ps[:, 3 * W:3 * W + LANES]
    gl = ps[:, 3 * W + LANES:3 * W + 2 * LANES]
    z = w0_ref[...] + _dot(jnp.tanh(wa).astype(BF16), w2_ref[...])
    lw = -math.exp(-0.5) * jax.nn.sigmoid(z)
    a = jax.nn.sigmoid(a0_ref[...] + _dot(wa.astype(BF16), a2_ref[...]))
    gate_ref[...] = _dot(jax.nn.sigmoid(gl).astype(BF16), g2_ref[...])
    seg = seg_ref[...]
    kk = k * kk_ref[...]
    kk = kk * jnp.minimum(lax.rsqrt(_seg_sum(kk * kk, seg)), 1e12)
    k = k * (1.0 + (a - 1.0) * ka_ref[...])
    b = kk * a
    bonus_ref[...] = _seg_sum(r * k * rk_ref[...], seg) * v
    v_s[...] = v.astype(BF16)
    lcum = lcum_ref[...]
    lw_hi = lw.astype(BF16)
    lw_lo = (lw - lw_hi.astype(F32)).astype(BF16)
    cs = _dot(lcum, lw_hi) + _dot(lcum, lw_lo)
    for c in range(n_chunks):
        rows = slice(c * C, (c + 1) * C)
        cs_c = cs[rows]
        gam = jnp.exp(cs[(c + 1) * C - 1:(c + 1) * C, :])
        e_neg = jnp.exp(-cs_c)
        rt = r[rows] * jnp.exp(cs_c)
        bt = b[rows] * e_neg
        kt = k[rows] * e_neg
        at_s[rows, :] = (-kk[rows] * jnp.exp(cs_c - lw[rows])).astype(BF16)
        bt_s[rows, :] = bt.astype(BF16)
        kt_s[rows, :] = kt.astype(BF16)
        rt_s[rows, :] = rt.astype(BF16)
        r_ref[rows, :] = rt
        bh_s[rows, :] = bt * gam
        kh_s[rows, :] = kt * gam
        gam_s[c:c + 1, :] = gam

    rowi = lax.broadcasted_iota(jnp.int32, (2 * C, 2 * SLAB), 0)
    lane_s = lax.broadcasted_iota(jnp.int32, (2 * C, 2 * SLAB), 1) & (C - 1)
    keep_a1 = lane_s < (rowi & (C - 1)) + jnp.where(rowi < C, 0, 1)
    eye_pack = jnp.where((lax.broadcasted_iota(jnp.int32, (C, SLAB), 1) & (C - 1))
                         == lax.broadcasted_iota(jnp.int32, (C, SLAB), 0), 1.0, 0.0).astype(F32)
    same_head = ((lax.broadcasted_iota(jnp.int32, (SLAB, 2 * SLAB), 0) >> 6)
                 == _lane_head((SLAB, 2 * SLAB), HEADS_PER_SLAB))
    diag = (lax.broadcasted_iota(jnp.int32, (SLAB, SLAB), 0)
            == lax.broadcasted_iota(jnp.int32, (SLAB, SLAB), 1))
    zeros_sq = jnp.zeros((SLAB, SLAB), BF16)
    zeros_c = jnp.zeros((C, SLAB), BF16)

    def fold(x):
        out = x[:HEAD_DIM]
        for h in range(1, HEADS_PER_SLAB):
            out = out + x[h * HEAD_DIM:(h + 1) * HEAD_DIM]
        return out

    chains = [(c, s) for c in range(n_chunks) for s in range(N_SLABS)]
    ld = lambda ref: [ref[c * C:(c + 1) * C, s * SLAB:(s + 1) * SLAB] for c, s in chains]
    at, bt, kt, rt, vv = ld(at_s), ld(bt_s), ld(kt_s), ld(rt_s), ld(v_s)
    a1 = [jnp.where(keep_a1,
                    _dot_nt(jnp.concatenate([x, y], axis=0),
                            jnp.concatenate([_block_diag(p), _block_diag(q)], axis=0)), 0.0)
          for x, y, p, q in zip(at, rt, bt, kt)]
    aab = [x[:C, :SLAB] for x in a1]
    akv = [_dot(x[:C, SLAB:].astype(BF16), _block_diag(y)).astype(BF16) for x, y in zip(a1, vv)]
    tinv = [eye_pack + x for x in aab]
    pwb = [x.astype(BF16) for x in aab]
    pwb = [_dot(x, _block_diag(x)).astype(BF16) for x in pwb]
    for step in range(5):
        if step < 4:
            res = [_dot(jnp.concatenate([p, t.astype(BF16)], axis=0), _block_diag(p))
                   for t, p in zip(tinv, pwb)]
            pwb = [x[:C].astype(BF16) for x in res]
            tinv = [t + x[C:] for t, x in zip(tinv, res)]
        else:
            tinv = [t + _dot(t.astype(BF16), _block_diag(p)) for t, p in zip(tinv, pwb)]
    wu = [_dot(t.astype(BF16), _block_diag(jnp.concatenate([x, y], axis=1))).astype(BF16)
          for t, x, y in zip(tinv, at, akv)]
    for i, (c, s) in enumerate(chains):
        rows = slice(c * C, (c + 1) * C)
        sl = slice(s * SLAB, (s + 1) * SLAB)
        lhs_t = jnp.concatenate([bh_s[rows, sl], kh_s[rows, sl]], axis=0)
        rhs = jnp.concatenate([wu[i], jnp.concatenate([zeros_c, vv[i]], axis=1)], axis=0)
        mn = _dot(lhs_t.T.astype(BF16), rhs)
        mn = jnp.where(same_head, mn, 0.0)
        m_ref[rows, sl] = fold(mn[:, :SLAB] + jnp.where(diag, gam_s[c:c + 1, sl], 0.0)).astype(BF16)
        n_ref[rows, sl] = fold(mn[:, SLAB:])
        ro_rhs = jnp.concatenate(
            [_block_diag(wu[i]), jnp.concatenate([zeros_sq, _block_diag(vv[i])], axis=1)], axis=0)
        ro = _dot(a1[i][C:].astype(BF16), ro_rhs)
        r_ref[rows, sl] = r_ref[rows, sl] + ro[:, :SLAB]
        o_ref[rows, sl] = ro[:, SLAB:]


def _rwkv_prep(p_shift, w0, w2, a0, a2, g2, k_k, k_a, r_k):
    n = p_shift.shape[0]
    W = RWKV_WIDTH
    tt = RWKV_ROWS
    lane_head = jnp.arange(LANES) // HEAD_DIM
    seg = (lane_head[:, None] == lane_head[None, :]).astype(BF16)
    t = jnp.arange(tt)
    lcum = ((t[:, None] >= t[None, :]) & (t[:, None] // CHUNK == t[None, :] // CHUNK)).astype(BF16)
    w2p = jnp.concatenate([w2, jnp.zeros((AAA_LORA, W), w2.dtype)], axis=0).astype(BF16)
    a2p = jnp.concatenate([jnp.zeros((DECAY_LORA, W), a2.dtype), a2], axis=0).astype(BF16)
    row = lambda i: (i, 0)
    vec = lambda t: t.astype(F32).reshape(1, -1)
    f32_out = jax.ShapeDtypeStruct((n, W), F32)
    return pl.pallas_call(
        functools.partial(_rwkv_prep_kernel, n_chunks=tt // CHUNK),
        grid=(n // tt,),
        in_specs=[pl.BlockSpec((tt, RWKV_COLS), row),
                  _resident((1, W)), _resident((LANES, W)), _resident((1, W)),
                  _resident((LANES, W)), _resident((GATE_LORA, W)),
                  _resident((1, W)), _resident((1, W)), _resident((1, W)),
                  _resident((LANES, LANES)), _resident((tt, tt))],
        out_specs=[pl.BlockSpec((tt, W), row)] * 6,
        out_shape=[jax.ShapeDtypeStruct((n, W), BF16), f32_out, f32_out, f32_out, f32_out,
                   f32_out],
        scratch_shapes=[pltpu.VMEM((tt, W), BF16)] * 5 + [pltpu.VMEM((tt, W), F32)] * 2
                       + [pltpu.VMEM((8, W), F32)],
        compiler_params=pltpu.CompilerParams(
            dimension_semantics=("parallel",), vmem_limit_bytes=VMEM_LIMIT),
        name="rwkv_prep",
    )(p_shift, vec(w0), w2p, vec(a0), a2p, g2.astype(BF16), vec(k_k), vec(k_a), vec(r_k),
      seg, lcum)


def _rwkv_scan_kernel(m_ref, n_ref, r_ref, o_ref, bonus_ref, gate_ref, lnw_ref, lnb_ref, seg_ref,
                      out_ref, state, obuf, *, n_batch, n_chunks):
    @pl.when(pl.program_id(0) == 0)
    def _():
        state[...] = jnp.zeros_like(state)

    C = CHUNK
    for c in range(n_chunks):
        rows = slice(c * C, (c + 1) * C)
        for b in range(n_batch):
            for s in range(N_SLABS):
                sl = slice(s * SLAB, (s + 1) * SLAB)
                s0 = state[b * N_SLABS + s]
                lhs = jnp.concatenate([m_ref[b, rows, sl], r_ref[b, rows, sl].astype(BF16)], axis=0)
                res = _dot(lhs, _block_diag(s0.astype(BF16)))
                state[b * N_SLABS + s] = res[:C] + n_ref[b, rows, sl]
                obuf[b, rows, sl] = res[C:] + o_ref[b, rows, sl]

    seg_mean = seg_ref[...]
    for b in range(n_batch):
        o = obuf[b]
        d = o - _seg_sum(o, seg_mean)
        var = _seg_sum(d * d, seg_mean)
        y = d * lax.rsqrt(var + GN_EPS) * lnw_ref[...] + lnb_ref[...]
        out_ref[b] = ((y + bonus_ref[b]) * gate_ref[b]).astype(BF16)


def _rwkv_scan(mats, n_batch, seq_len, ln_w, ln_b):
    W = RWKV_WIDTH
    tt = RWKV_ROWS
    lane_head = jnp.arange(LANES) // HEAD_DIM
    seg_mean = ((lane_head[:, None] == lane_head[None, :]).astype(F32) / HEAD_DIM).astype(BF16)
    mats = [t.reshape(n_batch, seq_len, W) for t in mats]
    blk = pl.BlockSpec((n_batch, tt, W), lambda i: (0, i, 0))
    out = pl.pallas_call(
        functools.partial(_rwkv_scan_kernel, n_batch=n_batch, n_chunks=tt // CHUNK),
        grid=(seq_len // tt,),
        in_specs=[blk] * 6 + [_resident((1, W)), _resident((1, W)), _resident((LANES, LANES))],
        out_specs=blk,
        out_shape=jax.ShapeDtypeStruct((n_batch, seq_len, W), BF16),
        scratch_shapes=[pltpu.VMEM((n_batch * N_SLABS, CHUNK, SLAB), F32),
                        pltpu.VMEM((n_batch, tt, W), F32)],
        compiler_params=pltpu.CompilerParams(
            dimension_semantics=("arbitrary",), vmem_limit_bytes=VMEM_LIMIT),
        name="rwkv_scan",
    )(*mats, ln_w.astype(F32).reshape(1, W), ln_b.astype(F32).reshape(1, W), seg_mean)
    return out.reshape(n_batch * seq_len, W)


def _attn_kernel(sink_ref, q_ref, kp_ref, kc_ref, vp_ref, vc_ref, out_ref, *, n_blocks):
    BQ = WINDOW
    first = pl.program_id(1) == 0
    kx = jnp.concatenate([kp_ref[...], kc_ref[...]], axis=0)
    vx = jnp.concatenate([vp_ref[...], vc_ref[...]], axis=0)
    lo = _lane_head(kx.shape, 2) == 0
    kxr = pltpu.roll(kx, HEAD_DIM, axis=1)
    vxr = pltpu.roll(vx, HEAD_DIM, axis=1)
    bf = lambda t: t.astype(BF16)
    k_lo = [bf(jnp.where(lo, kx, 0.0)), bf(jnp.where(lo, kxr, 0.0))]
    k_hi = [bf(jnp.where(lo, 0.0, kxr)), bf(jnp.where(lo, 0.0, kx))]
    v_lo = [bf(jnp.where(lo, vx, 0.0)), bf(jnp.where(lo, vxr, 0.0))]
    v_hi = [bf(jnp.where(lo, 0.0, vxr)), bf(jnp.where(lo, 0.0, vx))]

    qi = lax.broadcasted_iota(jnp.int32, (BQ, 2 * BQ), 0)
    kj = lax.broadcasted_iota(jnp.int32, (BQ, 2 * BQ), 1)
    dist = qi + BQ - kj
    band = (dist >= 0) & (dist < WINDOW)
    band_first = band & (kj >= jnp.where(first, BQ, 0))
    lo_out = _lane_head((BQ, LANES), 2) == 0
    scale = HEAD_DIM ** -0.5

    for qb in range(n_blocks):
        valid = band_first if qb == 0 else band
        krows = slice(qb * BQ, (qb + 2) * BQ)
        for j in range(N_Q_HEADS // 2):
            g = (2 * j) // (N_Q_HEADS // N_KV_HEADS)
            q = (q_ref[qb * BQ:(qb + 1) * BQ, j * LANES:(j + 1) * LANES] * scale).astype(BF16)
            s = _dot_nt(q, jnp.concatenate([k_lo[g][krows], k_hi[g][krows]], axis=0))
            es, inv = [], []
            for hh in range(2):
                sink = sink_ref[2 * j + hh]
                sh = jnp.where(valid, s[:, hh * 2 * BQ:(hh + 1) * 2 * BQ], -jnp.inf)
                m = jnp.maximum(jnp.max(sh, axis=-1, keepdims=True), sink)
                e = jnp.exp(sh - m)
                inv.append(1.0 / (jnp.sum(e, axis=-1, keepdims=True) + jnp.exp(sink - m)))
                es.append(e.astype(BF16))
            pv = _dot(jnp.concatenate(es, axis=1),
                      jnp.concatenate([v_lo[g][krows], v_hi[g][krows]], axis=0))
            out_ref[qb * BQ:(qb + 1) * BQ, j * LANES:(j + 1) * LANES] = (
                pv * jnp.where(lo_out, inv[0], inv[1])).astype(BF16)


def _attn(p_attn, n_batch, seq_len, sinks):
    n = p_attn.shape[0]
    rows = ATTN_ROWS
    n_blocks = rows // WINDOW
    steps = seq_len // rows
    kcol = ATTN_WIDTH // LANES
    vcol = kcol + KV_WIDTH // LANES
    cur = lambda b, i: b * steps + i
    prev = lambda b, i: (b * steps + i) * n_blocks - jnp.minimum(i, 1)
    return pl.pallas_call(
        functools.partial(_attn_kernel, n_blocks=n_blocks),
        grid=(n_batch, steps),
        in_specs=[pl.BlockSpec(memory_space=pltpu.SMEM),
                  pl.BlockSpec((rows, ATTN_WIDTH), lambda b, i: (cur(b, i), 0)),
                  pl.BlockSpec((WINDOW, KV_WIDTH), lambda b, i: (prev(b, i), kcol)),
                  pl.BlockSpec((rows, KV_WIDTH), lambda b, i: (cur(b, i), kcol)),
                  pl.BlockSpec((WINDOW, KV_WIDTH), lambda b, i: (prev(b, i), vcol)),
                  pl.BlockSpec((rows, KV_WIDTH), lambda b, i: (cur(b, i), vcol))],
        out_specs=pl.BlockSpec((rows, ATTN_WIDTH), lambda b, i: (cur(b, i), 0)),
        out_shape=jax.ShapeDtypeStruct((n, ATTN_WIDTH), BF16),
        compiler_params=pltpu.CompilerParams(
            dimension_semantics=("parallel", "arbitrary"), vmem_limit_bytes=VMEM_LIMIT),
        name="swa_attn",
    )(sinks.astype(F32), p_attn, p_attn, p_attn, p_attn, p_attn)


def kernel(x, norm_ffn1, ffn1_gate, ffn1_up, ffn1_down, norm_mix, w_in, b_in_attn, rwkv_shift_mix, rwkv_w0, rwkv_w2, rwkv_a0, rwkv_a2, rwkv_g2, rwkv_k_k, rwkv_k_a, rwkv_r_k, rwkv_ln_w, rwkv_ln_b, attn_sinks, w_out, norm_ffn2, ffn2_gate, ffn2_up, ffn2_down, norm_final):
    n_batch, seq_len, d = x.shape
    depth = w_in.shape[0]
    h = x.reshape(n_batch * seq_len, d)
    for l in range(depth):
        h = _ffn(h, norm_ffn1[l], ffn1_gate[l], ffn1_up[l], ffn1_down[l])
        p_shift, p_attn = _in_proj(h, seq_len, norm_mix[l], w_in[l], b_in_attn[l],
                                   rwkv_shift_mix[l])
        mats = _rwkv_prep(p_shift, rwkv_w0[l], rwkv_w2[l], rwkv_a0[l], rwkv_a2[l], rwkv_g2[l],
                          rwkv_k_k[l], rwkv_k_a[l], rwkv_r_k[l])
        o_rwkv = _rwkv_scan(mats, n_batch, seq_len, rwkv_ln_w[l], rwkv_ln_b[l])
        o_attn = _attn(p_attn, n_batch, seq_len, attn_sinks[l])
        h = _ffn(h, norm_ffn2[l], ffn2_gate[l], ffn2_up[l], ffn2_down[l],
                 mix=(o_rwkv, o_attn, w_out[l]),
                 final_norm=norm_final if l == depth - 1 else None)
    return h.reshape(n_batch, seq_len, d)
```

```python
import functools
import math

import jax
import jax.numpy as jnp
from jax import lax
from jax.experimental import pallas as pl
from jax.experimental.pallas import tpu as pltpu

F32 = jnp.float32
BF16 = jnp.bfloat16

HEAD_DIM = 64
LANES = 128
SLAB = 128
HEADS_PER_SLAB = SLAB // HEAD_DIM
CHUNK = 64
N_RWKV_HEADS = 8
RWKV_WIDTH = N_RWKV_HEADS * HEAD_DIM
N_SLABS = RWKV_WIDTH // SLAB
DECAY_LORA = 64
AAA_LORA = 64
GATE_LORA = 128
RWKV_COLS = 3 * RWKV_WIDTH + DECAY_LORA + AAA_LORA + GATE_LORA
N_Q_HEADS = 8
N_KV_HEADS = 2
ATTN_WIDTH = N_Q_HEADS * HEAD_DIM
KV_WIDTH = N_KV_HEADS * HEAD_DIM
ATTN_COLS = ATTN_WIDTH + 2 * KV_WIDTH
WINDOW = 128
GN_EPS = 64e-5
NORM_EPS = 1e-5
VMEM_LIMIT = 56 * 1024 * 1024

FFN_ROWS = 1024
FFN_COLS = 256
INPROJ_ROWS = 512
RWKV_ROWS = 256
ATTN_ROWS = 512


def _dot(a, b):
    return jnp.dot(a, b, preferred_element_type=F32)


def _dot_nt(a, b):
    return lax.dot_general(a, b, (((1,), (1,)), ((), ())), preferred_element_type=F32)


def _rms(x, g):
    ms = jnp.mean(x * x, axis=-1, keepdims=True)
    return x * lax.rsqrt(ms + NORM_EPS) * g


def _lane_head(shape, heads):
    lane = lax.broadcasted_iota(jnp.int32, shape, 1)
    return (lane >> 6) & (heads - 1)


def _block_diag(x, heads=HEADS_PER_SLAB):
    hd = _lane_head(x.shape, heads)
    return jnp.concatenate([jnp.where(hd == h, x, 0.0) for h in range(heads)], axis=0)


def _seg_sum(x, seg):
    xb = x.astype(BF16)
    return jnp.concatenate(
        [_dot(xb[:, j * LANES:(j + 1) * LANES], seg) for j in range(x.shape[1] // LANES)], axis=1)


def _resident(shape):
    return pl.BlockSpec(shape, lambda *_: (0,) * len(shape), pipeline_mode=pl.Buffered(1))


def _ffn_kernel(*refs, has_mix, final_norm, nf, tf):
    refs = list(refs)
    x_ref = refs.pop(0)
    if has_mix:
        orw_ref, oat_ref, wor_ref, woa_ref = refs[:4]
        refs = refs[4:]
    g_ref, wg_ref, wu_ref, wd_ref = refs[:4]
    refs = refs[4:]
    if final_norm:
        gf_ref = refs.pop(0)
    out_ref, h_s, acc = refs

    x = x_ref[...]
    if has_mix:
        x = x + _dot(orw_ref[...], wor_ref[...]) + _dot(oat_ref[...], woa_ref[...])
        out_ref[...] = x
    h_s[...] = _rms(x, g_ref[...]).astype(BF16)

    def down(f):
        cols = pl.ds(f * tf if isinstance(f, int) else pl.multiple_of(f * tf, tf), tf)
        h = h_s[...]
        gate = _dot(h, wg_ref[:, cols])
        up = _dot(h, wu_ref[:, cols])
        act = (gate * jax.nn.sigmoid(gate) * up).astype(BF16)
        return _dot(act, wd_ref[cols, :])

    acc[...] = down(0)

    def body(f, carry):
        acc[...] += down(f)
        return carry

    lax.fori_loop(1, nf, body, 0)
    res = out_ref[...] if has_mix else x_ref[...]
    y = res + 0.5 * acc[...]
    if final_norm:
        y = _rms(y, gf_ref[...])
    out_ref[...] = y


def _ffn(x, norm, w_gate, w_up, w_down, mix=None, final_norm=None):
    n, d = x.shape
    f = w_gate.shape[1]
    tm, tf = FFN_ROWS, FFN_COLS
    nf = f // tf
    row = lambda i: (i, 0)
    wg, wu, wd = w_gate.astype(BF16), w_up.astype(BF16), w_down.astype(BF16)
    args = [x]
    specs = [pl.BlockSpec((tm, d), row)]
    if mix is not None:
        o_rwkv, o_attn, w_out = mix
        wo = w_out.astype(BF16)
        args += [o_rwkv, o_attn, wo[:RWKV_WIDTH], wo[RWKV_WIDTH:]]
        specs += [pl.BlockSpec((tm, RWKV_WIDTH), row), pl.BlockSpec((tm, ATTN_WIDTH), row),
                  _resident((RWKV_WIDTH, d)), _resident((ATTN_WIDTH, d))]
    args += [norm.reshape(1, d), wg, wu, wd]
    specs += [_resident((1, d)), _resident((d, f)), _resident((d, f)), _resident((f, d))]
    if final_norm is not None:
        args.append(final_norm.reshape(1, d))
        specs.append(_resident((1, d)))
    return pl.pallas_call(
        functools.partial(_ffn_kernel, has_mix=mix is not None,
                          final_norm=final_norm is not None, nf=nf, tf=tf),
        grid=(n // tm,),
        in_specs=specs,
        out_specs=pl.BlockSpec((tm, d), row),
        out_shape=jax.ShapeDtypeStruct((n, d), F32),
        scratch_shapes=[pltpu.VMEM((tm, d), BF16), pltpu.VMEM((tm, d), F32)],
        compiler_params=pltpu.CompilerParams(
            dimension_semantics=("parallel",), vmem_limit_bytes=VMEM_LIMIT),
        name="ffn_mix" if mix is not None else "ffn",
    )(*args)


def _inproj_kernel(x_ref, g_ref, w_ref, b_ref, mix_ref, prw_ref, pat_ref, last_row,
                   *, tiles_per_seq):
    h = _rms(x_ref[...], g_ref[...]).astype(BF16)
    p = _dot(h, w_ref[...]) + b_ref[...]
    pat_ref[...] = p[:, RWKV_COLS:]
    pr = p[:, :RWKV_COLS]
    prev_row = jnp.where(pl.program_id(0) % tiles_per_seq == 0, 0.0, last_row[0:1, :])
    row = lax.broadcasted_iota(jnp.int32, pr.shape, 0)
    p_prev = jnp.where(row == 0, prev_row, pltpu.roll(pr, 1, axis=0))
    prw_ref[...] = pr + (p_prev - pr) * mix_ref[...]
    last_row[0:1, :] = pr[pr.shape[0] - 1:, :]


def _in_proj(x, seq_len, norm, w_in, b_attn, shift_mix):
    n, d = x.shape
    cols = w_in.shape[1]
    tm = INPROJ_ROWS
    bias = jnp.concatenate([jnp.zeros((RWKV_COLS,), F32), b_attn.astype(F32)]).reshape(1, cols)
    row = lambda i: (i, 0)
    return pl.pallas_call(
        functools.partial(_inproj_kernel, tiles_per_seq=seq_len // tm),
        grid=(n // tm,),
        in_specs=[pl.BlockSpec((tm, d), row), _resident((1, d)), _resident((d, cols)),
                  _resident((1, cols)), _resident((1, RWKV_COLS))],
        out_specs=[pl.BlockSpec((tm, RWKV_COLS), row), pl.BlockSpec((tm, ATTN_COLS), row)],
        out_shape=[jax.ShapeDtypeStruct((n, RWKV_COLS), F32),
                   jax.ShapeDtypeStruct((n, ATTN_COLS), F32)],
        scratch_shapes=[pltpu.VMEM((8, RWKV_COLS), F32)],
        compiler_params=pltpu.CompilerParams(
            dimension_semantics=("arbitrary",), vmem_limit_bytes=VMEM_LIMIT),
        name="in_proj",
    )(x, norm.reshape(1, d), w_in.astype(BF16), bias, shift_mix.astype(F32).reshape(1, -1))


def _rwkv_prep_kernel(ps_ref, w0_ref, w2_ref, a0_ref, a2_ref, g2_ref, kk_ref, ka_ref, rk_ref,
                      seg_ref, lcum_ref,
                      m_ref, n_ref, r_ref, o_ref, bonus_ref, gate_ref,
                      at_s, bt_s, kt_s, rt_s, v_s, bh_s, kh_s, rf_s, gam_s, *, n_chunks):
    C = CHUNK
    W = RWKV_WIDTH
    ps = ps_ref[...]
    r = ps[:, 0:W]
    k = ps[:, W:2 * W]
    v = ps[:, 2 * W:3 * W]
    wa = ps[:, 3 * W:3 * W + LANES]
    gl = ps[:, 3 * W + LANES:3 * W + 2 * LANES]
    z = w0_ref[...] + _dot(jnp.tanh(wa).astype(BF16), w2_ref[...])
    lw = -math.exp(-0.5) * jax.nn.sigmoid(z)
    a = jax.nn.sigmoid(a0_ref[...] + _dot(wa.astype(BF16), a2_ref[...]))
    gate_ref[...] = _dot(jax.nn.sigmoid(gl).astype(BF16), g2_ref[...]).astype(BF16)
    seg = seg_ref[...]
    kk = k * kk_ref[...]
    kk = kk * jnp.minimum(lax.rsqrt(_seg_sum(kk * kk, seg)), 1e12)
    k = k * (1.0 + (a - 1.0) * ka_ref[...])
    b = kk * a
    bonus_ref[...] = (_seg_sum(r * k * rk_ref[...], seg) * v).astype(BF16)
    v_s[...] = v.astype(BF16)
    lcum = lcum_ref[...]
    lw_hi = lw.astype(BF16)
    lw_lo = (lw - lw_hi.astype(F32)).astype(BF16)
    cs = _dot(lcum, lw_hi) + _dot(lcum, lw_lo)
    for c in range(n_chunks):
        rows = slice(c * C, (c + 1) * C)
        cs_c = cs[rows]
        gam = jnp.exp(cs[(c + 1) * C - 1:(c + 1) * C, :])
        e_neg = jnp.exp(-cs_c)
        rt = r[rows] * jnp.exp(cs_c)
        bt = b[rows] * e_neg
        kt = k[rows] * e_neg
        at_s[rows, :] = (-kk[rows] * jnp.exp(cs_c - lw[rows])).astype(BF16)
        bt_s[rows, :] = bt.astype(BF16)
        kt_s[rows, :] = kt.astype(BF16)
        rt_s[rows, :] = rt.astype(BF16)
        rf_s[rows, :] = rt
        bh_s[rows, :] = bt * gam
        kh_s[rows, :] = kt * gam
        gam_s[c:c + 1, :] = gam

    rowi = lax.broadcasted_iota(jnp.int32, (2 * C, 2 * SLAB), 0)
    lane_s = lax.broadcasted_iota(jnp.int32, (2 * C, 2 * SLAB), 1) & (C - 1)
    keep_a1 = lane_s < (rowi & (C - 1)) + jnp.where(rowi < C, 0, 1)
    eye_pack = jnp.where((lax.broadcasted_iota(jnp.int32, (C, SLAB), 1) & (C - 1))
                         == lax.broadcasted_iota(jnp.int32, (C, SLAB), 0), 1.0, 0.0).astype(F32)
    same_head = ((lax.broadcasted_iota(jnp.int32, (SLAB, 2 * SLAB), 0) >> 6)
                 == _lane_head((SLAB, 2 * SLAB), HEADS_PER_SLAB))
    diag = (lax.broadcasted_iota(jnp.int32, (SLAB, SLAB), 0)
            == lax.broadcasted_iota(jnp.int32, (SLAB, SLAB), 1))
    zeros_sq = jnp.zeros((SLAB, SLAB), BF16)
    zeros_c = jnp.zeros((C, SLAB), BF16)

    def fold(x):
        out = x[:HEAD_DIM]
        for h in range(1, HEADS_PER_SLAB):
            out = out + x[h * HEAD_DIM:(h + 1) * HEAD_DIM]
        return out

    chains = [(c, s) for c in range(n_chunks) for s in range(N_SLABS)]
    ld = lambda ref: [ref[c * C:(c + 1) * C, s * SLAB:(s + 1) * SLAB] for c, s in chains]
    at, bt, kt, rt, vv = ld(at_s), ld(bt_s), ld(kt_s), ld(rt_s), ld(v_s)
    a1 = [jnp.where(keep_a1,
                    _dot_nt(jnp.concatenate([x, y], axis=0),
                            jnp.concatenate([_block_diag(p), _block_diag(q)], axis=0)), 0.0)
          for x, y, p, q in zip(at, rt, bt, kt)]
    aab = [x[:C, :SLAB] for x in a1]
    akv = [_dot(x[:C, SLAB:].astype(BF16), _block_diag(y)).astype(BF16) for x, y in zip(a1, vv)]
    tinv = [eye_pack + x for x in aab]
    pwb = [x.astype(BF16) for x in aab]
    pwb = [_dot(x, _block_diag(x)).astype(BF16) for x in pwb]
    for step in range(5):
        if step < 4:
            res = [_dot(jnp.concatenate([p, t.astype(BF16)], axis=0), _block_diag(p))
                   for t, p in zip(tinv, pwb)]
            pwb = [x[:C].astype(BF16) for x in res]
            tinv = [t + x[C:] for t, x in zip(tinv, res)]
        else:
            tinv = [t + _dot(t.astype(BF16), _block_diag(p)) for t, p in zip(tinv, pwb)]
    wu = [_dot(t.astype(BF16), _block_diag(jnp.concatenate([x, y], axis=1))).astype(BF16)
          for t, x, y in zip(tinv, at, akv)]
    for i, (c, s) in enumerate(chains):
        rows = slice(c * C, (c + 1) * C)
        sl = slice(s * SLAB, (s + 1) * SLAB)
        lhs_t = jnp.concatenate([bh_s[rows, sl], kh_s[rows, sl]], axis=0)
        rhs = jnp.concatenate([wu[i], jnp.concatenate([zeros_c, vv[i]], axis=1)], axis=0)
        mn = _dot(lhs_t.T.astype(BF16), rhs)
        mn = jnp.where(same_head, mn, 0.0)
        m_ref[rows, sl] = fold(mn[:, :SLAB] + jnp.where(diag, gam_s[c:c + 1, sl], 0.0)).astype(BF16)
        n_ref[rows, sl] = fold(mn[:, SLAB:])
        ro_rhs = jnp.concatenate(
            [_block_diag(wu[i]), jnp.concatenate([zeros_sq, _block_diag(vv[i])], axis=1)], axis=0)
        ro = _dot(a1[i][C:].astype(BF16), ro_rhs)
        r_ref[rows, sl] = (rf_s[rows, sl] + ro[:, :SLAB]).astype(BF16)
        o_ref[rows, sl] = ro[:, SLAB:].astype(BF16)


def _rwkv_prep(p_shift, w0, w2, a0, a2, g2, k_k, k_a, r_k):
    n = p_shift.shape[0]
    W = RWKV_WIDTH
    tt = RWKV_ROWS
    lane_head = jnp.arange(LANES) // HEAD_DIM
    seg = (lane_head[:, None] == lane_head[None, :]).astype(BF16)
    t = jnp.arange(tt)
    lcum = ((t[:, None] >= t[None, :]) & (t[:, None] // CHUNK == t[None, :] // CHUNK)).astype(BF16)
    w2p = jnp.concatenate([w2, jnp.zeros((AAA_LORA, W), w2.dtype)], axis=0).astype(BF16)
    a2p = jnp.concatenate([jnp.zeros((DECAY_LORA, W), a2.dtype), a2], axis=0).astype(BF16)
    row = lambda i: (i, 0)
    vec = lambda t: t.astype(F32).reshape(1, -1)
    f32_out = jax.ShapeDtypeStruct((n, W), F32)
    bf16_out = jax.ShapeDtypeStruct((n, W), BF16)
    return pl.pallas_call(
        functools.partial(_rwkv_prep_kernel, n_chunks=tt // CHUNK),
        grid=(n // tt,),
        in_specs=[pl.BlockSpec((tt, RWKV_COLS), row),
                  _resident((1, W)), _resident((LANES, W)), _resident((1, W)),
                  _resident((LANES, W)), _resident((GATE_LORA, W)),
                  _resident((1, W)), _resident((1, W)), _resident((1, W)),
                  _resident((LANES, LANES)), _resident((tt, tt))],
        out_specs=[pl.BlockSpec((tt, W), row)] * 6,
        out_shape=[bf16_out, f32_out, bf16_out, bf16_out, bf16_out, bf16_out],
        scratch_shapes=[pltpu.VMEM((tt, W), BF16)] * 5 + [pltpu.VMEM((tt, W), F32)] * 3
                       + [pltpu.VMEM((8, W), F32)],
        compiler_params=pltpu.CompilerParams(
            dimension_semantics=("parallel",), vmem_limit_bytes=VMEM_LIMIT),
        name="rwkv_prep",
    )(p_shift, vec(w0), w2p, vec(a0), a2p, g2.astype(BF16), vec(k_k), vec(k_a), vec(r_k),
      seg, lcum)


def _rwkv_scan_kernel(m_ref, n_ref, r_ref, o_ref, bonus_ref, gate_ref, lnw_ref, lnb_ref, seg_ref,
                      out_ref, state, obuf, *, n_batch, n_chunks):
    @pl.when(pl.program_id(0) == 0)
    def _():
        state[...] = jnp.zeros_like(state)

    C = CHUNK
    for c in range(n_chunks):
        rows = slice(c * C, (c + 1) * C)
        for b in range(n_batch):
            for s in range(N_SLABS):
                sl = slice(s * SLAB, (s + 1) * SLAB)
                s0 = state[b * N_SLABS + s]
                lhs = jnp.concatenate([m_ref[b, rows, sl], r_ref[b, rows, sl]], axis=0)
                res = _dot(lhs, _block_diag(s0.astype(BF16)))
                state[b * N_SLABS + s] = res[:C] + n_ref[b, rows, sl]
                obuf[b, rows, sl] = res[C:] + o_ref[b, rows, sl]

    seg_mean = seg_ref[...]
    for b in range(n_batch):
        o = obuf[b]
        d = o - _seg_sum(o, seg_mean)
        var = _seg_sum(d * d, seg_mean)
        y = d * lax.rsqrt(var + GN_EPS) * lnw_ref[...] + lnb_ref[...]
        out_ref[b] = ((y + bonus_ref[b]) * gate_ref[b]).astype(BF16)


def _rwkv_scan(mats, n_batch, seq_len, ln_w, ln_b):
    W = RWKV_WIDTH
    tt = RWKV_ROWS
    lane_head = jnp.arange(LANES) // HEAD_DIM
    seg_mean = ((lane_head[:, None] == lane_head[None, :]).astype(F32) / HEAD_DIM).astype(BF16)
    mats = [t.reshape(n_batch, seq_len, W) for t in mats]
    blk = pl.BlockSpec((n_batch, tt, W), lambda i: (0, i, 0))
    out = pl.pallas_call(
        functools.partial(_rwkv_scan_kernel, n_batch=n_batch, n_chunks=tt // CHUNK),
        grid=(seq_len // tt,),
        in_specs=[blk] * 6 + [_resident((1, W)), _resident((1, W)), _resident((LANES, LANES))],
        out_specs=blk,
        out_shape=jax.ShapeDtypeStruct((n_batch, seq_len, W), BF16),
        scratch_shapes=[pltpu.VMEM((n_batch * N_SLABS, CHUNK, SLAB), F32),
                        pltpu.VMEM((n_batch, tt, W), F32)],
        compiler_params=pltpu.CompilerParams(
            dimension_semantics=("arbitrary",), vmem_limit_bytes=VMEM_LIMIT),
        name="rwkv_scan",
    )(*mats, ln_w.astype(F32).reshape(1, W), ln_b.astype(F32).reshape(1, W), seg_mean)
    return out.reshape(n_batch * seq_len, W)


def _attn_kernel(sink_ref, q_ref, kp_ref, kc_ref, vp_ref, vc_ref, out_ref, *, n_blocks):
    BQ = WINDOW
    first = pl.program_id(1) == 0
    kx = jnp.concatenate([kp_ref[...], kc_ref[...]], axis=0)
    vx = jnp.concatenate([vp_ref[...], vc_ref[...]], axis=0)
    lo = _lane_head(kx.shape, 2) == 0
    kxr = pltpu.roll(kx, HEAD_DIM, axis=1)
    vxr = pltpu.roll(vx, HEAD_DIM, axis=1)
    bf = lambda t: t.astype(BF16)
    k_lo = [bf(jnp.where(lo, kx, 0.0)), bf(jnp.where(lo, kxr, 0.0))]
    k_hi = [bf(jnp.where(lo, 0.0, kxr)), bf(jnp.where(lo, 0.0, kx))]
    v_lo = [bf(jnp.where(lo, vx, 0.0)), bf(jnp.where(lo, vxr, 0.0))]
    v_hi = [bf(jnp.where(lo, 0.0, vxr)), bf(jnp.where(lo, 0.0, vx))]

    qi = lax.broadcasted_iota(jnp.int32, (BQ, 2 * BQ), 0)
    kj = lax.broadcasted_iota(jnp.int32, (BQ, 2 * BQ), 1)
    dist = qi + BQ - kj
    band = (dist >= 0) & (dist < WINDOW)
    band_first = band & (kj >= jnp.where(first, BQ, 0))
    col0 = lax.broadcasted_iota(jnp.int32, (1, 2 * BQ), 1) == 0
    log2e = math.log2(math.e)
    fill = [jnp.where(col0, sink_ref[h] * log2e, -jnp.inf) for h in range(N_Q_HEADS)]
    vrow = lax.broadcasted_iota(jnp.int32, (4 * BQ, LANES), 0) & (2 * BQ - 1)
    ones_sel = jnp.where((lax.broadcasted_iota(jnp.int32, (4 * BQ, LANES), 0) < 2 * BQ)
                         == (_lane_head((4 * BQ, LANES), 2) == 0), 1.0, 0.0).astype(BF16)
    qscale = HEAD_DIM ** -0.5 * log2e
    n_pairs = N_Q_HEADS // 2

    for qb in range(n_blocks):
        valid = band_first if qb == 0 else band
        krows = slice(qb * BQ, (qb + 2) * BQ)
        kcat = [jnp.concatenate([k_lo[g][krows], k_hi[g][krows]], axis=0) for g in range(N_KV_HEADS)]
        vcat = [jnp.concatenate(
                    [jnp.where(vrow == 0, 0.0,
                               jnp.concatenate([v_lo[g][krows], v_hi[g][krows]], axis=0)),
                     ones_sel], axis=1) for g in range(N_KV_HEADS)]
        kv_of = [(2 * j) // (N_Q_HEADS // N_KV_HEADS) for j in range(n_pairs)]
        s = [_dot_nt((q_ref[qb * BQ:(qb + 1) * BQ, j * LANES:(j + 1) * LANES] * qscale).astype(BF16),
                     kcat[kv_of[j]]) for j in range(n_pairs)]
        sh = [[jnp.where(valid, s[j][:, hh * 2 * BQ:(hh + 1) * 2 * BQ], fill[2 * j + hh])
               for hh in range(2)] for j in range(n_pairs)]
        mx = [[jnp.max(x, axis=-1, keepdims=True) for x in row] for row in sh]
        e = [jnp.concatenate([jnp.exp2(x - m).astype(BF16) for x, m in zip(xr, mr)], axis=1)
             for xr, mr in zip(sh, mx)]
        pv = [_dot(e[j], vcat[kv_of[j]]) for j in range(n_pairs)]
        for j in range(n_pairs):
            out_ref[qb * BQ:(qb + 1) * BQ, j * LANES:(j + 1) * LANES] = (
                pv[j][:, :LANES] / pv[j][:, LANES:]).astype(BF16)


def _attn(p_attn, n_batch, seq_len, sinks):
    n = p_attn.shape[0]
    rows = ATTN_ROWS
    n_blocks = rows // WINDOW
    steps = seq_len // rows
    kcol = ATTN_WIDTH // LANES
    vcol = kcol + KV_WIDTH // LANES
    cur = lambda b, i: b * steps + i
    prev = lambda b, i: (b * steps + i) * n_blocks - jnp.minimum(i, 1)
    return pl.pallas_call(
        functools.partial(_attn_kernel, n_blocks=n_blocks),
        grid=(n_batch, steps),
        in_specs=[pl.BlockSpec(memory_space=pltpu.SMEM),
                  pl.BlockSpec((rows, ATTN_WIDTH), lambda b, i: (cur(b, i), 0)),
                  pl.BlockSpec((WINDOW, KV_WIDTH), lambda b, i: (prev(b, i), kcol)),
                  pl.BlockSpec((rows, KV_WIDTH), lambda b, i: (cur(b, i), kcol)),
                  pl.BlockSpec((WINDOW, KV_WIDTH), lambda b, i: (prev(b, i), vcol)),
                  pl.BlockSpec((rows, KV_WIDTH), lambda b, i: (cur(b, i), vcol))],
        out_specs=pl.BlockSpec((rows, ATTN_WIDTH), lambda b, i: (cur(b, i), 0)),
        out_shape=jax.ShapeDtypeStruct((n, ATTN_WIDTH), BF16),
        compiler_params=pltpu.CompilerParams(
            dimension_semantics=("parallel", "arbitrary"), vmem_limit_bytes=VMEM_LIMIT),
        name="swa_attn",
    )(sinks.astype(F32), p_attn, p_attn, p_attn, p_attn, p_attn)


def kernel(x, norm_ffn1, ffn1_gate, ffn1_up, ffn1_down, norm_mix, w_in, b_in_attn, rwkv_shift_mix, rwkv_w0, rwkv_w2, rwkv_a0, rwkv_a2, rwkv_g2, rwkv_k_k, rwkv_k_a, rwkv_r_k, rwkv_ln_w, rwkv_ln_b, attn_sinks, w_out, norm_ffn2, ffn2_gate, ffn2_up, ffn2_down, norm_final):
    n_batch, seq_len, d = x.shape
    depth = w_in.shape[0]
    h = x.reshape(n_batch * seq_len, d)
    for l in range(depth):
        h = _ffn(h, norm_ffn1[l], ffn1_gate[l], ffn1_up[l], ffn1_down[l])
        p_shift, p_attn = _in_proj(h, seq_len, norm_mix[l], w_in[l], b_in_attn[l],
                                   rwkv_shift_mix[l])
        mats = _rwkv_prep(p_shift, rwkv_w0[l], rwkv_w2[l], rwkv_a0[l], rwkv_a2[l], rwkv_g2[l],
                          rwkv_k_k[l], rwkv_k_a[l], rwkv_r_k[l])
        o_rwkv = _rwkv_scan(mats, n_batch, seq_len, rwkv_ln_w[l], rwkv_ln_b[l])
        o_attn = _attn(p_attn, n_batch, seq_len, attn_sinks[l])
        h = _ffn(h, norm_ffn2[l], ffn2_gate[l], ffn2_up[l], ffn2_down[l],
                 mix=(o_rwkv, o_attn, w_out[l]),
                 final_norm=norm_final if l == depth - 1 else None)
    return h.reshape(n_batch, seq_len, d)
```

```python
import functools
import math

import jax
import jax.numpy as jnp
from jax import lax
from jax.experimental import pallas as pl
from jax.experimental.pallas import tpu as pltpu

F32 = jnp.float32
BF16 = jnp.bfloat16

HEAD_DIM = 64
LANES = 128
SLAB = 128
HEADS_PER_SLAB = SLAB // HEAD_DIM
CHUNK = 64
N_RWKV_HEADS = 8
RWKV_WIDTH = N_RWKV_HEADS * HEAD_DIM
N_SLABS = RWKV_WIDTH // SLAB
DECAY_LORA = 64
AAA_LORA = 64
GATE_LORA = 128
RWKV_COLS = 3 * RWKV_WIDTH + DECAY_LORA + AAA_LORA + GATE_LORA
N_Q_HEADS = 8
N_KV_HEADS = 2
ATTN_WIDTH = N_Q_HEADS * HEAD_DIM
KV_WIDTH = N_KV_HEADS * HEAD_DIM
ATTN_COLS = ATTN_WIDTH + 2 * KV_WIDTH
WINDOW = 128
GN_EPS = 64e-5
NORM_EPS = 1e-5
VMEM_LIMIT = 56 * 1024 * 1024

FFN_ROWS = 1024
FFN_COLS = 256
INPROJ_ROWS = 512
RWKV_ROWS = 256
ATTN_ROWS = 512


def _dot(a, b):
    return jnp.dot(a, b, preferred_element_type=F32)


def _dot_nt(a, b):
    return lax.dot_general(a, b, (((1,), (1,)), ((), ())), preferred_element_type=F32)


def _rms(x, g):
    ms = jnp.mean(x * x, axis=-1, keepdims=True)
    return x * lax.rsqrt(ms + NORM_EPS) * g


def _lane_block(shape, width, span=SLAB):
    lane = lax.broadcasted_iota(jnp.int32, shape, 1)
    return (lane & (span - 1)) >> (width.bit_length() - 1)


def _block_diag(x, width=HEAD_DIM):
    blk = _lane_block(x.shape, width)
    return jnp.concatenate([jnp.where(blk == g, x, 0.0) for g in range(SLAB // width)], axis=0)


def _seg_sum(x, seg):
    xb = x.astype(BF16)
    rows, groups = x.shape[0], x.shape[1] // LANES
    y = _dot(jnp.concatenate([xb[:, j * LANES:(j + 1) * LANES] for j in range(groups)], axis=0), seg)
    return jnp.concatenate([y[j * rows:(j + 1) * rows] for j in range(groups)], axis=1)


def _resident(shape):
    return pl.BlockSpec(shape, lambda *_: (0,) * len(shape), pipeline_mode=pl.Buffered(1))


def _ffn_kernel(*refs, has_mix, final_norm, nf, tf):
    refs = list(refs)
    x_ref = refs.pop(0)
    if has_mix:
        orw_ref, oat_ref, wor_ref, woa_ref = refs[:4]
        refs = refs[4:]
    g_ref, wg_ref, wu_ref, wd_ref = refs[:4]
    refs = refs[4:]
    if final_norm:
        gf_ref = refs.pop(0)
    out_ref, h_s, acc = refs

    x = x_ref[...]
    if has_mix:
        x = x + _dot(orw_ref[...], wor_ref[...]) + _dot(oat_ref[...], woa_ref[...])
        out_ref[...] = x
    h_s[...] = _rms(x, g_ref[...]).astype(BF16)

    def down(f):
        cols = pl.ds(f * tf if isinstance(f, int) else pl.multiple_of(f * tf, tf), tf)
        h = h_s[...]
        gate = _dot(h, wg_ref[:, cols])
        up = _dot(h, wu_ref[:, cols])
        act = (gate * jax.nn.sigmoid(gate) * up).astype(BF16)
        return _dot(act, wd_ref[cols, :])

    acc[...] = down(0)

    for f in range(1, nf):
        acc[...] += down(f)
    res = out_ref[...] if has_mix else x_ref[...]
    y = res + 0.5 * acc[...]
    if final_norm:
        y = _rms(y, gf_ref[...])
    out_ref[...] = y


def _ffn(x, norm, w_gate, w_up, w_down, mix=None, final_norm=None):
    n, d = x.shape
    f = w_gate.shape[1]
    tm, tf = FFN_ROWS, FFN_COLS
    nf = f // tf
    row = lambda i: (i, 0)
    wg, wu, wd = w_gate.astype(BF16), w_up.astype(BF16), w_down.astype(BF16)
    args = [x]
    specs = [pl.BlockSpec((tm, d), row)]
    if mix is not None:
        o_rwkv, o_attn, w_out = mix
        wo = w_out.astype(BF16)
        args += [o_rwkv, o_attn, wo[:RWKV_WIDTH], wo[RWKV_WIDTH:]]
        specs += [pl.BlockSpec((tm, RWKV_WIDTH), row), pl.BlockSpec((tm, ATTN_WIDTH), row),
                  _resident((RWKV_WIDTH, d)), _resident((ATTN_WIDTH, d))]
    args += [norm.reshape(1, d), wg, wu, wd]
    specs += [_resident((1, d)), _resident((d, f)), _resident((d, f)), _resident((f, d))]
    if final_norm is not None:
        args.append(final_norm.reshape(1, d))
        specs.append(_resident((1, d)))
    return pl.pallas_call(
        functools.partial(_ffn_kernel, has_mix=mix is not None,
                          final_norm=final_norm is not None, nf=nf, tf=tf),
        grid=(n // tm,),
        in_specs=specs,
        out_specs=pl.BlockSpec((tm, d), row),
        out_shape=jax.ShapeDtypeStruct((n, d), F32),
        scratch_shapes=[pltpu.VMEM((tm, d), BF16), pltpu.VMEM((tm, d), F32)],
        compiler_params=pltpu.CompilerParams(
            dimension_semantics=("parallel",), vmem_limit_bytes=VMEM_LIMIT),
        name="ffn_mix" if mix is not None else "ffn",
    )(*args)


def _inproj_kernel(x_ref, g_ref, w_ref, b_ref, mix_ref, prw_ref, pat_ref, last_row,
                   *, tiles_per_seq):
    h = _rms(x_ref[...], g_ref[...]).astype(BF16)
    p = _dot(h, w_ref[...]) + b_ref[...]
    pat_ref[...] = p[:, RWKV_COLS:]
    pr = p[:, :RWKV_COLS]
    prev_row = jnp.where(pl.program_id(0) % tiles_per_seq == 0, 0.0, last_row[0:1, :])
    row = lax.broadcasted_iota(jnp.int32, pr.shape, 0)
    p_prev = jnp.where(row == 0, prev_row, pltpu.roll(pr, 1, axis=0))
    prw_ref[...] = pr + (p_prev - pr) * mix_ref[...]
    last_row[0:1, :] = pr[pr.shape[0] - 1:, :]


def _in_proj(x, seq_len, norm, w_in, b_attn, shift_mix):
    n, d = x.shape
    cols = w_in.shape[1]
    tm = INPROJ_ROWS
    bias = jnp.concatenate([jnp.zeros((RWKV_COLS,), F32), b_attn.astype(F32)]).reshape(1, cols)
    row = lambda i: (i, 0)
    return pl.pallas_call(
        functools.partial(_inproj_kernel, tiles_per_seq=seq_len // tm),
        grid=(n // tm,),
        in_specs=[pl.BlockSpec((tm, d), row), _resident((1, d)), _resident((d, cols)),
                  _resident((1, cols)), _resident((1, RWKV_COLS))],
        out_specs=[pl.BlockSpec((tm, RWKV_COLS), row), pl.BlockSpec((tm, ATTN_COLS), row)],
        out_shape=[jax.ShapeDtypeStruct((n, RWKV_COLS), F32),
                   jax.ShapeDtypeStruct((n, ATTN_COLS), F32)],
        scratch_shapes=[pltpu.VMEM((8, RWKV_COLS), F32)],
        compiler_params=pltpu.CompilerParams(
            dimension_semantics=("arbitrary",), vmem_limit_bytes=VMEM_LIMIT),
        name="in_proj",
    )(x, norm.reshape(1, d), w_in.astype(BF16), bias, shift_mix.astype(F32).reshape(1, -1))


def _rwkv_prep_kernel(ps_ref, w0_ref, w2_ref, a0_ref, a2_ref, g2_ref, kk_ref, ka_ref, rk_ref,
                      seg_ref, lcum_ref,
                      m_ref, n_ref, r_ref, o_ref, bonus_ref, gate_ref,
                      at_s, bt_s, kt_s, rt_s, v_s, bh_s, kh_s, rf_s, gam_s, *, n_chunks):
    C = CHUNK
    W = RWKV_WIDTH
    ps = ps_ref[...]
    r = ps[:, 0:W]
    k = ps[:, W:2 * W]
    v = ps[:, 2 * W:3 * W]
    wa = ps[:, 3 * W:3 * W + LANES]
    gl = ps[:, 3 * W + LANES:3 * W + 2 * LANES]
    z = w0_ref[...] + _dot(jnp.tanh(wa).astype(BF16), w2_ref[...])
    lw = -math.exp(-0.5) * jax.nn.sigmoid(z)
    a = jax.nn.sigmoid(a0_ref[...] + _dot(wa.astype(BF16), a2_ref[...]))
    gate_ref[...] = _dot(jax.nn.sigmoid(gl).astype(BF16), g2_ref[...]).astype(BF16)
    seg = seg_ref[...]
    kk = k * kk_ref[...]
    kk = kk * jnp.minimum(lax.rsqrt(_seg_sum(kk * kk, seg)), 1e12)
    k = k * (1.0 + (a - 1.0) * ka_ref[...])
    b = kk * a
    bonus_ref[...] = (_seg_sum(r * k * rk_ref[...], seg) * v).astype(BF16)
    v_s[...] = v.astype(BF16)
    lcum = lcum_ref[...]
    lw_hi = lw.astype(BF16)
    lw_lo = (lw - lw_hi.astype(F32)).astype(BF16)
    cs = _dot(lcum, lw_hi) + _dot(lcum, lw_lo)
    for c in range(n_chunks):
        rows = slice(c * C, (c + 1) * C)
        cs_c = cs[rows]
        gam = jnp.exp(cs[(c + 1) * C - 1:(c + 1) * C, :])
        e_neg = jnp.exp(-cs_c)
        rt = r[rows] * jnp.exp(cs_c)
        bt = b[rows] * e_neg
        kt = k[rows] * e_neg
        at_s[rows, :] = (-kk[rows] * jnp.exp(cs_c - lw[rows])).astype(BF16)
        bt_s[rows, :] = bt.astype(BF16)
        kt_s[rows, :] = kt.astype(BF16)
        rt_s[rows, :] = rt.astype(BF16)
        rf_s[rows, :] = rt
        bh_s[rows, :] = bt * gam
        kh_s[rows, :] = kt * gam
        gam_s[c:c + 1, :] = gam

    rowi = lax.broadcasted_iota(jnp.int32, (2 * C, 2 * SLAB), 0)
    lane_s = lax.broadcasted_iota(jnp.int32, (2 * C, 2 * SLAB), 1) & (C - 1)
    keep_a1 = lane_s < (rowi & (C - 1)) + jnp.where(rowi < C, 0, 1)
    eye_pack = jnp.where((lax.broadcasted_iota(jnp.int32, (C, SLAB), 1) & (C - 1))
                         == lax.broadcasted_iota(jnp.int32, (C, SLAB), 0), 1.0, 0.0).astype(F32)
    same_head = ((lax.broadcasted_iota(jnp.int32, (SLAB, 2 * SLAB), 0) >> 6)
                 == _lane_block((SLAB, 2 * SLAB), HEAD_DIM))
    diag = (lax.broadcasted_iota(jnp.int32, (SLAB, SLAB), 0)
            == lax.broadcasted_iota(jnp.int32, (SLAB, SLAB), 1))
    zeros_c = jnp.zeros((C, SLAB), BF16)

    def fold(x):
        out = x[:HEAD_DIM]
        for h in range(1, HEADS_PER_SLAB):
            out = out + x[h * HEAD_DIM:(h + 1) * HEAD_DIM]
        return out

    chains = [(c, s) for c in range(n_chunks) for s in range(N_SLABS)]
    ld = lambda ref: [ref[c * C:(c + 1) * C, s * SLAB:(s + 1) * SLAB] for c, s in chains]
    at, bt, kt, rt, vv = ld(at_s), ld(bt_s), ld(kt_s), ld(rt_s), ld(v_s)
    a1 = [jnp.where(keep_a1,
                    _dot_nt(jnp.concatenate([x, y], axis=0),
                            jnp.concatenate([_block_diag(p), _block_diag(q)], axis=0)), 0.0)
          for x, y, p, q in zip(at, rt, bt, kt)]
    aab = [x[:C, :SLAB] for x in a1]
    kv = [_dot(x[:, SLAB:].astype(BF16), _block_diag(y)) for x, y in zip(a1, vv)]
    akv = [x[:C].astype(BF16) for x in kv]
    arkv = [x[C:] for x in kv]
    tinv = [eye_pack + x for x in aab]
    pwb = [x.astype(BF16) for x in aab]
    pwb = [_dot(x, _block_diag(x)).astype(BF16) for x in pwb]
    for step in range(5):
        if step < 4:
            res = [_dot(jnp.concatenate([p, t.astype(BF16)], axis=0), _block_diag(p))
                   for t, p in zip(tinv, pwb)]
            pwb = [x[:C].astype(BF16) for x in res]
            tinv = [t + x[C:] for t, x in zip(tinv, res)]
        else:
            tinv = [t + _dot(t.astype(BF16), _block_diag(p)) for t, p in zip(tinv, pwb)]
    wu = [_dot(t.astype(BF16), _block_diag(jnp.concatenate([x, y], axis=1))).astype(BF16)
          for t, x, y in zip(tinv, at, akv)]
    for i, (c, s) in enumerate(chains):
        rows = slice(c * C, (c + 1) * C)
        sl = slice(s * SLAB, (s + 1) * SLAB)
        lhs_t = jnp.concatenate([bh_s[rows, sl], kh_s[rows, sl]], axis=0)
        rhs = jnp.concatenate([wu[i], jnp.concatenate([zeros_c, vv[i]], axis=1)], axis=0)
        mn = _dot(lhs_t.T.astype(BF16), rhs)
        mn = jnp.where(same_head, mn, 0.0)
        m_ref[rows, sl] = fold(mn[:, :SLAB] + jnp.where(diag, gam_s[c:c + 1, sl], 0.0)).astype(BF16)
        n_ref[rows, sl] = fold(mn[:, SLAB:])
        ro = _dot(a1[i][C:, :SLAB].astype(BF16), _block_diag(wu[i]))
        r_ref[rows, sl] = (rf_s[rows, sl] + ro[:, :SLAB]).astype(BF16)
        o_ref[rows, sl] = (ro[:, SLAB:] + arkv[i]).astype(BF16)


def _rwkv_prep(p_shift, w0, w2, a0, a2, g2, k_k, k_a, r_k):
    n = p_shift.shape[0]
    W = RWKV_WIDTH
    tt = RWKV_ROWS
    lane_head = jnp.arange(LANES) // HEAD_DIM
    seg = (lane_head[:, None] == lane_head[None, :]).astype(BF16)
    t = jnp.arange(tt)
    lcum = ((t[:, None] >= t[None, :]) & (t[:, None] // CHUNK == t[None, :] // CHUNK)).astype(BF16)
    w2p = jnp.concatenate([w2, jnp.zeros((AAA_LORA, W), w2.dtype)], axis=0).astype(BF16)
    a2p = jnp.concatenate([jnp.zeros((DECAY_LORA, W), a2.dtype), a2], axis=0).astype(BF16)
    row = lambda i: (i, 0)
    vec = lambda t: t.astype(F32).reshape(1, -1)
    f32_out = jax.ShapeDtypeStruct((n, W), F32)
    bf16_out = jax.ShapeDtypeStruct((n, W), BF16)
    return pl.pallas_call(
        functools.partial(_rwkv_prep_kernel, n_chunks=tt // CHUNK),
        grid=(n // tt,),
        in_specs=[pl.BlockSpec((tt, RWKV_COLS), row),
                  _resident((1, W)), _resident((LANES, W)), _resident((1, W)),
                  _resident((LANES, W)), _resident((GATE_LORA, W)),
                  _resident((1, W)), _resident((1, W)), _resident((1, W)),
                  _resident((LANES, LANES)), _resident((tt, tt))],
        out_specs=[pl.BlockSpec((tt, W), row)] * 6,
        out_shape=[bf16_out, f32_out, bf16_out, bf16_out, bf16_out, bf16_out],
        scratch_shapes=[pltpu.VMEM((tt, W), BF16)] * 5 + [pltpu.VMEM((tt, W), F32)] * 3
                       + [pltpu.VMEM((8, W), F32)],
        compiler_params=pltpu.CompilerParams(
            dimension_semantics=("parallel",), vmem_limit_bytes=VMEM_LIMIT),
        name="rwkv_prep",
    )(p_shift, vec(w0), w2p, vec(a0), a2p, g2.astype(BF16), vec(k_k), vec(k_a), vec(r_k),
      seg, lcum)


def _rwkv_scan_kernel(m_ref, n_ref, r_ref, o_ref, bonus_ref, gate_ref, lnw_ref, lnb_ref, seg_ref,
                      out_ref, state, obuf, *, n_batch, n_chunks):
    @pl.when(pl.program_id(0) == 0)
    def _():
        state[...] = jnp.zeros_like(state)

    C = CHUNK
    for c in range(n_chunks):
        rows = slice(c * C, (c + 1) * C)
        for b in range(n_batch):
            for s in range(N_SLABS):
                sl = slice(s * SLAB, (s + 1) * SLAB)
                s0 = state[b * N_SLABS + s]
                lhs = jnp.concatenate([m_ref[b, rows, sl], r_ref[b, rows, sl]], axis=0)
                res = _dot(lhs, _block_diag(s0.astype(BF16)))
                state[b * N_SLABS + s] = res[:C] + n_ref[b, rows, sl]
                obuf[b, rows, sl] = res[C:] + o_ref[b, rows, sl]

    seg_mean = seg_ref[...]
    for b in range(n_batch):
        o = obuf[b]
        d = o - _seg_sum(o, seg_mean)
        var = _seg_sum(d * d, seg_mean)
        y = d * lax.rsqrt(var + GN_EPS) * lnw_ref[...] + lnb_ref[...]
        out_ref[b] = ((y + bonus_ref[b]) * gate_ref[b]).astype(BF16)


def _rwkv_scan(mats, n_batch, seq_len, ln_w, ln_b):
    W = RWKV_WIDTH
    tt = RWKV_ROWS
    lane_head = jnp.arange(LANES) // HEAD_DIM
    seg_mean = ((lane_head[:, None] == lane_head[None, :]).astype(F32) / HEAD_DIM).astype(BF16)
    mats = [t.reshape(n_batch, seq_len, W) for t in mats]
    blk = pl.BlockSpec((n_batch, tt, W), lambda i: (0, i, 0))
    out = pl.pallas_call(
        functools.partial(_rwkv_scan_kernel, n_batch=n_batch, n_chunks=tt // CHUNK),
        grid=(seq_len // tt,),
        in_specs=[blk] * 6 + [_resident((1, W)), _resident((1, W)), _resident((LANES, LANES))],
        out_specs=blk,
        out_shape=jax.ShapeDtypeStruct((n_batch, seq_len, W), BF16),
        scratch_shapes=[pltpu.VMEM((n_batch * N_SLABS, CHUNK, SLAB), F32),
                        pltpu.VMEM((n_batch, tt, W), F32)],
        compiler_params=pltpu.CompilerParams(
            dimension_semantics=("arbitrary",), vmem_limit_bytes=VMEM_LIMIT),
        name="rwkv_scan",
    )(*mats, ln_w.astype(F32).reshape(1, W), ln_b.astype(F32).reshape(1, W), seg_mean)
    return out.reshape(n_batch * seq_len, W)


def _attn_kernel(sink_ref, q_ref, kp_ref, kc_ref, vp_ref, vc_ref, out_ref, *, n_blocks):
    BQ = WINDOW
    first = pl.program_id(1) == 0
    kx = jnp.concatenate([kp_ref[...], kc_ref[...]], axis=0)
    vx = jnp.concatenate([vp_ref[...], vc_ref[...]], axis=0)
    lo = _lane_block(kx.shape, HEAD_DIM, LANES) == 0
    kxr = pltpu.roll(kx, HEAD_DIM, axis=1)
    vxr = pltpu.roll(vx, HEAD_DIM, axis=1)
    bf = lambda t: t.astype(BF16)
    k_lo = [bf(jnp.where(lo, kx, 0.0)), bf(jnp.where(lo, kxr, 0.0))]
    k_hi = [bf(jnp.where(lo, 0.0, kxr)), bf(jnp.where(lo, 0.0, kx))]
    v_lo = [bf(jnp.where(lo, vx, 0.0)), bf(jnp.where(lo, vxr, 0.0))]
    v_hi = [bf(jnp.where(lo, 0.0, vxr)), bf(jnp.where(lo, 0.0, vx))]

    qi = lax.broadcasted_iota(jnp.int32, (BQ, 2 * BQ), 0)
    kj = lax.broadcasted_iota(jnp.int32, (BQ, 2 * BQ), 1)
    dist = qi + BQ - kj
    band = (dist >= 0) & (dist < WINDOW)
    band_first = band & (kj >= jnp.where(first, BQ, 0))
    col0 = lax.broadcasted_iota(jnp.int32, (1, 2 * BQ), 1) == 0
    log2e = math.log2(math.e)
    fill = [jnp.where(col0, sink_ref[h] * log2e, -jnp.inf) for h in range(N_Q_HEADS)]
    vrow = lax.broadcasted_iota(jnp.int32, (4 * BQ, LANES), 0) & (2 * BQ - 1)
    ones_sel = jnp.where((lax.broadcasted_iota(jnp.int32, (4 * BQ, LANES), 0) < 2 * BQ)
                         == (_lane_block((4 * BQ, LANES), HEAD_DIM, LANES) == 0),
                         1.0, 0.0).astype(BF16)
    qscale = HEAD_DIM ** -0.5 * log2e
    n_pairs = N_Q_HEADS // 2

    kv_of = [(2 * j) // (N_Q_HEADS // N_KV_HEADS) for j in range(n_pairs)]
    units = [(qb, j) for qb in range(n_blocks) for j in range(n_pairs)]
    kcat, vcat = {}, {}
    for qb in range(n_blocks):
        krows = slice(qb * BQ, (qb + 2) * BQ)
        for g in range(N_KV_HEADS):
            kcat[qb, g] = jnp.concatenate([k_lo[g][krows], k_hi[g][krows]], axis=0)
            vcat[qb, g] = jnp.concatenate(
                [jnp.where(vrow == 0, 0.0,
                           jnp.concatenate([v_lo[g][krows], v_hi[g][krows]], axis=0)),
                 ones_sel], axis=1)
    s = [_dot_nt((q_ref[qb * BQ:(qb + 1) * BQ, j * LANES:(j + 1) * LANES] * qscale).astype(BF16),
                 kcat[qb, kv_of[j]]) for qb, j in units]
    sh = [[jnp.where(band_first if qb == 0 else band,
                     x[:, hh * 2 * BQ:(hh + 1) * 2 * BQ], fill[2 * j + hh])
           for hh in range(2)] for x, (qb, j) in zip(s, units)]
    mx = [[jnp.max(x, axis=-1, keepdims=True) for x in row] for row in sh]
    e = [jnp.concatenate([jnp.exp2(x - m).astype(BF16) for x, m in zip(xr, mr)], axis=1)
         for xr, mr in zip(sh, mx)]
    pv = [_dot(x, vcat[qb, kv_of[j]]) for x, (qb, j) in zip(e, units)]
    for x, (qb, j) in zip(pv, units):
        out_ref[qb * BQ:(qb + 1) * BQ, j * LANES:(j + 1) * LANES] = (
            x[:, :LANES] / x[:, LANES:]).astype(BF16)


def _attn(p_attn, n_batch, seq_len, sinks):
    n = p_attn.shape[0]
    rows = ATTN_ROWS
    n_blocks = rows // WINDOW
    steps = seq_len // rows
    kcol = ATTN_WIDTH // LANES
    vcol = kcol + KV_WIDTH // LANES
    cur = lambda b, i: b * steps + i
    prev = lambda b, i: (b * steps + i) * n_blocks - jnp.minimum(i, 1)
    return pl.pallas_call(
        functools.partial(_attn_kernel, n_blocks=n_blocks),
        grid=(n_batch, steps),
        in_specs=[pl.BlockSpec(memory_space=pltpu.SMEM),
                  pl.BlockSpec((rows, ATTN_WIDTH), lambda b, i: (cur(b, i), 0)),
                  pl.BlockSpec((WINDOW, KV_WIDTH), lambda b, i: (prev(b, i), kcol)),
                  pl.BlockSpec((rows, KV_WIDTH), lambda b, i: (cur(b, i), kcol)),
                  pl.BlockSpec((WINDOW, KV_WIDTH), lambda b, i: (prev(b, i), vcol)),
                  pl.BlockSpec((rows, KV_WIDTH), lambda b, i: (cur(b, i), vcol))],
        out_specs=pl.BlockSpec((rows, ATTN_WIDTH), lambda b, i: (cur(b, i), 0)),
        out_shape=jax.ShapeDtypeStruct((n, ATTN_WIDTH), BF16),
        compiler_params=pltpu.CompilerParams(
            dimension_semantics=("parallel", "arbitrary"), vmem_limit_bytes=VMEM_LIMIT),
        name="swa_attn",
    )(sinks.astype(F32), p_attn, p_attn, p_attn, p_attn, p_attn)


def kernel(x, norm_ffn1, ffn1_gate, ffn1_up, ffn1_down, norm_mix, w_in, b_in_attn, rwkv_shift_mix, rwkv_w0, rwkv_w2, rwkv_a0, rwkv_a2, rwkv_g2, rwkv_k_k, rwkv_k_a, rwkv_r_k, rwkv_ln_w, rwkv_ln_b, attn_sinks, w_out, norm_ffn2, ffn2_gate, ffn2_up, ffn2_down, norm_final):
    n_batch, seq_len, d = x.shape
    depth = w_in.shape[0]
    h = x.reshape(n_batch * seq_len, d)
    for l in range(depth):
        h = _ffn(h, norm_ffn1[l], ffn1_gate[l], ffn1_up[l], ffn1_down[l])
        p_shift, p_attn = _in_proj(h, seq_len, norm_mix[l], w_in[l], b_in_attn[l],
                                   rwkv_shift_mix[l])
        mats = _rwkv_prep(p_shift, rwkv_w0[l], rwkv_w2[l], rwkv_a0[l], rwkv_a2[l], rwkv_g2[l],
                          rwkv_k_k[l], rwkv_k_a[l], rwkv_r_k[l])
        o_rwkv = _rwkv_scan(mats, n_batch, seq_len, rwkv_ln_w[l], rwkv_ln_b[l])
        o_attn = _attn(p_attn, n_batch, seq_len, attn_sinks[l])
        h = _ffn(h, norm_ffn2[l], ffn2_gate[l], ffn2_up[l], ffn2_down[l],
                 mix=(o_rwkv, o_attn, w_out[l]),
                 final_norm=norm_final if l == depth - 1 else None)
    return h.reshape(n_batch, seq_len, d)
```

```python
import functools
import math

import jax
import jax.numpy as jnp
from jax import lax
from jax.experimental import pallas as pl
from jax.experimental.pallas import tpu as pltpu

F32 = jnp.float32
BF16 = jnp.bfloat16

HEAD_DIM = 64
LANES = 128
SLAB = 128
HEADS_PER_SLAB = SLAB // HEAD_DIM
CHUNK = 64
N_RWKV_HEADS = 8
RWKV_WIDTH = N_RWKV_HEADS * HEAD_DIM
N_SLABS = RWKV_WIDTH // SLAB
DECAY_LORA = 64
AAA_LORA = 64
GATE_LORA = 128
RWKV_COLS = 3 * RWKV_WIDTH + DECAY_LORA + AAA_LORA + GATE_LORA
N_Q_HEADS = 8
N_KV_HEADS = 2
ATTN_WIDTH = N_Q_HEADS * HEAD_DIM
KV_WIDTH = N_KV_HEADS * HEAD_DIM
ATTN_COLS = ATTN_WIDTH + 2 * KV_WIDTH
WINDOW = 128
GN_EPS = 64e-5
NORM_EPS = 1e-5
VMEM_LIMIT = 56 * 1024 * 1024

FFN_ROWS = 1024
FFN_COLS = 256
INPROJ_ROWS = 1024
RWKV_ROWS = 512
ATTN_ROWS = 512


def _dot(a, b):
    return jnp.dot(a, b, preferred_element_type=F32)


def _dot_nt(a, b):
    return lax.dot_general(a, b, (((1,), (1,)), ((), ())), preferred_element_type=F32)


def _rms(x, g):
    ms = jnp.mean(x * x, axis=-1, keepdims=True)
    return x * lax.rsqrt(ms + NORM_EPS) * g


def _lane_block(shape, width, span=SLAB):
    lane = lax.broadcasted_iota(jnp.int32, shape, 1)
    return (lane & (span - 1)) >> (width.bit_length() - 1)


def _block_diag(x, width=HEAD_DIM):
    blk = _lane_block(x.shape, width)
    return jnp.concatenate([jnp.where(blk == g, x, 0.0) for g in range(SLAB // width)], axis=0)


def _seg_sum(x, seg):
    xb = x.astype(BF16)
    rows, groups = x.shape[0], x.shape[1] // LANES
    y = _dot(jnp.concatenate([xb[:, j * LANES:(j + 1) * LANES] for j in range(groups)], axis=0), seg)
    return jnp.concatenate([y[j * rows:(j + 1) * rows] for j in range(groups)], axis=1)


def _resident(shape):
    return pl.BlockSpec(shape, lambda *_: (0,) * len(shape), pipeline_mode=pl.Buffered(1))


def _ffn_kernel(*refs, has_mix, final_norm, nf, tf):
    refs = list(refs)
    x_ref = refs.pop(0)
    if has_mix:
        orw_ref, oat_ref, wor_ref, woa_ref = refs[:4]
        refs = refs[4:]
    g_ref, wg_ref, wu_ref, wd_ref = refs[:4]
    refs = refs[4:]
    if final_norm:
        gf_ref = refs.pop(0)
    out_ref, h_s, acc = refs

    x = x_ref[...]
    if has_mix:
        x = x + _dot(orw_ref[...], wor_ref[...]) + _dot(oat_ref[...], woa_ref[...])
        out_ref[...] = x
    h_s[...] = _rms(x, g_ref[...]).astype(BF16)

    def down(f):
        cols = pl.ds(f * tf if isinstance(f, int) else pl.multiple_of(f * tf, tf), tf)
        h = h_s[...]
        gate = _dot(h, wg_ref[:, cols])
        up = _dot(h, wu_ref[:, cols])
        act = (gate * jax.nn.sigmoid(gate) * up).astype(BF16)
        return _dot(act, wd_ref[cols, :])

    acc[...] = down(0)

    for f in range(1, nf):
        acc[...] += down(f)
    res = out_ref[...] if has_mix else x_ref[...]
    y = res + 0.5 * acc[...]
    if final_norm:
        y = _rms(y, gf_ref[...])
    out_ref[...] = y


def _ffn(x, norm, w_gate, w_up, w_down, mix=None, final_norm=None):
    n, d = x.shape
    f = w_gate.shape[1]
    tm, tf = FFN_ROWS, FFN_COLS
    nf = f // tf
    row = lambda i: (i, 0)
    wg, wu, wd = w_gate.astype(BF16), w_up.astype(BF16), w_down.astype(BF16)
    args = [x]
    specs = [pl.BlockSpec((tm, d), row)]
    if mix is not None:
        o_rwkv, o_attn, w_out = mix
        wo = w_out.astype(BF16)
        args += [o_rwkv, o_attn, wo[:RWKV_WIDTH], wo[RWKV_WIDTH:]]
        specs += [pl.BlockSpec((tm, RWKV_WIDTH), row), pl.BlockSpec((tm, ATTN_WIDTH), row),
                  _resident((RWKV_WIDTH, d)), _resident((ATTN_WIDTH, d))]
    args += [norm.reshape(1, d), wg, wu, wd]
    specs += [_resident((1, d)), _resident((d, f)), _resident((d, f)), _resident((f, d))]
    if final_norm is not None:
        args.append(final_norm.reshape(1, d))
        specs.append(_resident((1, d)))
    return pl.pallas_call(
        functools.partial(_ffn_kernel, has_mix=mix is not None,
                          final_norm=final_norm is not None, nf=nf, tf=tf),
        grid=(n // tm,),
        in_specs=specs,
        out_specs=pl.BlockSpec((tm, d), row),
        out_shape=jax.ShapeDtypeStruct((n, d), F32),
        scratch_shapes=[pltpu.VMEM((tm, d), BF16), pltpu.VMEM((tm, d), F32)],
        compiler_params=pltpu.CompilerParams(
            dimension_semantics=("parallel",), vmem_limit_bytes=VMEM_LIMIT),
        name="ffn_mix" if mix is not None else "ffn",
    )(*args)


def _inproj_kernel(x_ref, g_ref, w_ref, b_ref, mix_ref, prw_ref, pat_ref, last_row,
                   *, tiles_per_seq):
    h = _rms(x_ref[...], g_ref[...]).astype(BF16)
    p = _dot(h, w_ref[...])
    pat_ref[...] = p[:, RWKV_COLS:] + b_ref[...]
    pr = p[:, :RWKV_COLS]
    prev_row = jnp.where(pl.program_id(0) % tiles_per_seq == 0, 0.0, last_row[0:1, :])
    row = lax.broadcasted_iota(jnp.int32, pr.shape, 0)
    p_prev = jnp.where(row == 0, prev_row, pltpu.roll(pr, 1, axis=0))
    prw_ref[...] = pr + (p_prev - pr) * mix_ref[...]
    last_row[0:1, :] = pr[pr.shape[0] - 1:, :]


def _in_proj(x, seq_len, norm, w_in, b_attn, shift_mix):
    n, d = x.shape
    cols = w_in.shape[1]
    tm = INPROJ_ROWS
    bias = b_attn.astype(F32).reshape(1, ATTN_COLS)
    row = lambda i: (i, 0)
    return pl.pallas_call(
        functools.partial(_inproj_kernel, tiles_per_seq=seq_len // tm),
        grid=(n // tm,),
        in_specs=[pl.BlockSpec((tm, d), row), _resident((1, d)), _resident((d, cols)),
                  _resident((1, ATTN_COLS)), _resident((1, RWKV_COLS))],
        out_specs=[pl.BlockSpec((tm, RWKV_COLS), row), pl.BlockSpec((tm, ATTN_COLS), row)],
        out_shape=[jax.ShapeDtypeStruct((n, RWKV_COLS), F32),
                   jax.ShapeDtypeStruct((n, ATTN_COLS), F32)],
        scratch_shapes=[pltpu.VMEM((8, RWKV_COLS), F32)],
        compiler_params=pltpu.CompilerParams(
            dimension_semantics=("arbitrary",), vmem_limit_bytes=VMEM_LIMIT),
        name="in_proj",
    )(x, norm.reshape(1, d), w_in.astype(BF16), bias, shift_mix.astype(F32).reshape(1, -1))


def _rwkv_prep_kernel(ps_ref, w0_ref, w2_ref, a0_ref, a2_ref, g2_ref, kk_ref, ka_ref, rk_ref,
                      seg_ref,
                      m_ref, n_ref, r_ref, o_ref, bonus_ref, gate_ref,
                      at_s, bt_s, kt_s, rt_s, v_s, bh_s, kh_s, rf_s, gam_s, *, n_chunks):
    C = CHUNK
    W = RWKV_WIDTH
    ps = ps_ref[...]
    r = ps[:, 0:W]
    k = ps[:, W:2 * W]
    v = ps[:, 2 * W:3 * W]
    wa = ps[:, 3 * W:3 * W + LANES]
    gl = ps[:, 3 * W + LANES:3 * W + 2 * LANES]
    z = w0_ref[...] + _dot(jnp.tanh(wa).astype(BF16), w2_ref[...])
    lw = -math.exp(-0.5) * jax.nn.sigmoid(z)
    a = jax.nn.sigmoid(a0_ref[...] + _dot(wa.astype(BF16), a2_ref[...]))
    gate_ref[...] = _dot(jax.nn.sigmoid(gl).astype(BF16), g2_ref[...]).astype(BF16)
    seg = seg_ref[...]
    kk = k * kk_ref[...]
    kk = kk * jnp.minimum(lax.rsqrt(_seg_sum(kk * kk, seg)), 1e12)
    k = k * (1.0 + (a - 1.0) * ka_ref[...])
    b = kk * a
    bonus_ref[...] = (_seg_sum(r * k * rk_ref[...], seg) * v).astype(BF16)
    v_s[...] = v.astype(BF16)
    cs = lw
    row_in_chunk = lax.broadcasted_iota(jnp.int32, lw.shape, 0) & (C - 1)
    for shift in (1, 2, 4, 8, 16, 32):
        cs = cs + jnp.where(row_in_chunk >= shift, pltpu.roll(cs, shift, axis=0), 0.0)
    for c in range(n_chunks):
        rows = slice(c * C, (c + 1) * C)
        cs_c = cs[rows]
        gam = jnp.exp(cs[(c + 1) * C - 1:(c + 1) * C, :])
        e_neg = jnp.exp(-cs_c)
        rt = r[rows] * jnp.exp(cs_c)
        bt = b[rows] * e_neg
        kt = k[rows] * e_neg
        at_s[rows, :] = (-kk[rows] * jnp.exp(cs_c - lw[rows])).astype(BF16)
        bt_s[rows, :] = bt.astype(BF16)
        kt_s[rows, :] = kt.astype(BF16)
        rt_s[rows, :] = rt.astype(BF16)
        rf_s[rows, :] = rt
        bh_s[rows, :] = bt * gam
        kh_s[rows, :] = kt * gam
        gam_s[c:c + 1, :] = gam

    rowi = lax.broadcasted_iota(jnp.int32, (2 * C, 2 * SLAB), 0)
    lane_s = lax.broadcasted_iota(jnp.int32, (2 * C, 2 * SLAB), 1) & (C - 1)
    keep_a1 = lane_s < (rowi & (C - 1)) + jnp.where(rowi < C, 0, 1)
    eye_pack = jnp.where((lax.broadcasted_iota(jnp.int32, (C, SLAB), 1) & (C - 1))
                         == lax.broadcasted_iota(jnp.int32, (C, SLAB), 0), 1.0, 0.0).astype(F32)
    same_head = ((lax.broadcasted_iota(jnp.int32, (SLAB, 2 * SLAB), 0) >> 6)
                 == _lane_block((SLAB, 2 * SLAB), HEAD_DIM))
    diag = (lax.broadcasted_iota(jnp.int32, (SLAB, SLAB), 0)
            == lax.broadcasted_iota(jnp.int32, (SLAB, SLAB), 1))
    zeros_c = jnp.zeros((C, SLAB), BF16)

    def fold(x):
        out = x[:HEAD_DIM]
        for h in range(1, HEADS_PER_SLAB):
            out = out + x[h * HEAD_DIM:(h + 1) * HEAD_DIM]
        return out

    chains = [(c, s) for c in range(n_chunks) for s in range(N_SLABS)]
    ld = lambda ref: [ref[c * C:(c + 1) * C, s * SLAB:(s + 1) * SLAB] for c, s in chains]
    at, bt, kt, rt, vv = ld(at_s), ld(bt_s), ld(kt_s), ld(rt_s), ld(v_s)
    a1 = [jnp.where(keep_a1,
                    _dot_nt(jnp.concatenate([x, y], axis=0),
                            jnp.concatenate([_block_diag(p), _block_diag(q)], axis=0)), 0.0)
          for x, y, p, q in zip(at, rt, bt, kt)]
    aab = [x[:C, :SLAB] for x in a1]
    kv = [_dot(x[:, SLAB:].astype(BF16), _block_diag(y)) for x, y in zip(a1, vv)]
    akv = [x[:C].astype(BF16) for x in kv]
    arkv = [x[C:] for x in kv]
    tinv = [eye_pack + x for x in aab]
    pwb = [x.astype(BF16) for x in aab]
    pwb = [_dot(x, _block_diag(x)).astype(BF16) for x in pwb]
    for step in range(5):
        if step < 4:
            res = [_dot(jnp.concatenate([p, t.astype(BF16)], axis=0), _block_diag(p))
                   for t, p in zip(tinv, pwb)]
            pwb = [x[:C].astype(BF16) for x in res]
            tinv = [t + x[C:] for t, x in zip(tinv, res)]
        else:
            tinv = [t + _dot(t.astype(BF16), _block_diag(p)) for t, p in zip(tinv, pwb)]
    wu = [_dot(t.astype(BF16), _block_diag(jnp.concatenate([x, y], axis=1))).astype(BF16)
          for t, x, y in zip(tinv, at, akv)]
    for i, (c, s) in enumerate(chains):
        rows = slice(c * C, (c + 1) * C)
        sl = slice(s * SLAB, (s + 1) * SLAB)
        lhs_t = jnp.concatenate([bh_s[rows, sl], kh_s[rows, sl]], axis=0)
        rhs = jnp.concatenate([wu[i], jnp.concatenate([zeros_c, vv[i]], axis=1)], axis=0)
        mn = _dot(lhs_t.T.astype(BF16), rhs)
        mn = jnp.where(same_head, mn, 0.0)
        m_ref[rows, sl] = fold(mn[:, :SLAB] + jnp.where(diag, gam_s[c:c + 1, sl], 0.0)).astype(BF16)
        n_ref[rows, sl] = fold(mn[:, SLAB:])
        ro = _dot(a1[i][C:, :SLAB].astype(BF16), _block_diag(wu[i]))
        r_ref[rows, sl] = (rf_s[rows, sl] + ro[:, :SLAB]).astype(BF16)
        o_ref[rows, sl] = (ro[:, SLAB:] + arkv[i]).astype(BF16)


def _rwkv_prep(p_shift, w0, w2, a0, a2, g2, k_k, k_a, r_k):
    n = p_shift.shape[0]
    W = RWKV_WIDTH
    tt = RWKV_ROWS
    lane_head = jnp.arange(LANES) // HEAD_DIM
    seg = (lane_head[:, None] == lane_head[None, :]).astype(BF16)
    w2p = jnp.concatenate([w2, jnp.zeros((AAA_LORA, W), w2.dtype)], axis=0).astype(BF16)
    a2p = jnp.concatenate([jnp.zeros((DECAY_LORA, W), a2.dtype), a2], axis=0).astype(BF16)
    row = lambda i: (i, 0)
    vec = lambda t: t.astype(F32).reshape(1, -1)
    f32_out = jax.ShapeDtypeStruct((n, W), F32)
    bf16_out = jax.ShapeDtypeStruct((n, W), BF16)
    return pl.pallas_call(
        functools.partial(_rwkv_prep_kernel, n_chunks=tt // CHUNK),
        grid=(n // tt,),
        in_specs=[pl.BlockSpec((tt, RWKV_COLS), row),
                  _resident((1, W)), _resident((LANES, W)), _resident((1, W)),
                  _resident((LANES, W)), _resident((GATE_LORA, W)),
                  _resident((1, W)), _resident((1, W)), _resident((1, W)),
                  _resident((LANES, LANES))],
        out_specs=[pl.BlockSpec((tt, W), row)] * 6,
        out_shape=[bf16_out, f32_out, bf16_out, bf16_out, bf16_out, bf16_out],
        scratch_shapes=[pltpu.VMEM((tt, W), BF16)] * 5 + [pltpu.VMEM((tt, W), F32)] * 3
                       + [pltpu.VMEM((8, W), F32)],
        compiler_params=pltpu.CompilerParams(
            dimension_semantics=("parallel",), vmem_limit_bytes=VMEM_LIMIT),
        name="rwkv_prep",
    )(p_shift, vec(w0), w2p, vec(a0), a2p, g2.astype(BF16), vec(k_k), vec(k_a), vec(r_k),
      seg)


def _rwkv_scan_kernel(m_ref, n_ref, r_ref, o_ref, bonus_ref, gate_ref, lnw_ref, lnb_ref, seg_ref,
                      out_ref, state, obuf, *, n_batch, n_chunks):
    @pl.when(pl.program_id(0) == 0)
    def _():
        state[...] = jnp.zeros_like(state)

    C = CHUNK
    for c in range(n_chunks):
        rows = slice(c * C, (c + 1) * C)
        for b in range(n_batch):
            for s in range(N_SLABS):
                sl = slice(s * SLAB, (s + 1) * SLAB)
                s0 = state[b * N_SLABS + s]
                lhs = jnp.concatenate([m_ref[b, rows, sl], r_ref[b, rows, sl]], axis=0)
                res = _dot(lhs, _block_diag(s0.astype(BF16)))
                state[b * N_SLABS + s] = res[:C] + n_ref[b, rows, sl]
                obuf[b, rows, sl] = res[C:] + o_ref[b, rows, sl]

    seg_mean = seg_ref[...]
    for b in range(n_batch):
        o = obuf[b]
        d = o - _seg_sum(o, seg_mean)
        var = _seg_sum(d * d, seg_mean)
        y = d * lax.rsqrt(var + GN_EPS) * lnw_ref[...] + lnb_ref[...]
        out_ref[b] = ((y + bonus_ref[b]) * gate_ref[b]).astype(BF16)


def _rwkv_scan(mats, n_batch, seq_len, ln_w, ln_b):
    W = RWKV_WIDTH
    tt = RWKV_ROWS
    lane_head = jnp.arange(LANES) // HEAD_DIM
    seg_mean = ((lane_head[:, None] == lane_head[None, :]).astype(F32) / HEAD_DIM).astype(BF16)
    mats = [t.reshape(n_batch, seq_len, W) for t in mats]
    blk = pl.BlockSpec((n_batch, tt, W), lambda i: (0, i, 0))
    out = pl.pallas_call(
        functools.partial(_rwkv_scan_kernel, n_batch=n_batch, n_chunks=tt // CHUNK),
        grid=(seq_len // tt,),
        in_specs=[blk] * 6 + [_resident((1, W)), _resident((1, W)), _resident((LANES, LANES))],
        out_specs=blk,
        out_shape=jax.ShapeDtypeStruct((n_batch, seq_len, W), BF16),
        scratch_shapes=[pltpu.VMEM((n_batch * N_SLABS, CHUNK, SLAB), F32),
                        pltpu.VMEM((n_batch, tt, W), F32)],
        compiler_params=pltpu.CompilerParams(
            dimension_semantics=("arbitrary",), vmem_limit_bytes=VMEM_LIMIT),
        name="rwkv_scan",
    )(*mats, ln_w.astype(F32).reshape(1, W), ln_b.astype(F32).reshape(1, W), seg_mean)
    return out.reshape(n_batch * seq_len, W)


def _attn_kernel(sink_ref, q_ref, kp_ref, kc_ref, vp_ref, vc_ref, out_ref, *, n_blocks):
    BQ = WINDOW
    first = pl.program_id(1) == 0
    kx = jnp.concatenate([kp_ref[...], kc_ref[...]], axis=0)
    vx = jnp.concatenate([vp_ref[...], vc_ref[...]], axis=0)
    lo = _lane_block(kx.shape, HEAD_DIM, LANES) == 0
    kxr = pltpu.roll(kx, HEAD_DIM, axis=1)
    vxr = pltpu.roll(vx, HEAD_DIM, axis=1)
    bf = lambda t: t.astype(BF16)
    k_lo = [bf(jnp.where(lo, kx, 0.0)), bf(jnp.where(lo, kxr, 0.0))]
    k_hi = [bf(jnp.where(lo, 0.0, kxr)), bf(jnp.where(lo, 0.0, kx))]
    v_lo = [bf(jnp.where(lo, vx, 0.0)), bf(jnp.where(lo, vxr, 0.0))]
    v_hi = [bf(jnp.where(lo, 0.0, vxr)), bf(jnp.where(lo, 0.0, vx))]

    qi = lax.broadcasted_iota(jnp.int32, (BQ, 2 * BQ), 0)
    kj = lax.broadcasted_iota(jnp.int32, (BQ, 2 * BQ), 1)
    dist = qi + BQ - kj
    band = (dist >= 0) & (dist < WINDOW)
    band_first = band & (kj >= jnp.where(first, BQ, 0))
    col0 = lax.broadcasted_iota(jnp.int32, (1, 2 * BQ), 1) == 0
    log2e = math.log2(math.e)
    fill = [jnp.where(col0, sink_ref[h] * log2e, -jnp.inf) for h in range(N_Q_HEADS)]
    vrow = lax.broadcasted_iota(jnp.int32, (4 * BQ, LANES), 0) & (2 * BQ - 1)
    ones_sel = jnp.where((lax.broadcasted_iota(jnp.int32, (4 * BQ, LANES), 0) < 2 * BQ)
                         == (_lane_block((4 * BQ, LANES), HEAD_DIM, LANES) == 0),
                         1.0, 0.0).astype(BF16)
    qscale = HEAD_DIM ** -0.5 * log2e
    n_pairs = N_Q_HEADS // 2

    kv_of = [(2 * j) // (N_Q_HEADS // N_KV_HEADS) for j in range(n_pairs)]
    units = [(qb, j) for qb in range(n_blocks) for j in range(n_pairs)]
    kcat, vcat = {}, {}
    for qb in range(n_blocks):
        krows = slice(qb * BQ, (qb + 2) * BQ)
        for g in range(N_KV_HEADS):
            kcat[qb, g] = jnp.concatenate([k_lo[g][krows], k_hi[g][krows]], axis=0)
            vcat[qb, g] = jnp.concatenate(
                [jnp.where(vrow == 0, 0.0,
                           jnp.concatenate([v_lo[g][krows], v_hi[g][krows]], axis=0)),
                 ones_sel], axis=1)
    s = [_dot_nt((q_ref[qb * BQ:(qb + 1) * BQ, j * LANES:(j + 1) * LANES] * qscale).astype(BF16),
                 kcat[qb, kv_of[j]]) for qb, j in units]
    sh = [[jnp.where(band_first if qb == 0 else band,
                     x[:, hh * 2 * BQ:(hh + 1) * 2 * BQ], fill[2 * j + hh])
           for hh in range(2)] for x, (qb, j) in zip(s, units)]
    mx = [[jnp.max(x, axis=-1, keepdims=True) for x in row] for row in sh]
    e = [jnp.concatenate([jnp.exp2(x - m).astype(BF16) for x, m in zip(xr, mr)], axis=1)
         for xr, mr in zip(sh, mx)]
    pv = [_dot(x, vcat[qb, kv_of[j]]) for x, (qb, j) in zip(e, units)]
    for x, (qb, j) in zip(pv, units):
        out_ref[qb * BQ:(qb + 1) * BQ, j * LANES:(j + 1) * LANES] = (
            x[:, :LANES] / x[:, LANES:]).astype(BF16)


def _attn(p_attn, n_batch, seq_len, sinks):
    n = p_attn.shape[0]
    rows = ATTN_ROWS
    n_blocks = rows // WINDOW
    steps = seq_len // rows
    kcol = ATTN_WIDTH // LANES
    vcol = kcol + KV_WIDTH // LANES
    cur = lambda b, i: b * steps + i
    prev = lambda b, i: (b * steps + i) * n_blocks - jnp.minimum(i, 1)
    return pl.pallas_call(
        functools.partial(_attn_kernel, n_blocks=n_blocks),
        grid=(n_batch, steps),
        in_specs=[pl.BlockSpec(memory_space=pltpu.SMEM),
                  pl.BlockSpec((rows, ATTN_WIDTH), lambda b, i: (cur(b, i), 0)),
                  pl.BlockSpec((WINDOW, KV_WIDTH), lambda b, i: (prev(b, i), kcol)),
                  pl.BlockSpec((rows, KV_WIDTH), lambda b, i: (cur(b, i), kcol)),
                  pl.BlockSpec((WINDOW, KV_WIDTH), lambda b, i: (prev(b, i), vcol)),
                  pl.BlockSpec((rows, KV_WIDTH), lambda b, i: (cur(b, i), vcol))],
        out_specs=pl.BlockSpec((rows, ATTN_WIDTH), lambda b, i: (cur(b, i), 0)),
        out_shape=jax.ShapeDtypeStruct((n, ATTN_WIDTH), BF16),
        compiler_params=pltpu.CompilerParams(
            dimension_semantics=("parallel", "arbitrary"), vmem_limit_bytes=VMEM_LIMIT),
        name="swa_attn",
    )(sinks.astype(F32), p_attn, p_attn, p_attn, p_attn, p_attn)


def kernel(x, norm_ffn1, ffn1_gate, ffn1_up, ffn1_down, norm_mix, w_in, b_in_attn, rwkv_shift_mix, rwkv_w0, rwkv_w2, rwkv_a0, rwkv_a2, rwkv_g2, rwkv_k_k, rwkv_k_a, rwkv_r_k, rwkv_ln_w, rwkv_ln_b, attn_sinks, w_out, norm_ffn2, ffn2_gate, ffn2_up, ffn2_down, norm_final):
    n_batch, seq_len, d = x.shape
    depth = w_in.shape[0]
    h = x.reshape(n_batch * seq_len, d)
    for l in range(depth):
        h = _ffn(h, norm_ffn1[l], ffn1_gate[l], ffn1_up[l], ffn1_down[l])
        p_shift, p_attn = _in_proj(h, seq_len, norm_mix[l], w_in[l], b_in_attn[l],
                                   rwkv_shift_mix[l])
        mats = _rwkv_prep(p_shift, rwkv_w0[l], rwkv_w2[l], rwkv_a0[l], rwkv_a2[l], rwkv_g2[l],
                          rwkv_k_k[l], rwkv_k_a[l], rwkv_r_k[l])
        o_rwkv = _rwkv_scan(mats, n_batch, seq_len, rwkv_ln_w[l], rwkv_ln_b[l])
        o_attn = _attn(p_attn, n_batch, seq_len, attn_sinks[l])
        h = _ffn(h, norm_ffn2[l], ffn2_gate[l], ffn2_up[l], ffn2_down[l],
                 mix=(o_rwkv, o_attn, w_out[l]),
                 final_norm=norm_final if l == depth - 1 else None)
    return h.reshape(n_batch, seq_len, d)
```

```python
import functools
import math

import jax
import jax.numpy as jnp
from jax import lax
from jax.experimental import pallas as pl
from jax.experimental.pallas import tpu as pltpu

F32 = jnp.float32
BF16 = jnp.bfloat16

HEAD_DIM = 64
LANES = 128
SLAB = 128
HEADS_PER_SLAB = SLAB // HEAD_DIM
CHUNK = 64
N_RWKV_HEADS = 8
RWKV_WIDTH = N_RWKV_HEADS * HEAD_DIM
N_SLABS = RWKV_WIDTH // SLAB
DECAY_LORA = 64
AAA_LORA = 64
GATE_LORA = 128
RWKV_COLS = 3 * RWKV_WIDTH + DECAY_LORA + AAA_LORA + GATE_LORA
N_Q_HEADS = 8
N_KV_HEADS = 2
ATTN_WIDTH = N_Q_HEADS * HEAD_DIM
KV_WIDTH = N_KV_HEADS * HEAD_DIM
ATTN_COLS = ATTN_WIDTH + 2 * KV_WIDTH
WINDOW = 128
GN_EPS = 64e-5
NORM_EPS = 1e-5
VMEM_LIMIT = 56 * 1024 * 1024

FFN_ROWS = 512
FFN_COLS = 256
INPROJ_ROWS = 1024
RWKV_ROWS = 512
ATTN_ROWS = 512


def _dot(a, b):
    return jnp.dot(a, b, preferred_element_type=F32)


def _dot_nt(a, b):
    return lax.dot_general(a, b, (((1,), (1,)), ((), ())), preferred_element_type=F32)


def _rms(x, g):
    ms = jnp.mean(x * x, axis=-1, keepdims=True)
    return x * lax.rsqrt(ms + NORM_EPS) * g


def _lane_block(shape, width, span=SLAB):
    lane = lax.broadcasted_iota(jnp.int32, shape, 1)
    return (lane & (span - 1)) >> (width.bit_length() - 1)


def _block_diag(x, width=HEAD_DIM):
    blk = _lane_block(x.shape, width)
    return jnp.concatenate([jnp.where(blk == g, x, 0.0) for g in range(SLAB // width)], axis=0)


def _seg_sum(x, seg):
    xb = x.astype(BF16)
    rows, groups = x.shape[0], x.shape[1] // LANES
    y = _dot(jnp.concatenate([xb[:, j * LANES:(j + 1) * LANES] for j in range(groups)], axis=0), seg)
    return jnp.concatenate([y[j * rows:(j + 1) * rows] for j in range(groups)], axis=1)


def _resident(shape):
    return pl.BlockSpec(shape, lambda *_: (0,) * len(shape), pipeline_mode=pl.Buffered(1))


def _ffn_kernel(*refs, has_mix, final_norm, nf, tf):
    refs = list(refs)
    x_ref = refs.pop(0)
    if has_mix:
        orw_ref, oat_ref, wor_ref, woa_ref = refs[:4]
        refs = refs[4:]
    g_ref, wg_ref, wu_ref, wd_ref = refs[:4]
    refs = refs[4:]
    if final_norm:
        gf_ref = refs.pop(0)
    out_ref, h_s, acc = refs

    x = x_ref[...]
    if has_mix:
        x = x + _dot(orw_ref[...], wor_ref[...]) + _dot(oat_ref[...], woa_ref[...])
        out_ref[...] = x
    h_s[...] = _rms(x, g_ref[...]).astype(BF16)

    def down(f):
        cols = pl.ds(f * tf if isinstance(f, int) else pl.multiple_of(f * tf, tf), tf)
        h = h_s[...]
        gate = _dot(h, wg_ref[:, cols].astype(BF16))
        up = _dot(h, wu_ref[:, cols].astype(BF16))
        act = (gate * jax.nn.sigmoid(gate) * up).astype(BF16)
        return _dot(act, wd_ref[cols, :].astype(BF16))

    acc[...] = down(0)

    for f in range(1, nf):
        acc[...] += down(f)
    res = out_ref[...] if has_mix else x_ref[...]
    y = res + 0.5 * acc[...]
    if final_norm:
        y = _rms(y, gf_ref[...])
    out_ref[...] = y


def _ffn(x, norm, w_gate, w_up, w_down, mix=None, final_norm=None):
    n, d = x.shape
    f = w_gate.shape[1]
    tm, tf = FFN_ROWS, FFN_COLS
    nf = f // tf
    row = lambda i: (i, 0)
    wg, wu, wd = w_gate, w_up, w_down
    args = [x]
    specs = [pl.BlockSpec((tm, d), row)]
    if mix is not None:
        o_rwkv, o_attn, w_out = mix
        wo = w_out.astype(BF16)
        args += [o_rwkv, o_attn, wo[:RWKV_WIDTH], wo[RWKV_WIDTH:]]
        specs += [pl.BlockSpec((tm, RWKV_WIDTH), row), pl.BlockSpec((tm, ATTN_WIDTH), row),
                  _resident((RWKV_WIDTH, d)), _resident((ATTN_WIDTH, d))]
    args += [norm.reshape(1, d), wg, wu, wd]
    specs += [_resident((1, d)), _resident((d, f)), _resident((d, f)), _resident((f, d))]
    if final_norm is not None:
        args.append(final_norm.reshape(1, d))
        specs.append(_resident((1, d)))
    return pl.pallas_call(
        functools.partial(_ffn_kernel, has_mix=mix is not None,
                          final_norm=final_norm is not None, nf=nf, tf=tf),
        grid=(n // tm,),
        in_specs=specs,
        out_specs=pl.BlockSpec((tm, d), row),
        out_shape=jax.ShapeDtypeStruct((n, d), F32),
        scratch_shapes=[pltpu.VMEM((tm, d), BF16), pltpu.VMEM((tm, d), F32)],
        compiler_params=pltpu.CompilerParams(
            dimension_semantics=("parallel",), vmem_limit_bytes=VMEM_LIMIT),
        name="ffn_mix" if mix is not None else "ffn",
    )(*args)


def _inproj_kernel(x_ref, g_ref, w_ref, b_ref, mix_ref, prw_ref, pat_ref, last_row,
                   *, tiles_per_seq):
    h = _rms(x_ref[...], g_ref[...]).astype(BF16)
    p = _dot(h, w_ref[...])
    pat_ref[...] = p[:, RWKV_COLS:] + b_ref[...]
    pr = p[:, :RWKV_COLS]
    prev_row = jnp.where(pl.program_id(0) % tiles_per_seq == 0, 0.0, last_row[0:1, :])
    row = lax.broadcasted_iota(jnp.int32, pr.shape, 0)
    p_prev = jnp.where(row == 0, prev_row, pltpu.roll(pr, 1, axis=0))
    prw_ref[...] = pr + (p_prev - pr) * mix_ref[...]
    last_row[0:1, :] = pr[pr.shape[0] - 1:, :]


def _in_proj(x, seq_len, norm, w_in, b_attn, shift_mix):
    n, d = x.shape
    cols = w_in.shape[1]
    tm = INPROJ_ROWS
    bias = b_attn.astype(F32).reshape(1, ATTN_COLS)
    row = lambda i: (i, 0)
    return pl.pallas_call(
        functools.partial(_inproj_kernel, tiles_per_seq=seq_len // tm),
        grid=(n // tm,),
        in_specs=[pl.BlockSpec((tm, d), row), _resident((1, d)), _resident((d, cols)),
                  _resident((1, ATTN_COLS)), _resident((1, RWKV_COLS))],
        out_specs=[pl.BlockSpec((tm, RWKV_COLS), row), pl.BlockSpec((tm, ATTN_COLS), row)],
        out_shape=[jax.ShapeDtypeStruct((n, RWKV_COLS), F32),
                   jax.ShapeDtypeStruct((n, ATTN_COLS), F32)],
        scratch_shapes=[pltpu.VMEM((8, RWKV_COLS), F32)],
        compiler_params=pltpu.CompilerParams(
            dimension_semantics=("arbitrary",), vmem_limit_bytes=VMEM_LIMIT),
        name="in_proj",
    )(x, norm.reshape(1, d), w_in.astype(BF16), bias, shift_mix.astype(F32).reshape(1, -1))


def _rwkv_prep_kernel(ps_ref, w0_ref, w2_ref, a0_ref, a2_ref, g2_ref, kk_ref, ka_ref, rk_ref,
                      seg_ref,
                      m_ref, n_ref, r_ref, o_ref, bonus_ref, gate_ref,
                      at_s, bt_s, kt_s, rt_s, v_s, bh_s, kh_s, rf_s, gam_s, *, n_chunks):
    C = CHUNK
    W = RWKV_WIDTH
    ps = ps_ref[...]
    r = ps[:, 0:W]
    k = ps[:, W:2 * W]
    v = ps[:, 2 * W:3 * W]
    wa = ps[:, 3 * W:3 * W + LANES]
    gl = ps[:, 3 * W + LANES:3 * W + 2 * LANES]
    z = w0_ref[...] + _dot(jnp.tanh(wa).astype(BF16), w2_ref[...])
    lw = -math.exp(-0.5) * jax.nn.sigmoid(z)
    a = jax.nn.sigmoid(a0_ref[...] + _dot(wa.astype(BF16), a2_ref[...]))
    gate_ref[...] = _dot(jax.nn.sigmoid(gl).astype(BF16), g2_ref[...]).astype(BF16)
    seg = seg_ref[...]
    kk = k * kk_ref[...]
    kk = kk * jnp.minimum(lax.rsqrt(_seg_sum(kk * kk, seg)), 1e12)
    k = k * (1.0 + (a - 1.0) * ka_ref[...])
    b = kk * a
    bonus_ref[...] = (_seg_sum(r * k * rk_ref[...], seg) * v).astype(BF16)
    v_s[...] = v.astype(BF16)
    cs = lw
    row_in_chunk = lax.broadcasted_iota(jnp.int32, lw.shape, 0) & (C - 1)
    for shift in (1, 2, 4, 8, 16, 32):
        cs = cs + jnp.where(row_in_chunk >= shift, pltpu.roll(cs, shift, axis=0), 0.0)
    for c in range(n_chunks):
        rows = slice(c * C, (c + 1) * C)
        cs_c = cs[rows]
        gam = jnp.exp(cs[(c + 1) * C - 1:(c + 1) * C, :])
        e_neg = jnp.exp(-cs_c)
        rt = r[rows] * jnp.exp(cs_c)
        bt = b[rows] * e_neg
        kt = k[rows] * e_neg
        at_s[rows, :] = (-kk[rows] * jnp.exp(cs_c - lw[rows])).astype(BF16)
        bt_s[rows, :] = bt.astype(BF16)
        kt_s[rows, :] = kt.astype(BF16)
        rt_s[rows, :] = rt.astype(BF16)
        rf_s[rows, :] = rt
        bh_s[rows, :] = bt * gam
        kh_s[rows, :] = kt * gam
        gam_s[c:c + 1, :] = gam

    rowi = lax.broadcasted_iota(jnp.int32, (2 * C, 2 * SLAB), 0)
    lane_s = lax.broadcasted_iota(jnp.int32, (2 * C, 2 * SLAB), 1) & (C - 1)
    keep_a1 = lane_s < (rowi & (C - 1)) + jnp.where(rowi < C, 0, 1)
    eye_pack = jnp.where((lax.broadcasted_iota(jnp.int32, (C, SLAB), 1) & (C - 1))
                         == lax.broadcasted_iota(jnp.int32, (C, SLAB), 0), 1.0, 0.0).astype(F32)
    same_head = ((lax.broadcasted_iota(jnp.int32, (SLAB, 2 * SLAB), 0) >> 6)
                 == _lane_block((SLAB, 2 * SLAB), HEAD_DIM))
    diag = (lax.broadcasted_iota(jnp.int32, (SLAB, SLAB), 0)
            == lax.broadcasted_iota(jnp.int32, (SLAB, SLAB), 1))
    zeros_c = jnp.zeros((C, SLAB), BF16)

    def fold(x):
        out = x[:HEAD_DIM]
        for h in range(1, HEADS_PER_SLAB):
            out = out + x[h * HEAD_DIM:(h + 1) * HEAD_DIM]
        return out

    chains = [(c, s) for c in range(n_chunks) for s in range(N_SLABS)]
    ld = lambda ref: [ref[c * C:(c + 1) * C, s * SLAB:(s + 1) * SLAB] for c, s in chains]
    at, bt, kt, rt, vv = ld(at_s), ld(bt_s), ld(kt_s), ld(rt_s), ld(v_s)
    a1 = [jnp.where(keep_a1,
                    _dot_nt(jnp.concatenate([x, y], axis=0),
                            jnp.concatenate([_block_diag(p), _block_diag(q)], axis=0)), 0.0)
          for x, y, p, q in zip(at, rt, bt, kt)]
    aab = [x[:C, :SLAB] for x in a1]
    kv = [_dot(x[:, SLAB:].astype(BF16), _block_diag(y)) for x, y in zip(a1, vv)]
    akv = [x[:C].astype(BF16) for x in kv]
    arkv = [x[C:] for x in kv]
    tinv = [eye_pack + x for x in aab]
    pwb = [x.astype(BF16) for x in aab]
    pwb = [_dot(x, _block_diag(x)).astype(BF16) for x in pwb]
    for step in range(5):
        if step < 4:
            res = [_dot(jnp.concatenate([p, t.astype(BF16)], axis=0), _block_diag(p))
                   for t, p in zip(tinv, pwb)]
            pwb = [x[:C].astype(BF16) for x in res]
            tinv = [t + x[C:] for t, x in zip(tinv, res)]
        else:
            tinv = [t + _dot(t.astype(BF16), _block_diag(p)) for t, p in zip(tinv, pwb)]
    wu = [_dot(t.astype(BF16), _block_diag(jnp.concatenate([x, y], axis=1))).astype(BF16)
          for t, x, y in zip(tinv, at, akv)]
    for i, (c, s) in enumerate(chains):
        rows = slice(c * C, (c + 1) * C)
        sl = slice(s * SLAB, (s + 1) * SLAB)
        lhs_t = jnp.concatenate([bh_s[rows, sl], kh_s[rows, sl]], axis=0)
        rhs = jnp.concatenate([wu[i], jnp.concatenate([zeros_c, vv[i]], axis=1)], axis=0)
        mn = _dot(lhs_t.T.astype(BF16), rhs)
        mn = jnp.where(same_head, mn, 0.0)
        m_ref[rows, sl] = fold(mn[:, :SLAB] + jnp.where(diag, gam_s[c:c + 1, sl], 0.0)).astype(BF16)
        n_ref[rows, sl] = fold(mn[:, SLAB:])
        ro = _dot(a1[i][C:, :SLAB].astype(BF16), _block_diag(wu[i]))
        r_ref[rows, sl] = (rf_s[rows, sl] + ro[:, :SLAB]).astype(BF16)
        o_ref[rows, sl] = (ro[:, SLAB:] + arkv[i]).astype(BF16)


def _rwkv_prep(p_shift, w0, w2, a0, a2, g2, k_k, k_a, r_k):
    n = p_shift.shape[0]
    W = RWKV_WIDTH
    tt = RWKV_ROWS
    lane_head = jnp.arange(LANES) // HEAD_DIM
    seg = (lane_head[:, None] == lane_head[None, :]).astype(BF16)
    w2p = jnp.concatenate([w2, jnp.zeros((AAA_LORA, W), w2.dtype)], axis=0).astype(BF16)
    a2p = jnp.concatenate([jnp.zeros((DECAY_LORA, W), a2.dtype), a2], axis=0).astype(BF16)
    row = lambda i: (i, 0)
    vec = lambda t: t.astype(F32).reshape(1, -1)
    f32_out = jax.ShapeDtypeStruct((n, W), F32)
    bf16_out = jax.ShapeDtypeStruct((n, W), BF16)
    return pl.pallas_call(
        functools.partial(_rwkv_prep_kernel, n_chunks=tt // CHUNK),
        grid=(n // tt,),
        in_specs=[pl.BlockSpec((tt, RWKV_COLS), row),
                  _resident((1, W)), _resident((LANES, W)), _resident((1, W)),
                  _resident((LANES, W)), _resident((GATE_LORA, W)),
                  _resident((1, W)), _resident((1, W)), _resident((1, W)),
                  _resident((LANES, LANES))],
        out_specs=[pl.BlockSpec((tt, W), row)] * 6,
        out_shape=[bf16_out, f32_out, bf16_out, bf16_out, bf16_out, bf16_out],
        scratch_shapes=[pltpu.VMEM((tt, W), BF16)] * 5 + [pltpu.VMEM((tt, W), F32)] * 3
                       + [pltpu.VMEM((8, W), F32)],
        compiler_params=pltpu.CompilerParams(
            dimension_semantics=("parallel",), vmem_limit_bytes=VMEM_LIMIT),
        name="rwkv_prep",
    )(p_shift, vec(w0), w2p, vec(a0), a2p, g2.astype(BF16), vec(k_k), vec(k_a), vec(r_k),
      seg)


def _rwkv_scan_kernel(m_ref, n_ref, r_ref, o_ref, bonus_ref, gate_ref, lnw_ref, lnb_ref, seg_ref,
                      out_ref, state, obuf, *, n_batch, n_chunks):
    @pl.when(pl.program_id(0) == 0)
    def _():
        state[...] = jnp.zeros_like(state)

    C = CHUNK
    for c in range(n_chunks):
        rows = slice(c * C, (c + 1) * C)
        for b in range(n_batch):
            for s in range(N_SLABS):
                sl = slice(s * SLAB, (s + 1) * SLAB)
                s0 = state[b * N_SLABS + s]
                lhs = jnp.concatenate([m_ref[b, rows, sl], r_ref[b, rows, sl]], axis=0)
                res = _dot(lhs, _block_diag(s0.astype(BF16)))
                state[b * N_SLABS + s] = res[:C] + n_ref[b, rows, sl]
                obuf[b, rows, sl] = res[C:] + o_ref[b, rows, sl]

    seg_mean = seg_ref[...]
    for b in range(n_batch):
        o = obuf[b]
        d = o - _seg_sum(o, seg_mean)
        var = _seg_sum(d * d, seg_mean)
        y = d * lax.rsqrt(var + GN_EPS) * lnw_ref[...] + lnb_ref[...]
        out_ref[b] = ((y + bonus_ref[b]) * gate_ref[b]).astype(BF16)


def _rwkv_scan(mats, n_batch, seq_len, ln_w, ln_b):
    W = RWKV_WIDTH
    tt = RWKV_ROWS
    lane_head = jnp.arange(LANES) // HEAD_DIM
    seg_mean = ((lane_head[:, None] == lane_head[None, :]).astype(F32) / HEAD_DIM).astype(BF16)
    mats = [t.reshape(n_batch, seq_len, W) for t in mats]
    blk = pl.BlockSpec((n_batch, tt, W), lambda i: (0, i, 0))
    out = pl.pallas_call(
        functools.partial(_rwkv_scan_kernel, n_batch=n_batch, n_chunks=tt // CHUNK),
        grid=(seq_len // tt,),
        in_specs=[blk] * 6 + [_resident((1, W)), _resident((1, W)), _resident((LANES, LANES))],
        out_specs=blk,
        out_shape=jax.ShapeDtypeStruct((n_batch, seq_len, W), BF16),
        scratch_shapes=[pltpu.VMEM((n_batch * N_SLABS, CHUNK, SLAB), F32),
                        pltpu.VMEM((n_batch, tt, W), F32)],
        compiler_params=pltpu.CompilerParams(
            dimension_semantics=("arbitrary",), vmem_limit_bytes=VMEM_LIMIT),
        name="rwkv_scan",
    )(*mats, ln_w.astype(F32).reshape(1, W), ln_b.astype(F32).reshape(1, W), seg_mean)
    return out.reshape(n_batch * seq_len, W)


def _attn_kernel(sink_ref, q_ref, kp_ref, kc_ref, vp_ref, vc_ref, out_ref, *, n_blocks):
    BQ = WINDOW
    first = pl.program_id(1) == 0
    kx = jnp.concatenate([kp_ref[...], kc_ref[...]], axis=0)
    vx = jnp.concatenate([vp_ref[...], vc_ref[...]], axis=0)
    lo = _lane_block(kx.shape, HEAD_DIM, LANES) == 0
    kxr = pltpu.roll(kx, HEAD_DIM, axis=1)
    vxr = pltpu.roll(vx, HEAD_DIM, axis=1)
    bf = lambda t: t.astype(BF16)
    k_lo = [bf(jnp.where(lo, kx, 0.0)), bf(jnp.where(lo, kxr, 0.0))]
    k_hi = [bf(jnp.where(lo, 0.0, kxr)), bf(jnp.where(lo, 0.0, kx))]
    v_lo = [bf(jnp.where(lo, vx, 0.0)), bf(jnp.where(lo, vxr, 0.0))]
    v_hi = [bf(jnp.where(lo, 0.0, vxr)), bf(jnp.where(lo, 0.0, vx))]

    qi = lax.broadcasted_iota(jnp.int32, (BQ, 2 * BQ), 0)
    kj = lax.broadcasted_iota(jnp.int32, (BQ, 2 * BQ), 1)
    dist = qi + BQ - kj
    band = (dist >= 0) & (dist < WINDOW)
    band_first = band & (kj >= jnp.where(first, BQ, 0))
    col0 = lax.broadcasted_iota(jnp.int32, (1, 2 * BQ), 1) == 0
    log2e = math.log2(math.e)
    fill = [jnp.where(col0, sink_ref[h] * log2e, -jnp.inf) for h in range(N_Q_HEADS)]
    vrow = lax.broadcasted_iota(jnp.int32, (4 * BQ, LANES), 0) & (2 * BQ - 1)
    ones_sel = jnp.where((lax.broadcasted_iota(jnp.int32, (4 * BQ, LANES), 0) < 2 * BQ)
                         == (_lane_block((4 * BQ, LANES), HEAD_DIM, LANES) == 0),
                         1.0, 0.0).astype(BF16)
    qscale = HEAD_DIM ** -0.5 * log2e
    n_pairs = N_Q_HEADS // 2

    kv_of = [(2 * j) // (N_Q_HEADS // N_KV_HEADS) for j in range(n_pairs)]
    units = [(qb, j) for qb in range(n_blocks) for j in range(n_pairs)]
    kcat, vcat = {}, {}
    for qb in range(n_blocks):
        krows = slice(qb * BQ, (qb + 2) * BQ)
        for g in range(N_KV_HEADS):
            kcat[qb, g] = jnp.concatenate([k_lo[g][krows], k_hi[g][krows]], axis=0)
            vcat[qb, g] = jnp.concatenate(
                [jnp.where(vrow == 0, 0.0,
                           jnp.concatenate([v_lo[g][krows], v_hi[g][krows]], axis=0)),
                 ones_sel], axis=1)
    s = [_dot_nt((q_ref[qb * BQ:(qb + 1) * BQ, j * LANES:(j + 1) * LANES] * qscale).astype(BF16),
                 kcat[qb, kv_of[j]]) for qb, j in units]
    sh = [[jnp.where(band_first if qb == 0 else band,
                     x[:, hh * 2 * BQ:(hh + 1) * 2 * BQ], fill[2 * j + hh])
           for hh in range(2)] for x, (qb, j) in zip(s, units)]
    mx = [[jnp.max(x, axis=-1, keepdims=True) for x in row] for row in sh]
    e = [jnp.concatenate([jnp.exp2(x - m).astype(BF16) for x, m in zip(xr, mr)], axis=1)
         for xr, mr in zip(sh, mx)]
    pv = [_dot(x, vcat[qb, kv_of[j]]) for x, (qb, j) in zip(e, units)]
    for x, (qb, j) in zip(pv, units):
        out_ref[qb * BQ:(qb + 1) * BQ, j * LANES:(j + 1) * LANES] = (
            x[:, :LANES] / x[:, LANES:]).astype(BF16)


def _attn(p_attn, n_batch, seq_len, sinks):
    n = p_attn.shape[0]
    rows = ATTN_ROWS
    n_blocks = rows // WINDOW
    steps = seq_len // rows
    kcol = ATTN_WIDTH // LANES
    vcol = kcol + KV_WIDTH // LANES
    cur = lambda b, i: b * steps + i
    prev = lambda b, i: (b * steps + i) * n_blocks - jnp.minimum(i, 1)
    return pl.pallas_call(
        functools.partial(_attn_kernel, n_blocks=n_blocks),
        grid=(n_batch, steps),
        in_specs=[pl.BlockSpec(memory_space=pltpu.SMEM),
                  pl.BlockSpec((rows, ATTN_WIDTH), lambda b, i: (cur(b, i), 0)),
                  pl.BlockSpec((WINDOW, KV_WIDTH), lambda b, i: (prev(b, i), kcol)),
                  pl.BlockSpec((rows, KV_WIDTH), lambda b, i: (cur(b, i), kcol)),
                  pl.BlockSpec((WINDOW, KV_WIDTH), lambda b, i: (prev(b, i), vcol)),
                  pl.BlockSpec((rows, KV_WIDTH), lambda b, i: (cur(b, i), vcol))],
        out_specs=pl.BlockSpec((rows, ATTN_WIDTH), lambda b, i: (cur(b, i), 0)),
        out_shape=jax.ShapeDtypeStruct((n, ATTN_WIDTH), BF16),
        compiler_params=pltpu.CompilerParams(
            dimension_semantics=("parallel", "arbitrary"), vmem_limit_bytes=VMEM_LIMIT),
        name="swa_attn",
    )(sinks.astype(F32), p_attn, p_attn, p_attn, p_attn, p_attn)


def kernel(x, norm_ffn1, ffn1_gate, ffn1_up, ffn1_down, norm_mix, w_in, b_in_attn, rwkv_shift_mix, rwkv_w0, rwkv_w2, rwkv_a0, rwkv_a2, rwkv_g2, rwkv_k_k, rwkv_k_a, rwkv_r_k, rwkv_ln_w, rwkv_ln_b, attn_sinks, w_out, norm_ffn2, ffn2_gate, ffn2_up, ffn2_down, norm_final):
    n_batch, seq_len, d = x.shape
    depth = w_in.shape[0]
    h = x.reshape(n_batch * seq_len, d)
    for l in range(depth):
        h = _ffn(h, norm_ffn1[l], ffn1_gate[l], ffn1_up[l], ffn1_down[l])
        p_shift, p_attn = _in_proj(h, seq_len, norm_mix[l], w_in[l], b_in_attn[l],
                                   rwkv_shift_mix[l])
        mats = _rwkv_prep(p_shift, rwkv_w0[l], rwkv_w2[l], rwkv_a0[l], rwkv_a2[l], rwkv_g2[l],
                          rwkv_k_k[l], rwkv_k_a[l], rwkv_r_k[l])
        o_rwkv = _rwkv_scan(mats, n_batch, seq_len, rwkv_ln_w[l], rwkv_ln_b[l])
        o_attn = _attn(p_attn, n_batch, seq_len, attn_sinks[l])
        h = _ffn(h, norm_ffn2[l], ffn2_gate[l], ffn2_up[l], ffn2_down[l],
                 mix=(o_rwkv, o_attn, w_out[l]),
                 final_norm=norm_final if l == depth - 1 else None)
    return h.reshape(n_batch, seq_len, d)
```

```python
import functools
import math

import jax
import jax.numpy as jnp
from jax import lax
from jax.experimental import pallas as pl
from jax.experimental.pallas import tpu as pltpu

F32 = jnp.float32
BF16 = jnp.bfloat16

HEAD_DIM = 64
LANES = 128
SLAB = 128
HEADS_PER_SLAB = SLAB // HEAD_DIM
CHUNK = 64
N_RWKV_HEADS = 8
RWKV_WIDTH = N_RWKV_HEADS * HEAD_DIM
N_SLABS = RWKV_WIDTH // SLAB
DECAY_LORA = 64
AAA_LORA = 64
GATE_LORA = 128
RWKV_COLS = 3 * RWKV_WIDTH + DECAY_LORA + AAA_LORA + GATE_LORA
N_Q_HEADS = 8
N_KV_HEADS = 2
ATTN_WIDTH = N_Q_HEADS * HEAD_DIM
KV_WIDTH = N_KV_HEADS * HEAD_DIM
ATTN_COLS = ATTN_WIDTH + 2 * KV_WIDTH
WINDOW = 128
GN_EPS = 64e-5
NORM_EPS = 1e-5
VMEM_LIMIT = 56 * 1024 * 1024

FFN_ROWS = 512
FFN_COLS = 256
INPROJ_ROWS = 1024
RWKV_ROWS = 512
ATTN_ROWS = 512


def _dot(a, b):
    return jnp.dot(a, b, preferred_element_type=F32)


def _dot_nt(a, b):
    return lax.dot_general(a, b, (((1,), (1,)), ((), ())), preferred_element_type=F32)


def _rms(x, g):
    ms = jnp.mean(x * x, axis=-1, keepdims=True)
    return x * lax.rsqrt(ms + NORM_EPS) * g


def _lane_block(shape, width, span=SLAB):
    lane = lax.broadcasted_iota(jnp.int32, shape, 1)
    return (lane & (span - 1)) >> (width.bit_length() - 1)


def _block_diag(x, width=HEAD_DIM):
    blk = _lane_block(x.shape, width)
    return jnp.concatenate([jnp.where(blk == g, x, 0.0) for g in range(SLAB // width)], axis=0)


def _seg_sum(x, seg):
    xb = x.astype(BF16)
    rows, groups = x.shape[0], x.shape[1] // LANES
    y = _dot(jnp.concatenate([xb[:, j * LANES:(j + 1) * LANES] for j in range(groups)], axis=0), seg)
    return jnp.concatenate([y[j * rows:(j + 1) * rows] for j in range(groups)], axis=1)


def _resident(shape):
    return pl.BlockSpec(shape, lambda *_: (0,) * len(shape), pipeline_mode=pl.Buffered(1))


def _ffn_kernel(*refs, has_mix, final_norm, nf, tf):
    refs = list(refs)
    x_ref = refs.pop(0)
    if has_mix:
        orw_ref, oat_ref, wo_ref = refs[:3]
        refs = refs[3:]
    g_ref, wg_ref, wu_ref, wd_ref = refs[:4]
    refs = refs[4:]
    if final_norm:
        gf_ref = refs.pop(0)
    out_ref, h_s, acc = refs

    x = x_ref[...]
    if has_mix:
        x = (x + _dot(orw_ref[...], wo_ref[:RWKV_WIDTH].astype(BF16))
             + _dot(oat_ref[...], wo_ref[RWKV_WIDTH:].astype(BF16)))
        out_ref[...] = x
    h_s[...] = _rms(x, g_ref[...]).astype(BF16)

    def down(f):
        cols = pl.ds(f * tf if isinstance(f, int) else pl.multiple_of(f * tf, tf), tf)
        h = h_s[...]
        gate = _dot(h, wg_ref[:, cols].astype(BF16))
        up = _dot(h, wu_ref[:, cols].astype(BF16))
        act = (gate * jax.nn.sigmoid(gate) * up).astype(BF16)
        return _dot(act, wd_ref[cols, :].astype(BF16))

    acc[...] = down(0)

    for f in range(1, nf):
        acc[...] += down(f)
    res = out_ref[...] if has_mix else x_ref[...]
    y = res + 0.5 * acc[...]
    if final_norm:
        y = _rms(y, gf_ref[...])
    out_ref[...] = y


def _ffn(x, norm, w_gate, w_up, w_down, mix=None, final_norm=None):
    n, d = x.shape
    f = w_gate.shape[1]
    tm, tf = FFN_ROWS, FFN_COLS
    nf = f // tf
    row = lambda i: (i, 0)
    wg, wu, wd = w_gate, w_up, w_down
    args = [x]
    specs = [pl.BlockSpec((tm, d), row)]
    if mix is not None:
        o_rwkv, o_attn, w_out = mix
        args += [o_rwkv, o_attn, w_out]
        specs += [pl.BlockSpec((tm, RWKV_WIDTH), row), pl.BlockSpec((tm, ATTN_WIDTH), row),
                  _resident((RWKV_WIDTH + ATTN_WIDTH, d))]
    args += [norm.reshape(1, d), wg, wu, wd]
    specs += [_resident((1, d)), _resident((d, f)), _resident((d, f)), _resident((f, d))]
    if final_norm is not None:
        args.append(final_norm.reshape(1, d))
        specs.append(_resident((1, d)))
    return pl.pallas_call(
        functools.partial(_ffn_kernel, has_mix=mix is not None,
                          final_norm=final_norm is not None, nf=nf, tf=tf),
        grid=(n // tm,),
        in_specs=specs,
        out_specs=pl.BlockSpec((tm, d), row),
        out_shape=jax.ShapeDtypeStruct((n, d), F32),
        scratch_shapes=[pltpu.VMEM((tm, d), BF16), pltpu.VMEM((tm, d), F32)],
        compiler_params=pltpu.CompilerParams(
            dimension_semantics=("parallel",), vmem_limit_bytes=VMEM_LIMIT),
        name="ffn_mix" if mix is not None else "ffn",
    )(*args)


def _inproj_kernel(x_ref, g_ref, w_ref, b_ref, mix_ref, prw_ref, pat_ref, last_row,
                   *, tiles_per_seq):
    h = _rms(x_ref[...], g_ref[...]).astype(BF16)
    p = _dot(h, w_ref[...])
    pat_ref[...] = p[:, RWKV_COLS:] + b_ref[...]
    pr = p[:, :RWKV_COLS]
    prev_row = jnp.where(pl.program_id(0) % tiles_per_seq == 0, 0.0, last_row[0:1, :])
    row = lax.broadcasted_iota(jnp.int32, pr.shape, 0)
    p_prev = jnp.where(row == 0, prev_row, pltpu.roll(pr, 1, axis=0))
    prw_ref[...] = pr + (p_prev - pr) * mix_ref[...]
    last_row[0:1, :] = pr[pr.shape[0] - 1:, :]


def _in_proj(x, seq_len, norm, w_in, b_attn, shift_mix):
    n, d = x.shape
    cols = w_in.shape[1]
    tm = INPROJ_ROWS
    bias = b_attn.astype(F32).reshape(1, ATTN_COLS)
    row = lambda i: (i, 0)
    return pl.pallas_call(
        functools.partial(_inproj_kernel, tiles_per_seq=seq_len // tm),
        grid=(n // tm,),
        in_specs=[pl.BlockSpec((tm, d), row), _resident((1, d)), _resident((d, cols)),
                  _resident((1, ATTN_COLS)), _resident((1, RWKV_COLS))],
        out_specs=[pl.BlockSpec((tm, RWKV_COLS), row), pl.BlockSpec((tm, ATTN_COLS), row)],
        out_shape=[jax.ShapeDtypeStruct((n, RWKV_COLS), F32),
                   jax.ShapeDtypeStruct((n, ATTN_COLS), F32)],
        scratch_shapes=[pltpu.VMEM((8, RWKV_COLS), F32)],
        compiler_params=pltpu.CompilerParams(
            dimension_semantics=("arbitrary",), vmem_limit_bytes=VMEM_LIMIT),
        name="in_proj",
    )(x, norm.reshape(1, d), w_in.astype(BF16), bias, shift_mix.astype(F32).reshape(1, -1))


def _rwkv_prep_kernel(ps_ref, w0_ref, w2_ref, a0_ref, a2_ref, g2_ref, kk_ref, ka_ref, rk_ref,
                      seg_ref,
                      m_ref, n_ref, r_ref, o_ref, bonus_ref, gate_ref,
                      at_s, bt_s, kt_s, rt_s, v_s, bh_s, kh_s, rf_s, gam_s, *, n_chunks):
    C = CHUNK
    W = RWKV_WIDTH
    ps = ps_ref[...]
    r = ps[:, 0:W]
    k = ps[:, W:2 * W]
    v = ps[:, 2 * W:3 * W]
    wa = ps[:, 3 * W:3 * W + LANES]
    gl = ps[:, 3 * W + LANES:3 * W + 2 * LANES]
    z = w0_ref[...] + _dot(jnp.tanh(wa).astype(BF16), w2_ref[...])
    lw = -math.exp(-0.5) * jax.nn.sigmoid(z)
    a = jax.nn.sigmoid(a0_ref[...] + _dot(wa.astype(BF16), a2_ref[...]))
    gate_ref[...] = _dot(jax.nn.sigmoid(gl).astype(BF16), g2_ref[...]).astype(BF16)
    seg = seg_ref[...]
    kk = k * kk_ref[...]
    kk = kk * jnp.minimum(lax.rsqrt(_seg_sum(kk * kk, seg)), 1e12)
    k = k * (1.0 + (a - 1.0) * ka_ref[...])
    b = kk * a
    bonus_ref[...] = (_seg_sum(r * k * rk_ref[...], seg) * v).astype(BF16)
    v_s[...] = v.astype(BF16)
    cs = lw
    row_in_chunk = lax.broadcasted_iota(jnp.int32, lw.shape, 0) & (C - 1)
    for shift in (1, 2, 4, 8, 16, 32):
        cs = cs + jnp.where(row_in_chunk >= shift, pltpu.roll(cs, shift, axis=0), 0.0)
    for c in range(n_chunks):
        rows = slice(c * C, (c + 1) * C)
        cs_c = cs[rows]
        gam = jnp.exp(cs[(c + 1) * C - 1:(c + 1) * C, :])
        e_neg = jnp.exp(-cs_c)
        rt = r[rows] * jnp.exp(cs_c)
        bt = b[rows] * e_neg
        kt = k[rows] * e_neg
        at_s[rows, :] = (-kk[rows] * jnp.exp(cs_c - lw[rows])).astype(BF16)
        bt_s[rows, :] = bt.astype(BF16)
        kt_s[rows, :] = kt.astype(BF16)
        rt_s[rows, :] = rt.astype(BF16)
        rf_s[rows, :] = rt
        bh_s[rows, :] = bt * gam
        kh_s[rows, :] = kt * gam
        gam_s[c:c + 1, :] = gam

    rowi = lax.broadcasted_iota(jnp.int32, (2 * C, 2 * SLAB), 0)
    lane_s = lax.broadcasted_iota(jnp.int32, (2 * C, 2 * SLAB), 1) & (C - 1)
    keep_a1 = lane_s < (rowi & (C - 1)) + jnp.where(rowi < C, 0, 1)
    eye_pack = jnp.where((lax.broadcasted_iota(jnp.int32, (C, SLAB), 1) & (C - 1))
                         == lax.broadcasted_iota(jnp.int32, (C, SLAB), 0), 1.0, 0.0).astype(F32)
    same_head = ((lax.broadcasted_iota(jnp.int32, (SLAB, 2 * SLAB), 0) >> 6)
                 == _lane_block((SLAB, 2 * SLAB), HEAD_DIM))
    diag = (lax.broadcasted_iota(jnp.int32, (SLAB, SLAB), 0)
            == lax.broadcasted_iota(jnp.int32, (SLAB, SLAB), 1))
    zeros_c = jnp.zeros((C, SLAB), BF16)

    def fold(x):
        out = x[:HEAD_DIM]
        for h in range(1, HEADS_PER_SLAB):
            out = out + x[h * HEAD_DIM:(h + 1) * HEAD_DIM]
        return out

    chains = [(c, s) for c in range(n_chunks) for s in range(N_SLABS)]
    ld = lambda ref: [ref[c * C:(c + 1) * C, s * SLAB:(s + 1) * SLAB] for c, s in chains]
    at, bt, kt, rt, vv = ld(at_s), ld(bt_s), ld(kt_s), ld(rt_s), ld(v_s)
    a1 = [jnp.where(keep_a1,
                    _dot_nt(jnp.concatenate([x, y], axis=0),
                            jnp.concatenate([_block_diag(p), _block_diag(q)], axis=0)), 0.0)
          for x, y, p, q in zip(at, rt, bt, kt)]
    aab = [x[:C, :SLAB] for x in a1]
    kv = [_dot(x[:, SLAB:].astype(BF16), _block_diag(y)) for x, y in zip(a1, vv)]
    akv = [x[:C].astype(BF16) for x in kv]
    arkv = [x[C:] for x in kv]
    tinv = [eye_pack + x for x in aab]
    pwb = [x.astype(BF16) for x in aab]
    pwb = [_dot(x, _block_diag(x)).astype(BF16) for x in pwb]
    for step in range(5):
        if step < 4:
            res = [_dot(jnp.concatenate([p, t.astype(BF16)], axis=0), _block_diag(p))
                   for t, p in zip(tinv, pwb)]
            pwb = [x[:C].astype(BF16) for x in res]
            tinv = [t + x[C:] for t, x in zip(tinv, res)]
        else:
            tinv = [t + _dot(t.astype(BF16), _block_diag(p)) for t, p in zip(tinv, pwb)]
    wu = [_dot(t.astype(BF16), _block_diag(jnp.concatenate([x, y], axis=1))).astype(BF16)
          for t, x, y in zip(tinv, at, akv)]
    for i, (c, s) in enumerate(chains):
        rows = slice(c * C, (c + 1) * C)
        sl = slice(s * SLAB, (s + 1) * SLAB)
        lhs_t = jnp.concatenate([bh_s[rows, sl], kh_s[rows, sl]], axis=0)
        rhs = jnp.concatenate([wu[i], jnp.concatenate([zeros_c, vv[i]], axis=1)], axis=0)
        mn = _dot(lhs_t.T.astype(BF16), rhs)
        mn = jnp.where(same_head, mn, 0.0)
        m_ref[rows, sl] = fold(mn[:, :SLAB] + jnp.where(diag, gam_s[c:c + 1, sl], 0.0)).astype(BF16)
        n_ref[rows, sl] = fold(mn[:, SLAB:]).astype(BF16)
        ro = _dot(a1[i][C:, :SLAB].astype(BF16), _block_diag(wu[i]))
        r_ref[rows, sl] = (rf_s[rows, sl] + ro[:, :SLAB]).astype(BF16)
        o_ref[rows, sl] = (ro[:, SLAB:] + arkv[i]).astype(BF16)


def _rwkv_prep(p_shift, w0, w2, a0, a2, g2, k_k, k_a, r_k):
    n = p_shift.shape[0]
    W = RWKV_WIDTH
    tt = RWKV_ROWS
    lane_head = jnp.arange(LANES) // HEAD_DIM
    seg = (lane_head[:, None] == lane_head[None, :]).astype(BF16)
    w2p = jnp.concatenate([w2, jnp.zeros((AAA_LORA, W), w2.dtype)], axis=0).astype(BF16)
    a2p = jnp.concatenate([jnp.zeros((DECAY_LORA, W), a2.dtype), a2], axis=0).astype(BF16)
    row = lambda i: (i, 0)
    vec = lambda t: t.astype(F32).reshape(1, -1)
    bf16_out = jax.ShapeDtypeStruct((n, W), BF16)
    return pl.pallas_call(
        functools.partial(_rwkv_prep_kernel, n_chunks=tt // CHUNK),
        grid=(n // tt,),
        in_specs=[pl.BlockSpec((tt, RWKV_COLS), row),
                  _resident((1, W)), _resident((LANES, W)), _resident((1, W)),
                  _resident((LANES, W)), _resident((GATE_LORA, W)),
                  _resident((1, W)), _resident((1, W)), _resident((1, W)),
                  _resident((LANES, LANES))],
        out_specs=[pl.BlockSpec((tt, W), row)] * 6,
        out_shape=[bf16_out] * 6,
        scratch_shapes=[pltpu.VMEM((tt, W), BF16)] * 5 + [pltpu.VMEM((tt, W), F32)] * 3
                       + [pltpu.VMEM((tt // CHUNK, W), F32)],
        compiler_params=pltpu.CompilerParams(
            dimension_semantics=("parallel",), vmem_limit_bytes=VMEM_LIMIT),
        name="rwkv_prep",
    )(p_shift, vec(w0), w2p, vec(a0), a2p, g2.astype(BF16), vec(k_k), vec(k_a), vec(r_k),
      seg)


def _rwkv_scan_kernel(m_ref, n_ref, r_ref, o_ref, bonus_ref, gate_ref, lnw_ref, lnb_ref, seg_ref,
                      out_ref, state, obuf, *, n_batch, n_chunks):
    @pl.when(pl.program_id(0) == 0)
    def _():
        state[...] = jnp.zeros_like(state)

    C = CHUNK
    for c in range(n_chunks):
        rows = slice(c * C, (c + 1) * C)
        for b in range(n_batch):
            for s in range(N_SLABS):
                sl = slice(s * SLAB, (s + 1) * SLAB)
                s0 = state[b * N_SLABS + s]
                lhs = jnp.concatenate([m_ref[b, rows, sl], r_ref[b, rows, sl]], axis=0)
                res = _dot(lhs, _block_diag(s0.astype(BF16)))
                state[b * N_SLABS + s] = res[:C] + n_ref[b, rows, sl]
                obuf[b, rows, sl] = res[C:] + o_ref[b, rows, sl]

    seg_mean = seg_ref[...]
    for b in range(n_batch):
        o = obuf[b]
        d = o - _seg_sum(o, seg_mean)
        var = _seg_sum(d * d, seg_mean)
        y = d * lax.rsqrt(var + GN_EPS) * lnw_ref[...] + lnb_ref[...]
        out_ref[b] = ((y + bonus_ref[b]) * gate_ref[b]).astype(BF16)


def _rwkv_scan(mats, n_batch, seq_len, ln_w, ln_b):
    W = RWKV_WIDTH
    tt = RWKV_ROWS
    lane_head = jnp.arange(LANES) // HEAD_DIM
    seg_mean = ((lane_head[:, None] == lane_head[None, :]).astype(F32) / HEAD_DIM).astype(BF16)
    mats = [t.reshape(n_batch, seq_len, W) for t in mats]
    blk = pl.BlockSpec((n_batch, tt, W), lambda i: (0, i, 0))
    out = pl.pallas_call(
        functools.partial(_rwkv_scan_kernel, n_batch=n_batch, n_chunks=tt // CHUNK),
        grid=(seq_len // tt,),
        in_specs=[blk] * 6 + [_resident((1, W)), _resident((1, W)), _resident((LANES, LANES))],
        out_specs=blk,
        out_shape=jax.ShapeDtypeStruct((n_batch, seq_len, W), BF16),
        scratch_shapes=[pltpu.VMEM((n_batch * N_SLABS, CHUNK, SLAB), F32),
                        pltpu.VMEM((n_batch, tt, W), F32)],
        compiler_params=pltpu.CompilerParams(
            dimension_semantics=("arbitrary",), vmem_limit_bytes=VMEM_LIMIT),
        name="rwkv_scan",
    )(*mats, ln_w.astype(F32).reshape(1, W), ln_b.astype(F32).reshape(1, W), seg_mean)
    return out.reshape(n_batch * seq_len, W)


def _attn_kernel(sink_ref, q_ref, kp_ref, kc_ref, vp_ref, vc_ref, out_ref, *, n_blocks):
    BQ = WINDOW
    first = pl.program_id(1) == 0
    kx = jnp.concatenate([kp_ref[...], kc_ref[...]], axis=0)
    vx = jnp.concatenate([vp_ref[...], vc_ref[...]], axis=0)
    lo = _lane_block(kx.shape, HEAD_DIM, LANES) == 0
    kxr = pltpu.roll(kx, HEAD_DIM, axis=1)
    vxr = pltpu.roll(vx, HEAD_DIM, axis=1)
    bf = lambda t: t.astype(BF16)
    k_lo = [bf(jnp.where(lo, kx, 0.0)), bf(jnp.where(lo, kxr, 0.0))]
    k_hi = [bf(jnp.where(lo, 0.0, kxr)), bf(jnp.where(lo, 0.0, kx))]
    v_lo = [bf(jnp.where(lo, vx, 0.0)), bf(jnp.where(lo, vxr, 0.0))]
    v_hi = [bf(jnp.where(lo, 0.0, vxr)), bf(jnp.where(lo, 0.0, vx))]

    qi = lax.broadcasted_iota(jnp.int32, (BQ, 2 * BQ), 0)
    kj = lax.broadcasted_iota(jnp.int32, (BQ, 2 * BQ), 1)
    dist = qi + BQ - kj
    band = (dist >= 0) & (dist < WINDOW)
    band_first = band & (kj >= jnp.where(first, BQ, 0))
    col0 = lax.broadcasted_iota(jnp.int32, (1, 2 * BQ), 1) == 0
    log2e = math.log2(math.e)
    fill = [jnp.where(col0, sink_ref[h] * log2e, -jnp.inf) for h in range(N_Q_HEADS)]
    vrow = lax.broadcasted_iota(jnp.int32, (4 * BQ, LANES), 0) & (2 * BQ - 1)
    ones_sel = jnp.where((lax.broadcasted_iota(jnp.int32, (4 * BQ, LANES), 0) < 2 * BQ)
                         == (_lane_block((4 * BQ, LANES), HEAD_DIM, LANES) == 0),
                         1.0, 0.0).astype(BF16)
    qscale = HEAD_DIM ** -0.5 * log2e
    n_pairs = N_Q_HEADS // 2

    kv_of = [(2 * j) // (N_Q_HEADS // N_KV_HEADS) for j in range(n_pairs)]
    units = [(qb, j) for qb in range(n_blocks) for j in range(n_pairs)]
    kcat, vcat = {}, {}
    for qb in range(n_blocks):
        krows = slice(qb * BQ, (qb + 2) * BQ)
        for g in range(N_KV_HEADS):
            kcat[qb, g] = jnp.concatenate([k_lo[g][krows], k_hi[g][krows]], axis=0)
            vcat[qb, g] = jnp.concatenate(
                [jnp.where(vrow == 0, 0.0,
                           jnp.concatenate([v_lo[g][krows], v_hi[g][krows]], axis=0)),
                 ones_sel], axis=1)
    s = [_dot_nt((q_ref[qb * BQ:(qb + 1) * BQ, j * LANES:(j + 1) * LANES] * qscale).astype(BF16),
                 kcat[qb, kv_of[j]]) for qb, j in units]
    sh = [[jnp.where(band_first if qb == 0 else band,
                     x[:, hh * 2 * BQ:(hh + 1) * 2 * BQ], fill[2 * j + hh])
           for hh in range(2)] for x, (qb, j) in zip(s, units)]
    mx = [[jnp.max(x, axis=-1, keepdims=True) for x in row] for row in sh]
    e = [jnp.concatenate([jnp.exp2(x - m).astype(BF16) for x, m in zip(xr, mr)], axis=1)
         for xr, mr in zip(sh, mx)]
    pv = [_dot(x, vcat[qb, kv_of[j]]) for x, (qb, j) in zip(e, units)]
    for x, (qb, j) in zip(pv, units):
        out_ref[qb * BQ:(qb + 1) * BQ, j * LANES:(j + 1) * LANES] = (
            x[:, :LANES] / x[:, LANES:]).astype(BF16)


def _attn(p_attn, n_batch, seq_len, sinks):
    n = p_attn.shape[0]
    rows = ATTN_ROWS
    n_blocks = rows // WINDOW
    steps = seq_len // rows
    kcol = ATTN_WIDTH // LANES
    vcol = kcol + KV_WIDTH // LANES
    cur = lambda b, i: b * steps + i
    prev = lambda b, i: (b * steps + i) * n_blocks - jnp.minimum(i, 1)
    return pl.pallas_call(
        functools.partial(_attn_kernel, n_blocks=n_blocks),
        grid=(n_batch, steps),
        in_specs=[pl.BlockSpec(memory_space=pltpu.SMEM),
                  pl.BlockSpec((rows, ATTN_WIDTH), lambda b, i: (cur(b, i), 0)),
                  pl.BlockSpec((WINDOW, KV_WIDTH), lambda b, i: (prev(b, i), kcol)),
                  pl.BlockSpec((rows, KV_WIDTH), lambda b, i: (cur(b, i), kcol)),
                  pl.BlockSpec((WINDOW, KV_WIDTH), lambda b, i: (prev(b, i), vcol)),
                  pl.BlockSpec((rows, KV_WIDTH), lambda b, i: (cur(b, i), vcol))],
        out_specs=pl.BlockSpec((rows, ATTN_WIDTH), lambda b, i: (cur(b, i), 0)),
        out_shape=jax.ShapeDtypeStruct((n, ATTN_WIDTH), BF16),
        compiler_params=pltpu.CompilerParams(
            dimension_semantics=("parallel", "arbitrary"), vmem_limit_bytes=VMEM_LIMIT),
        name="swa_attn",
    )(sinks.astype(F32), p_attn, p_attn, p_attn, p_attn, p_attn)


def kernel(x, norm_ffn1, ffn1_gate, ffn1_up, ffn1_down, norm_mix, w_in, b_in_attn, rwkv_shift_mix, rwkv_w0, rwkv_w2, rwkv_a0, rwkv_a2, rwkv_g2, rwkv_k_k, rwkv_k_a, rwkv_r_k, rwkv_ln_w, rwkv_ln_b, attn_sinks, w_out, norm_ffn2, ffn2_gate, ffn2_up, ffn2_down, norm_final):
    n_batch, seq_len, d = x.shape
    depth = w_in.shape[0]
    h = x.reshape(n_batch * seq_len, d)
    for l in range(depth):
        h = _ffn(h, norm_ffn1[l], ffn1_gate[l], ffn1_up[l], ffn1_down[l])
        p_shift, p_attn = _in_proj(h, seq_len, norm_mix[l], w_in[l], b_in_attn[l],
                                   rwkv_shift_mix[l])
        mats = _rwkv_prep(p_shift, rwkv_w0[l], rwkv_w2[l], rwkv_a0[l], rwkv_a2[l], rwkv_g2[l],
                          rwkv_k_k[l], rwkv_k_a[l], rwkv_r_k[l])
        o_rwkv = _rwkv_scan(mats, n_batch, seq_len, rwkv_ln_w[l], rwkv_ln_b[l])
        o_attn = _attn(p_attn, n_batch, seq_len, attn_sinks[l])
        h = _ffn(h, norm_ffn2[l], ffn2_gate[l], ffn2_up[l], ffn2_down[l],
                 mix=(o_rwkv, o_attn, w_out[l]),
                 final_norm=norm_final if l == depth - 1 else None)
    return h.reshape(n_batch, seq_len, d)
```

```python
import functools
import math

import jax
import jax.numpy as jnp
from jax import lax
from jax.experimental import pallas as pl
from jax.experimental.pallas import tpu as pltpu

F32 = jnp.float32
BF16 = jnp.bfloat16

HEAD_DIM = 64
LANES = 128
SLAB = 128
HEADS_PER_SLAB = SLAB // HEAD_DIM
CHUNK = 64
N_RWKV_HEADS = 8
RWKV_WIDTH = N_RWKV_HEADS * HEAD_DIM
N_SLABS = RWKV_WIDTH // SLAB
DECAY_LORA = 64
AAA_LORA = 64
GATE_LORA = 128
RWKV_COLS = 3 * RWKV_WIDTH + DECAY_LORA + AAA_LORA + GATE_LORA
N_Q_HEADS = 8
N_KV_HEADS = 2
ATTN_WIDTH = N_Q_HEADS * HEAD_DIM
KV_WIDTH = N_KV_HEADS * HEAD_DIM
ATTN_COLS = ATTN_WIDTH + 2 * KV_WIDTH
WINDOW = 128
GN_EPS = 64e-5
NORM_EPS = 1e-5
VMEM_LIMIT = 56 * 1024 * 1024

FFN_ROWS = 512
FFN_COLS = 256
INPROJ_ROWS = 1024
RWKV_ROWS = 512
SCAN_ROWS = 1024
ATTN_ROWS = 1024


def _dot(a, b):
    return jnp.dot(a, b, preferred_element_type=F32)


def _dot_nt(a, b):
    return lax.dot_general(a, b, (((1,), (1,)), ((), ())), preferred_element_type=F32)


def _rms(x, g):
    ms = jnp.mean(x * x, axis=-1, keepdims=True)
    return x * lax.rsqrt(ms + NORM_EPS) * g


def _lane_block(shape, width, span=SLAB):
    lane = lax.broadcasted_iota(jnp.int32, shape, 1)
    return (lane & (span - 1)) >> (width.bit_length() - 1)


def _block_diag(x, width=HEAD_DIM):
    blk = _lane_block(x.shape, width)
    return jnp.concatenate([jnp.where(blk == g, x, 0.0) for g in range(SLAB // width)], axis=0)


def _seg_sum(x, seg):
    xb = x.astype(BF16)
    rows, groups = x.shape[0], x.shape[1] // LANES
    y = _dot(jnp.concatenate([xb[:, j * LANES:(j + 1) * LANES] for j in range(groups)], axis=0), seg)
    return jnp.concatenate([y[j * rows:(j + 1) * rows] for j in range(groups)], axis=1)


def _resident(shape):
    return pl.BlockSpec(shape, lambda *_: (0,) * len(shape), pipeline_mode=pl.Buffered(1))


def _ffn_kernel(*refs, has_mix, final_norm, nf, tf):
    refs = list(refs)
    x_ref = refs.pop(0)
    if has_mix:
        orw_ref, oat_ref, wo_ref = refs[:3]
        refs = refs[3:]
    g_ref, wg_ref, wu_ref, wd_ref = refs[:4]
    refs = refs[4:]
    if final_norm:
        gf_ref = refs.pop(0)
    out_ref, h_s, acc = refs

    x = x_ref[...]
    if has_mix:
        x = (x + _dot(orw_ref[...], wo_ref[:RWKV_WIDTH].astype(BF16))
             + _dot(oat_ref[...], wo_ref[RWKV_WIDTH:].astype(BF16)))
        out_ref[...] = x
    h_s[...] = _rms(x, g_ref[...]).astype(BF16)

    def down(f):
        cols = pl.ds(f * tf if isinstance(f, int) else pl.multiple_of(f * tf, tf), tf)
        h = h_s[...]
        gate = _dot(h, wg_ref[:, cols].astype(BF16))
        up = _dot(h, wu_ref[:, cols].astype(BF16))
        act = (gate * jax.nn.sigmoid(gate) * up).astype(BF16)
        return _dot(act, wd_ref[cols, :].astype(BF16))

    acc[...] = down(0)

    for f in range(1, nf):
        acc[...] += down(f)
    res = out_ref[...] if has_mix else x_ref[...]
    y = res + 0.5 * acc[...]
    if final_norm:
        y = _rms(y, gf_ref[...])
    out_ref[...] = y


def _ffn(x, norm, w_gate, w_up, w_down, mix=None, final_norm=None):
    n, d = x.shape
    f = w_gate.shape[1]
    tm, tf = FFN_ROWS, FFN_COLS
    nf = f // tf
    row = lambda i: (i, 0)
    wg, wu, wd = w_gate, w_up, w_down
    args = [x]
    specs = [pl.BlockSpec((tm, d), row)]
    if mix is not None:
        o_rwkv, o_attn, w_out = mix
        args += [o_rwkv, o_attn, w_out]
        specs += [pl.BlockSpec((tm, RWKV_WIDTH), row), pl.BlockSpec((tm, ATTN_WIDTH), row),
                  _resident((RWKV_WIDTH + ATTN_WIDTH, d))]
    args += [norm.reshape(1, d), wg, wu, wd]
    specs += [_resident((1, d)), _resident((d, f)), _resident((d, f)), _resident((f, d))]
    if final_norm is not None:
        args.append(final_norm.reshape(1, d))
        specs.append(_resident((1, d)))
    return pl.pallas_call(
        functools.partial(_ffn_kernel, has_mix=mix is not None,
                          final_norm=final_norm is not None, nf=nf, tf=tf),
        grid=(n // tm,),
        in_specs=specs,
        out_specs=pl.BlockSpec((tm, d), row),
        out_shape=jax.ShapeDtypeStruct((n, d), F32),
        scratch_shapes=[pltpu.VMEM((tm, d), BF16), pltpu.VMEM((tm, d), F32)],
        compiler_params=pltpu.CompilerParams(
            dimension_semantics=("parallel",), vmem_limit_bytes=VMEM_LIMIT),
        name="ffn_mix" if mix is not None else "ffn",
    )(*args)


def _inproj_kernel(x_ref, g_ref, w_ref, b_ref, mix_ref, prw_ref, pat_ref, last_row,
                   *, tiles_per_seq):
    h = _rms(x_ref[...], g_ref[...]).astype(BF16)
    p = _dot(h, w_ref[...])
    pat_ref[...] = p[:, RWKV_COLS:] + b_ref[...]
    pr = p[:, :RWKV_COLS]
    prev_row = jnp.where(pl.program_id(0) % tiles_per_seq == 0, 0.0, last_row[0:1, :])
    row = lax.broadcasted_iota(jnp.int32, pr.shape, 0)
    p_prev = jnp.where(row == 0, prev_row, pltpu.roll(pr, 1, axis=0))
    prw_ref[...] = pr + (p_prev - pr) * mix_ref[...]
    last_row[0:1, :] = pr[pr.shape[0] - 1:, :]


def _in_proj(x, seq_len, norm, w_in, b_attn, shift_mix):
    n, d = x.shape
    cols = w_in.shape[1]
    tm = INPROJ_ROWS
    bias = b_attn.astype(F32).reshape(1, ATTN_COLS)
    row = lambda i: (i, 0)
    return pl.pallas_call(
        functools.partial(_inproj_kernel, tiles_per_seq=seq_len // tm),
        grid=(n // tm,),
        in_specs=[pl.BlockSpec((tm, d), row), _resident((1, d)), _resident((d, cols)),
                  _resident((1, ATTN_COLS)), _resident((1, RWKV_COLS))],
        out_specs=[pl.BlockSpec((tm, RWKV_COLS), row), pl.BlockSpec((tm, ATTN_COLS), row)],
        out_shape=[jax.ShapeDtypeStruct((n, RWKV_COLS), F32),
                   jax.ShapeDtypeStruct((n, ATTN_COLS), F32)],
        scratch_shapes=[pltpu.VMEM((8, RWKV_COLS), F32)],
        compiler_params=pltpu.CompilerParams(
            dimension_semantics=("arbitrary",), vmem_limit_bytes=VMEM_LIMIT),
        name="in_proj",
    )(x, norm.reshape(1, d), w_in.astype(BF16), bias, shift_mix.astype(F32).reshape(1, -1))


def _rwkv_prep_kernel(ps_ref, w0_ref, w2_ref, a0_ref, a2_ref, g2_ref, kk_ref, ka_ref, rk_ref,
                      seg_ref,
                      m_ref, n_ref, r_ref, o_ref, bonus_ref, gate_ref,
                      at_s, bt_s, kt_s, rt_s, v_s, bh_s, kh_s, rf_s, gam_s, *, n_chunks):
    C = CHUNK
    W = RWKV_WIDTH
    ps = ps_ref[...]
    r = ps[:, 0:W]
    k = ps[:, W:2 * W]
    v = ps[:, 2 * W:3 * W]
    wa = ps[:, 3 * W:3 * W + LANES]
    gl = ps[:, 3 * W + LANES:3 * W + 2 * LANES]
    z = w0_ref[...] + _dot(jnp.tanh(wa).astype(BF16), w2_ref[...])
    lw = -math.exp(-0.5) * jax.nn.sigmoid(z)
    a = jax.nn.sigmoid(a0_ref[...] + _dot(wa.astype(BF16), a2_ref[...]))
    gate_ref[...] = _dot(jax.nn.sigmoid(gl).astype(BF16), g2_ref[...]).astype(BF16)
    seg = seg_ref[...]
    kk = k * kk_ref[...]
    kk = kk * jnp.minimum(lax.rsqrt(_seg_sum(kk * kk, seg)), 1e12)
    k = k * (1.0 + (a - 1.0) * ka_ref[...])
    b = kk * a
    bonus_ref[...] = (_seg_sum(r * k * rk_ref[...], seg) * v).astype(BF16)
    v_s[...] = v.astype(BF16)
    cs = lw
    row_in_chunk = lax.broadcasted_iota(jnp.int32, lw.shape, 0) & (C - 1)
    for shift in (1, 2, 4, 8, 16, 32):
        cs = cs + jnp.where(row_in_chunk >= shift, pltpu.roll(cs, shift, axis=0), 0.0)
    for c in range(n_chunks):
        rows = slice(c * C, (c + 1) * C)
        cs_c = cs[rows]
        gam = jnp.exp(cs[(c + 1) * C - 1:(c + 1) * C, :])
        e_neg = jnp.exp(-cs_c)
        rt = r[rows] * jnp.exp(cs_c)
        bt = b[rows] * e_neg
        kt = k[rows] * e_neg
        at_s[rows, :] = (-kk[rows] * jnp.exp(cs_c - lw[rows])).astype(BF16)
        bt_s[rows, :] = bt.astype(BF16)
        kt_s[rows, :] = kt.astype(BF16)
        rt_s[rows, :] = rt.astype(BF16)
        rf_s[rows, :] = rt
        bh_s[rows, :] = bt * gam
        kh_s[rows, :] = kt * gam
        gam_s[c:c + 1, :] = gam

    rowi = lax.broadcasted_iota(jnp.int32, (2 * C, 2 * SLAB), 0)
    lane_s = lax.broadcasted_iota(jnp.int32, (2 * C, 2 * SLAB), 1) & (C - 1)
    keep_a1 = lane_s < (rowi & (C - 1)) + jnp.where(rowi < C, 0, 1)
    eye_pack = jnp.where((lax.broadcasted_iota(jnp.int32, (C, SLAB), 1) & (C - 1))
                         == lax.broadcasted_iota(jnp.int32, (C, SLAB), 0), 1.0, 0.0).astype(F32)
    same_head = ((lax.broadcasted_iota(jnp.int32, (SLAB, 2 * SLAB), 0) >> 6)
                 == _lane_block((SLAB, 2 * SLAB), HEAD_DIM))
    diag = (lax.broadcasted_iota(jnp.int32, (SLAB, SLAB), 0)
            == lax.broadcasted_iota(jnp.int32, (SLAB, SLAB), 1))
    zeros_c = jnp.zeros((C, SLAB), BF16)

    def fold(x):
        out = x[:HEAD_DIM]
        for h in range(1, HEADS_PER_SLAB):
            out = out + x[h * HEAD_DIM:(h + 1) * HEAD_DIM]
        return out

    chains = [(c, s) for c in range(n_chunks) for s in range(N_SLABS)]
    ld = lambda ref: [ref[c * C:(c + 1) * C, s * SLAB:(s + 1) * SLAB] for c, s in chains]
    at, bt, kt, rt, vv = ld(at_s), ld(bt_s), ld(kt_s), ld(rt_s), ld(v_s)
    a1 = [jnp.where(keep_a1,
                    _dot_nt(jnp.concatenate([x, y], axis=0),
                            jnp.concatenate([_block_diag(p), _block_diag(q)], axis=0)), 0.0)
          for x, y, p, q in zip(at, rt, bt, kt)]
    aab = [x[:C, :SLAB] for x in a1]
    kv = [_dot(x[:, SLAB:].astype(BF16), _block_diag(y)) for x, y in zip(a1, vv)]
    akv = [x[:C].astype(BF16) for x in kv]
    arkv = [x[C:] for x in kv]
    tinv = [eye_pack + x for x in aab]
    pwb = [x.astype(BF16) for x in aab]
    pwb = [_dot(x, _block_diag(x)).astype(BF16) for x in pwb]
    for step in range(5):
        if step < 4:
            res = [_dot(jnp.concatenate([p, t.astype(BF16)], axis=0), _block_diag(p))
                   for t, p in zip(tinv, pwb)]
            pwb = [x[:C].astype(BF16) for x in res]
            tinv = [t + x[C:] for t, x in zip(tinv, res)]
        else:
            tinv = [t + _dot(t.astype(BF16), _block_diag(p)) for t, p in zip(tinv, pwb)]
    wu = [_dot(t.astype(BF16), _block_diag(jnp.concatenate([x, y], axis=1))).astype(BF16)
          for t, x, y in zip(tinv, at, akv)]
    for i, (c, s) in enumerate(chains):
        rows = slice(c * C, (c + 1) * C)
        sl = slice(s * SLAB, (s + 1) * SLAB)
        lhs_t = jnp.concatenate([bh_s[rows, sl], kh_s[rows, sl]], axis=0)
        rhs = jnp.concatenate([wu[i], jnp.concatenate([zeros_c, vv[i]], axis=1)], axis=0)
        mn = _dot(lhs_t.T.astype(BF16), rhs)
        mn = jnp.where(same_head, mn, 0.0)
        m_ref[rows, sl] = fold(mn[:, :SLAB] + jnp.where(diag, gam_s[c:c + 1, sl], 0.0)).astype(BF16)
        n_ref[rows, sl] = fold(mn[:, SLAB:]).astype(BF16)
        ro = _dot(a1[i][C:, :SLAB].astype(BF16), _block_diag(wu[i]))
        r_ref[rows, sl] = (rf_s[rows, sl] + ro[:, :SLAB]).astype(BF16)
        o_ref[rows, sl] = (ro[:, SLAB:] + arkv[i]).astype(BF16)


def _rwkv_prep(p_shift, w0, w2, a0, a2, g2, k_k, k_a, r_k):
    n = p_shift.shape[0]
    W = RWKV_WIDTH
    tt = RWKV_ROWS
    lane_head = jnp.arange(LANES) // HEAD_DIM
    seg = (lane_head[:, None] == lane_head[None, :]).astype(BF16)
    w2p = jnp.concatenate([w2, jnp.zeros((AAA_LORA, W), w2.dtype)], axis=0).astype(BF16)
    a2p = jnp.concatenate([jnp.zeros((DECAY_LORA, W), a2.dtype), a2], axis=0).astype(BF16)
    row = lambda i: (i, 0)
    vec = lambda t: t.astype(F32).reshape(1, -1)
    bf16_out = jax.ShapeDtypeStruct((n, W), BF16)
    return pl.pallas_call(
        functools.partial(_rwkv_prep_kernel, n_chunks=tt // CHUNK),
        grid=(n // tt,),
        in_specs=[pl.BlockSpec((tt, RWKV_COLS), row),
                  _resident((1, W)), _resident((LANES, W)), _resident((1, W)),
                  _resident((LANES, W)), _resident((GATE_LORA, W)),
                  _resident((1, W)), _resident((1, W)), _resident((1, W)),
                  _resident((LANES, LANES))],
        out_specs=[pl.BlockSpec((tt, W), row)] * 6,
        out_shape=[bf16_out] * 6,
        scratch_shapes=[pltpu.VMEM((tt, W), BF16)] * 5 + [pltpu.VMEM((tt, W), F32)] * 3
                       + [pltpu.VMEM((tt // CHUNK, W), F32)],
        compiler_params=pltpu.CompilerParams(
            dimension_semantics=("parallel",), vmem_limit_bytes=VMEM_LIMIT),
        name="rwkv_prep",
    )(p_shift, vec(w0), w2p, vec(a0), a2p, g2.astype(BF16), vec(k_k), vec(k_a), vec(r_k),
      seg)


def _rwkv_scan_kernel(m_ref, n_ref, r_ref, o_ref, bonus_ref, gate_ref, lnw_ref, lnb_ref, seg_ref,
                      out_ref, state, obuf, *, n_batch, n_chunks):
    @pl.when(pl.program_id(0) == 0)
    def _():
        state[...] = jnp.zeros_like(state)

    C = CHUNK
    for c in range(n_chunks):
        rows = slice(c * C, (c + 1) * C)
        for b in range(n_batch):
            for s in range(N_SLABS):
                sl = slice(s * SLAB, (s + 1) * SLAB)
                s0 = state[b * N_SLABS + s]
                lhs = jnp.concatenate([m_ref[b, rows, sl], r_ref[b, rows, sl]], axis=0)
                res = _dot(lhs, _block_diag(s0.astype(BF16)))
                state[b * N_SLABS + s] = res[:C] + n_ref[b, rows, sl]
                obuf[b, rows, sl] = res[C:] + o_ref[b, rows, sl]

    seg_mean = seg_ref[...]
    for b in range(n_batch):
        o = obuf[b]
        d = o - _seg_sum(o, seg_mean)
        var = _seg_sum(d * d, seg_mean)
        y = d * lax.rsqrt(var + GN_EPS) * lnw_ref[...] + lnb_ref[...]
        out_ref[b] = ((y + bonus_ref[b]) * gate_ref[b]).astype(BF16)


def _rwkv_scan(mats, n_batch, seq_len, ln_w, ln_b):
    W = RWKV_WIDTH
    tt = SCAN_ROWS
    lane_head = jnp.arange(LANES) // HEAD_DIM
    seg_mean = ((lane_head[:, None] == lane_head[None, :]).astype(F32) / HEAD_DIM).astype(BF16)
    mats = [t.reshape(n_batch, seq_len, W) for t in mats]
    blk = pl.BlockSpec((n_batch, tt, W), lambda i: (0, i, 0))
    out = pl.pallas_call(
        functools.partial(_rwkv_scan_kernel, n_batch=n_batch, n_chunks=tt // CHUNK),
        grid=(seq_len // tt,),
        in_specs=[blk] * 6 + [_resident((1, W)), _resident((1, W)), _resident((LANES, LANES))],
        out_specs=blk,
        out_shape=jax.ShapeDtypeStruct((n_batch, seq_len, W), BF16),
        scratch_shapes=[pltpu.VMEM((n_batch * N_SLABS, CHUNK, SLAB), F32),
                        pltpu.VMEM((n_batch, tt, W), F32)],
        compiler_params=pltpu.CompilerParams(
            dimension_semantics=("arbitrary",), vmem_limit_bytes=VMEM_LIMIT),
        name="rwkv_scan",
    )(*mats, ln_w.astype(F32).reshape(1, W), ln_b.astype(F32).reshape(1, W), seg_mean)
    return out.reshape(n_batch * seq_len, W)


def _attn_kernel(sink_ref, q_ref, kp_ref, kc_ref, vp_ref, vc_ref, out_ref, *, n_blocks):
    BQ = WINDOW
    first = pl.program_id(1) == 0
    kx = jnp.concatenate([kp_ref[...], kc_ref[...]], axis=0)
    vx = jnp.concatenate([vp_ref[...], vc_ref[...]], axis=0)
    lo = _lane_block(kx.shape, HEAD_DIM, LANES) == 0
    kxr = pltpu.roll(kx, HEAD_DIM, axis=1)
    vxr = pltpu.roll(vx, HEAD_DIM, axis=1)
    bf = lambda t: t.astype(BF16)
    k_lo = [bf(jnp.where(lo, kx, 0.0)), bf(jnp.where(lo, kxr, 0.0))]
    k_hi = [bf(jnp.where(lo, 0.0, kxr)), bf(jnp.where(lo, 0.0, kx))]
    v_lo = [bf(jnp.where(lo, vx, 0.0)), bf(jnp.where(lo, vxr, 0.0))]
    v_hi = [bf(jnp.where(lo, 0.0, vxr)), bf(jnp.where(lo, 0.0, vx))]

    qi = lax.broadcasted_iota(jnp.int32, (BQ, 2 * BQ), 0)
    kj = lax.broadcasted_iota(jnp.int32, (BQ, 2 * BQ), 1)
    dist = qi + BQ - kj
    band = (dist >= 0) & (dist < WINDOW)
    band_first = band & (kj >= jnp.where(first, BQ, 0))
    col0 = lax.broadcasted_iota(jnp.int32, (1, 2 * BQ), 1) == 0
    log2e = math.log2(math.e)
    fill = [jnp.where(col0, sink_ref[h] * log2e, -jnp.inf) for h in range(N_Q_HEADS)]
    vrow = lax.broadcasted_iota(jnp.int32, (4 * BQ, LANES), 0) & (2 * BQ - 1)
    ones_sel = jnp.where((lax.broadcasted_iota(jnp.int32, (4 * BQ, LANES), 0) < 2 * BQ)
                         == (_lane_block((4 * BQ, LANES), HEAD_DIM, LANES) == 0),
                         1.0, 0.0).astype(BF16)
    qscale = HEAD_DIM ** -0.5 * log2e
    n_pairs = N_Q_HEADS // 2

    kv_of = [(2 * j) // (N_Q_HEADS // N_KV_HEADS) for j in range(n_pairs)]
    units = [(qb, j) for qb in range(n_blocks) for j in range(n_pairs)]
    kcat, vcat = {}, {}
    for qb in range(n_blocks):
        krows = slice(qb * BQ, (qb + 2) * BQ)
        for g in range(N_KV_HEADS):
            kcat[qb, g] = jnp.concatenate([k_lo[g][krows], k_hi[g][krows]], axis=0)
            vcat[qb, g] = jnp.concatenate(
                [jnp.where(vrow == 0, 0.0,
                           jnp.concatenate([v_lo[g][krows], v_hi[g][krows]], axis=0)),
                 ones_sel], axis=1)
    s = [_dot_nt((q_ref[qb * BQ:(qb + 1) * BQ, j * LANES:(j + 1) * LANES] * qscale).astype(BF16),
                 kcat[qb, kv_of[j]]) for qb, j in units]
    sh = [[jnp.where(band_first if qb == 0 else band,
                     x[:, hh * 2 * BQ:(hh + 1) * 2 * BQ], fill[2 * j + hh])
           for hh in range(2)] for x, (qb, j) in zip(s, units)]
    mx = [[jnp.max(x, axis=-1, keepdims=True) for x in row] for row in sh]
    e = [jnp.concatenate([jnp.exp2(x - m).astype(BF16) for x, m in zip(xr, mr)], axis=1)
         for xr, mr in zip(sh, mx)]
    pv = [_dot(x, vcat[qb, kv_of[j]]) for x, (qb, j) in zip(e, units)]
    for x, (qb, j) in zip(pv, units):
        out_ref[qb * BQ:(qb + 1) * BQ, j * LANES:(j + 1) * LANES] = (
            x[:, :LANES] / x[:, LANES:]).astype(BF16)


def _attn(p_attn, n_batch, seq_len, sinks):
    n = p_attn.shape[0]
    rows = ATTN_ROWS
    n_blocks = rows // WINDOW
    steps = seq_len // rows
    kcol = ATTN_WIDTH // LANES
    vcol = kcol + KV_WIDTH // LANES
    cur = lambda b, i: b * steps + i
    prev = lambda b, i: (b * steps + i) * n_blocks - jnp.minimum(i, 1)
    return pl.pallas_call(
        functools.partial(_attn_kernel, n_blocks=n_blocks),
        grid=(n_batch, steps),
        in_specs=[pl.BlockSpec(memory_space=pltpu.SMEM),
                  pl.BlockSpec((rows, ATTN_WIDTH), lambda b, i: (cur(b, i), 0)),
                  pl.BlockSpec((WINDOW, KV_WIDTH), lambda b, i: (prev(b, i), kcol)),
                  pl.BlockSpec((rows, KV_WIDTH), lambda b, i: (cur(b, i), kcol)),
                  pl.BlockSpec((WINDOW, KV_WIDTH), lambda b, i: (prev(b, i), vcol)),
                  pl.BlockSpec((rows, KV_WIDTH), lambda b, i: (cur(b, i), vcol))],
        out_specs=pl.BlockSpec((rows, ATTN_WIDTH), lambda b, i: (cur(b, i), 0)),
        out_shape=jax.ShapeDtypeStruct((n, ATTN_WIDTH), BF16),
        compiler_params=pltpu.CompilerParams(
            dimension_semantics=("parallel", "arbitrary"), vmem_limit_bytes=VMEM_LIMIT),
        name="swa_attn",
    )(sinks.astype(F32), p_attn, p_attn, p_attn, p_attn, p_attn)


def kernel(x, norm_ffn1, ffn1_gate, ffn1_up, ffn1_down, norm_mix, w_in, b_in_attn, rwkv_shift_mix, rwkv_w0, rwkv_w2, rwkv_a0, rwkv_a2, rwkv_g2, rwkv_k_k, rwkv_k_a, rwkv_r_k, rwkv_ln_w, rwkv_ln_b, attn_sinks, w_out, norm_ffn2, ffn2_gate, ffn2_up, ffn2_down, norm_final):
    n_batch, seq_len, d = x.shape
    depth = w_in.shape[0]
    h = x.reshape(n_batch * seq_len, d)
    for l in range(depth):
        h = _ffn(h, norm_ffn1[l], ffn1_gate[l], ffn1_up[l], ffn1_down[l])
        p_shift, p_attn = _in_proj(h, seq_len, norm_mix[l], w_in[l], b_in_attn[l],
                                   rwkv_shift_mix[l])
        mats = _rwkv_prep(p_shift, rwkv_w0[l], rwkv_w2[l], rwkv_a0[l], rwkv_a2[l], rwkv_g2[l],
                          rwkv_k_k[l], rwkv_k_a[l], rwkv_r_k[l])
        o_rwkv = _rwkv_scan(mats, n_batch, seq_len, rwkv_ln_w[l], rwkv_ln_b[l])
        o_attn = _attn(p_attn, n_batch, seq_len, attn_sinks[l])
        h = _ffn(h, norm_ffn2[l], ffn2_gate[l], ffn2_up[l], ffn2_down[l],
                 mix=(o_rwkv, o_attn, w_out[l]),
                 final_norm=norm_final if l == depth - 1 else None)
    return h.reshape(n_batch, seq_len, d)
```

```python
import functools
import math

import jax
import jax.numpy as jnp
from jax import lax
from jax.experimental import pallas as pl
from jax.experimental.pallas import tpu as pltpu

F32 = jnp.float32
BF16 = jnp.bfloat16

HEAD_DIM = 64
LANES = 128
SLAB = 128
HEADS_PER_SLAB = SLAB // HEAD_DIM
CHUNK = 64
N_RWKV_HEADS = 8
RWKV_WIDTH = N_RWKV_HEADS * HEAD_DIM
N_SLABS = RWKV_WIDTH // SLAB
DECAY_LORA = 64
AAA_LORA = 64
GATE_LORA = 128
RWKV_COLS = 3 * RWKV_WIDTH + DECAY_LORA + AAA_LORA + GATE_LORA
N_Q_HEADS = 8
N_KV_HEADS = 2
ATTN_WIDTH = N_Q_HEADS * HEAD_DIM
KV_WIDTH = N_KV_HEADS * HEAD_DIM
ATTN_COLS = ATTN_WIDTH + 2 * KV_WIDTH
WINDOW = 128
GN_EPS = 64e-5
NORM_EPS = 1e-5
VMEM_LIMIT = 56 * 1024 * 1024

FFN_ROWS = 512
FFN_COLS = 256
INPROJ_ROWS = 1024
RWKV_ROWS = 512
SCAN_ROWS = 1024
ATTN_ROWS = 1024


def _dot(a, b):
    return jnp.dot(a, b, preferred_element_type=F32)


def _dot_nt(a, b):
    return lax.dot_general(a, b, (((1,), (1,)), ((), ())), preferred_element_type=F32)


def _rms(x, g):
    ms = jnp.mean(x * x, axis=-1, keepdims=True)
    return x * lax.rsqrt(ms + NORM_EPS) * g


def _lane_block(shape, width, span=SLAB):
    lane = lax.broadcasted_iota(jnp.int32, shape, 1)
    return (lane & (span - 1)) >> (width.bit_length() - 1)


def _block_diag(x, width=HEAD_DIM):
    blk = _lane_block(x.shape, width)
    return jnp.concatenate([jnp.where(blk == g, x, 0.0) for g in range(SLAB // width)], axis=0)


def _seg_sum(x, seg):
    xb = x.astype(BF16)
    rows, groups = x.shape[0], x.shape[1] // LANES
    y = _dot(jnp.concatenate([xb[:, j * LANES:(j + 1) * LANES] for j in range(groups)], axis=0), seg)
    return jnp.concatenate([y[j * rows:(j + 1) * rows] for j in range(groups)], axis=1)


def _resident(shape):
    return pl.BlockSpec(shape, lambda *_: (0,) * len(shape), pipeline_mode=pl.Buffered(1))


def _ffn_kernel(*refs, has_mix, final_norm, nf, tf):
    refs = list(refs)
    x_ref = refs.pop(0)
    if has_mix:
        orw_ref, oat_ref, wo_ref = refs[:3]
        refs = refs[3:]
    g_ref, wg_ref, wu_ref, wd_ref = refs[:4]
    refs = refs[4:]
    if final_norm:
        gf_ref = refs.pop(0)
    out_ref, h_s, acc = refs

    x = x_ref[...]
    if has_mix:
        x = (x + _dot(orw_ref[...], wo_ref[:RWKV_WIDTH].astype(BF16))
             + _dot(oat_ref[...], wo_ref[RWKV_WIDTH:].astype(BF16)))
        out_ref[...] = x
    h_s[...] = _rms(x, g_ref[...]).astype(BF16)

    def down(f):
        cols = pl.ds(f * tf if isinstance(f, int) else pl.multiple_of(f * tf, tf), tf)
        h = h_s[...]
        gate = _dot(h, wg_ref[:, cols].astype(BF16))
        up = _dot(h, wu_ref[:, cols].astype(BF16))
        act = (gate * jax.nn.sigmoid(gate) * up).astype(BF16)
        return _dot(act, wd_ref[cols, :].astype(BF16))

    acc[...] = down(0)

    for f in range(1, nf):
        acc[...] += down(f)
    res = out_ref[...] if has_mix else x_ref[...]
    y = res + 0.5 * acc[...]
    if final_norm:
        y = _rms(y, gf_ref[...])
    out_ref[...] = y


def _ffn(x, norm, w_gate, w_up, w_down, mix=None, final_norm=None):
    n, d = x.shape
    f = w_gate.shape[1]
    tm, tf = FFN_ROWS, FFN_COLS
    nf = f // tf
    row = lambda i: (i, 0)
    wg, wu, wd = w_gate, w_up, w_down
    args = [x]
    specs = [pl.BlockSpec((tm, d), row)]
    if mix is not None:
        o_rwkv, o_attn, w_out = mix
        args += [o_rwkv, o_attn, w_out]
        specs += [pl.BlockSpec((tm, RWKV_WIDTH), row), pl.BlockSpec((tm, ATTN_WIDTH), row),
                  _resident((RWKV_WIDTH + ATTN_WIDTH, d))]
    args += [norm.reshape(1, d), wg, wu, wd]
    specs += [_resident((1, d)), _resident((d, f)), _resident((d, f)), _resident((f, d))]
    if final_norm is not None:
        args.append(final_norm.reshape(1, d))
        specs.append(_resident((1, d)))
    return pl.pallas_call(
        functools.partial(_ffn_kernel, has_mix=mix is not None,
                          final_norm=final_norm is not None, nf=nf, tf=tf),
        grid=(n // tm,),
        in_specs=specs,
        out_specs=pl.BlockSpec((tm, d), row),
        out_shape=jax.ShapeDtypeStruct((n, d), F32),
        scratch_shapes=[pltpu.VMEM((tm, d), BF16), pltpu.VMEM((tm, d), F32)],
        compiler_params=pltpu.CompilerParams(
            dimension_semantics=("parallel",), vmem_limit_bytes=VMEM_LIMIT),
        name="ffn_mix" if mix is not None else "ffn",
    )(*args)


def _inproj_kernel(x_ref, g_ref, w_ref, b_ref, mix_ref, prw_ref, pat_ref, last_row,
                   *, tiles_per_seq):
    h = _rms(x_ref[...], g_ref[...]).astype(BF16)
    p = _dot(h, w_ref[...])
    pat_ref[...] = p[:, RWKV_COLS:] + b_ref[...]
    pr = p[:, :RWKV_COLS]
    prev_row = jnp.where(pl.program_id(0) % tiles_per_seq == 0, 0.0, last_row[0:1, :])
    row = lax.broadcasted_iota(jnp.int32, pr.shape, 0)
    p_prev = jnp.where(row == 0, prev_row, pltpu.roll(pr, 1, axis=0))
    prw_ref[...] = pr + (p_prev - pr) * mix_ref[...]
    last_row[0:1, :] = pr[pr.shape[0] - 1:, :]


def _in_proj(x, seq_len, norm, w_in, b_attn, shift_mix):
    n, d = x.shape
    cols = w_in.shape[1]
    tm = INPROJ_ROWS
    bias = b_attn.astype(F32).reshape(1, ATTN_COLS)
    row = lambda i: (i, 0)
    return pl.pallas_call(
        functools.partial(_inproj_kernel, tiles_per_seq=seq_len // tm),
        grid=(n // tm,),
        in_specs=[pl.BlockSpec((tm, d), row), _resident((1, d)), _resident((d, cols)),
                  _resident((1, ATTN_COLS)), _resident((1, RWKV_COLS))],
        out_specs=[pl.BlockSpec((tm, RWKV_COLS), row), pl.BlockSpec((tm, ATTN_COLS), row)],
        out_shape=[jax.ShapeDtypeStruct((n, RWKV_COLS), F32),
                   jax.ShapeDtypeStruct((n, ATTN_COLS), F32)],
        scratch_shapes=[pltpu.VMEM((8, RWKV_COLS), F32)],
        compiler_params=pltpu.CompilerParams(
            dimension_semantics=("arbitrary",), vmem_limit_bytes=VMEM_LIMIT),
        name="in_proj",
    )(x, norm.reshape(1, d), w_in.astype(BF16), bias, shift_mix.astype(F32).reshape(1, -1))


def _rwkv_prep_kernel(ps_ref, w0_ref, w2_ref, a0_ref, a2_ref, g2_ref, kk_ref, ka_ref, rk_ref,
                      seg_ref,
                      m_ref, n_ref, r_ref, o_ref, bonus_ref, gate_ref,
                      at_s, bt_s, kt_s, rt_s, v_s, bh_s, kh_s, rf_s, gam_s, *, n_chunks):
    C = CHUNK
    W = RWKV_WIDTH
    ps = ps_ref[...]
    r = ps[:, 0:W]
    k = ps[:, W:2 * W]
    v = ps[:, 2 * W:3 * W]
    wa = ps[:, 3 * W:3 * W + LANES]
    gl = ps[:, 3 * W + LANES:3 * W + 2 * LANES]
    z = w0_ref[...] + _dot(jnp.tanh(wa).astype(BF16), w2_ref[...])
    lw = -math.exp(-0.5) * jax.nn.sigmoid(z)
    a = jax.nn.sigmoid(a0_ref[...] + _dot(wa.astype(BF16), a2_ref[...]))
    gate_ref[...] = _dot(jax.nn.sigmoid(gl).astype(BF16), g2_ref[...]).astype(BF16)
    seg = seg_ref[...]
    kk = k * kk_ref[...]
    kk = kk * jnp.minimum(lax.rsqrt(_seg_sum(kk * kk, seg)), 1e12)
    k = k * (1.0 + (a - 1.0) * ka_ref[...])
    b = kk * a
    bonus_ref[...] = (_seg_sum(r * k * rk_ref[...], seg) * v).astype(BF16)
    v_s[...] = v.astype(BF16)
    cs = lw
    row_in_chunk = lax.broadcasted_iota(jnp.int32, lw.shape, 0) & (C - 1)
    for shift in (1, 2, 4, 8, 16, 32):
        cs = cs + jnp.where(row_in_chunk >= shift, pltpu.roll(cs, shift, axis=0), 0.0)
    for c in range(n_chunks):
        rows = slice(c * C, (c + 1) * C)
        cs_c = cs[rows]
        gam = jnp.exp(cs[(c + 1) * C - 1:(c + 1) * C, :])
        e_neg = jnp.exp(-cs_c)
        rt = r[rows] * jnp.exp(cs_c)
        bt = b[rows] * e_neg
        kt = k[rows] * e_neg
        at_s[rows, :] = (-kk[rows] * jnp.exp(cs_c - lw[rows])).astype(BF16)
        bt_s[rows, :] = bt.astype(BF16)
        kt_s[rows, :] = kt.astype(BF16)
        rt_s[rows, :] = rt.astype(BF16)
        rf_s[rows, :] = rt
        bh_s[rows, :] = bt * gam
        kh_s[rows, :] = kt * gam
        gam_s[c:c + 1, :] = gam

    rowi = lax.broadcasted_iota(jnp.int32, (2 * C, 2 * SLAB), 0)
    lane_s = lax.broadcasted_iota(jnp.int32, (2 * C, 2 * SLAB), 1) & (C - 1)
    keep_a1 = lane_s < (rowi & (C - 1)) + jnp.where(rowi < C, 0, 1)
    row_c = lax.broadcasted_iota(jnp.int32, (C, SLAB), 0)
    col_c = lax.broadcasted_iota(jnp.int32, (C, SLAB), 1) & (C - 1)
    eye_pack = jnp.where(col_c == row_c, 1.0, 0.0).astype(F32)
    same_head = ((lax.broadcasted_iota(jnp.int32, (SLAB, 2 * SLAB), 0) >> 6)
                 == _lane_block((SLAB, 2 * SLAB), HEAD_DIM))
    diag = (lax.broadcasted_iota(jnp.int32, (SLAB, SLAB), 0)
            == lax.broadcasted_iota(jnp.int32, (SLAB, SLAB), 1))
    zeros_c = jnp.zeros((C, SLAB), BF16)

    def fold(x):
        out = x[:HEAD_DIM]
        for h in range(1, HEADS_PER_SLAB):
            out = out + x[h * HEAD_DIM:(h + 1) * HEAD_DIM]
        return out

    chains = [(c, s) for c in range(n_chunks) for s in range(N_SLABS)]
    ld = lambda ref: [ref[c * C:(c + 1) * C, s * SLAB:(s + 1) * SLAB] for c, s in chains]
    at, bt, kt, rt, vv = ld(at_s), ld(bt_s), ld(kt_s), ld(rt_s), ld(v_s)
    a1 = [jnp.where(keep_a1,
                    _dot_nt(jnp.concatenate([x, y], axis=0),
                            jnp.concatenate([_block_diag(p), _block_diag(q)], axis=0)), 0.0)
          for x, y, p, q in zip(at, rt, bt, kt)]
    aab = [x[:C, :SLAB] for x in a1]
    kv = [_dot(x[:, SLAB:].astype(BF16), _block_diag(y)) for x, y in zip(a1, vv)]
    akv = [x[:C].astype(BF16) for x in kv]
    arkv = [x[C:] for x in kv]
    def sub_blocks(x, s):
        return jnp.where(((row_c // s) & 1 == 1) & (col_c // s == row_c // s - 1), x, 0.0)

    tinv = [eye_pack + sub_blocks(x, 1) for x in aab]
    for s in (2, 4, 8, 16, 32):
        tb = [t.astype(BF16) for t in tinv]
        te = [_dot(t, _block_diag(sub_blocks(x, s).astype(BF16))).astype(BF16)
              for t, x in zip(tb, aab)]
        tinv = [t + _dot(x, _block_diag(y)) for t, x, y in zip(tinv, te, tb)]
    wu = [_dot(t.astype(BF16), _block_diag(jnp.concatenate([x, y], axis=1))).astype(BF16)
          for t, x, y in zip(tinv, at, akv)]
    for i, (c, s) in enumerate(chains):
        rows = slice(c * C, (c + 1) * C)
        sl = slice(s * SLAB, (s + 1) * SLAB)
        lhs_t = jnp.concatenate([bh_s[rows, sl], kh_s[rows, sl]], axis=0)
        rhs = jnp.concatenate([wu[i], jnp.concatenate([zeros_c, vv[i]], axis=1)], axis=0)
        mn = _dot(lhs_t.T.astype(BF16), rhs)
        mn = jnp.where(same_head, mn, 0.0)
        m_ref[rows, sl] = fold(mn[:, :SLAB] + jnp.where(diag, gam_s[c:c + 1, sl], 0.0)).astype(BF16)
        n_ref[rows, sl] = fold(mn[:, SLAB:]).astype(BF16)
        ro = _dot(a1[i][C:, :SLAB].astype(BF16), _block_diag(wu[i]))
        r_ref[rows, sl] = (rf_s[rows, sl] + ro[:, :SLAB]).astype(BF16)
        o_ref[rows, sl] = (ro[:, SLAB:] + arkv[i]).astype(BF16)


def _rwkv_prep(p_shift, w0, w2, a0, a2, g2, k_k, k_a, r_k):
    n = p_shift.shape[0]
    W = RWKV_WIDTH
    tt = RWKV_ROWS
    lane_head = jnp.arange(LANES) // HEAD_DIM
    seg = (lane_head[:, None] == lane_head[None, :]).astype(BF16)
    w2p = jnp.concatenate([w2, jnp.zeros((AAA_LORA, W), w2.dtype)], axis=0).astype(BF16)
    a2p = jnp.concatenate([jnp.zeros((DECAY_LORA, W), a2.dtype), a2], axis=0).astype(BF16)
    row = lambda i: (i, 0)
    vec = lambda t: t.astype(F32).reshape(1, -1)
    bf16_out = jax.ShapeDtypeStruct((n, W), BF16)
    return pl.pallas_call(
        functools.partial(_rwkv_prep_kernel, n_chunks=tt // CHUNK),
        grid=(n // tt,),
        in_specs=[pl.BlockSpec((tt, RWKV_COLS), row),
                  _resident((1, W)), _resident((LANES, W)), _resident((1, W)),
                  _resident((LANES, W)), _resident((GATE_LORA, W)),
                  _resident((1, W)), _resident((1, W)), _resident((1, W)),
                  _resident((LANES, LANES))],
        out_specs=[pl.BlockSpec((tt, W), row)] * 6,
        out_shape=[bf16_out] * 6,
        scratch_shapes=[pltpu.VMEM((tt, W), BF16)] * 5 + [pltpu.VMEM((tt, W), F32)] * 3
                       + [pltpu.VMEM((tt // CHUNK, W), F32)],
        compiler_params=pltpu.CompilerParams(
            dimension_semantics=("parallel",), vmem_limit_bytes=VMEM_LIMIT),
        name="rwkv_prep",
    )(p_shift, vec(w0), w2p, vec(a0), a2p, g2.astype(BF16), vec(k_k), vec(k_a), vec(r_k),
      seg)


def _rwkv_scan_kernel(m_ref, n_ref, r_ref, o_ref, bonus_ref, gate_ref, lnw_ref, lnb_ref, seg_ref,
                      out_ref, state, obuf, *, n_batch, n_chunks):
    @pl.when(pl.program_id(0) == 0)
    def _():
        state[...] = jnp.zeros_like(state)

    C = CHUNK
    for c in range(n_chunks):
        rows = slice(c * C, (c + 1) * C)
        for b in range(n_batch):
            for s in range(N_SLABS):
                sl = slice(s * SLAB, (s + 1) * SLAB)
                s0 = state[b * N_SLABS + s]
                lhs = jnp.concatenate([m_ref[b, rows, sl], r_ref[b, rows, sl]], axis=0)
                res = _dot(lhs, _block_diag(s0.astype(BF16)))
                state[b * N_SLABS + s] = res[:C] + n_ref[b, rows, sl]
                obuf[b, rows, sl] = res[C:] + o_ref[b, rows, sl]

    seg_mean = seg_ref[...]
    for b in range(n_batch):
        o = obuf[b]
        d = o - _seg_sum(o, seg_mean)
        var = _seg_sum(d * d, seg_mean)
        y = d * lax.rsqrt(var + GN_EPS) * lnw_ref[...] + lnb_ref[...]
        out_ref[b] = ((y + bonus_ref[b]) * gate_ref[b]).astype(BF16)


def _rwkv_scan(mats, n_batch, seq_len, ln_w, ln_b):
    W = RWKV_WIDTH
    tt = SCAN_ROWS
    lane_head = jnp.arange(LANES) // HEAD_DIM
    seg_mean = ((lane_head[:, None] == lane_head[None, :]).astype(F32) / HEAD_DIM).astype(BF16)
    mats = [t.reshape(n_batch, seq_len, W) for t in mats]
    blk = pl.BlockSpec((n_batch, tt, W), lambda i: (0, i, 0))
    out = pl.pallas_call(
        functools.partial(_rwkv_scan_kernel, n_batch=n_batch, n_chunks=tt // CHUNK),
        grid=(seq_len // tt,),
        in_specs=[blk] * 6 + [_resident((1, W)), _resident((1, W)), _resident((LANES, LANES))],
        out_specs=blk,
        out_shape=jax.ShapeDtypeStruct((n_batch, seq_len, W), BF16),
        scratch_shapes=[pltpu.VMEM((n_batch * N_SLABS, CHUNK, SLAB), F32),
                        pltpu.VMEM((n_batch, tt, W), F32)],
        compiler_params=pltpu.CompilerParams(
            dimension_semantics=("arbitrary",), vmem_limit_bytes=VMEM_LIMIT),
        name="rwkv_scan",
    )(*mats, ln_w.astype(F32).reshape(1, W), ln_b.astype(F32).reshape(1, W), seg_mean)
    return out.reshape(n_batch * seq_len, W)


def _attn_kernel(sink_ref, q_ref, kp_ref, kc_ref, vp_ref, vc_ref, out_ref, *, n_blocks):
    BQ = WINDOW
    first = pl.program_id(1) == 0
    kx = jnp.concatenate([kp_ref[...], kc_ref[...]], axis=0)
    vx = jnp.concatenate([vp_ref[...], vc_ref[...]], axis=0)
    lo = _lane_block(kx.shape, HEAD_DIM, LANES) == 0
    kxr = pltpu.roll(kx, HEAD_DIM, axis=1)
    vxr = pltpu.roll(vx, HEAD_DIM, axis=1)
    bf = lambda t: t.astype(BF16)
    k_lo = [bf(jnp.where(lo, kx, 0.0)), bf(jnp.where(lo, kxr, 0.0))]
    k_hi = [bf(jnp.where(lo, 0.0, kxr)), bf(jnp.where(lo, 0.0, kx))]
    v_lo = [bf(jnp.where(lo, vx, 0.0)), bf(jnp.where(lo, vxr, 0.0))]
    v_hi = [bf(jnp.where(lo, 0.0, vxr)), bf(jnp.where(lo, 0.0, vx))]

    qi = lax.broadcasted_iota(jnp.int32, (BQ, 2 * BQ), 0)
    kj = lax.broadcasted_iota(jnp.int32, (BQ, 2 * BQ), 1)
    dist = qi + BQ - kj
    band = (dist >= 0) & (dist < WINDOW)
    band_first = band & (kj >= jnp.where(first, BQ, 0))
    col0 = lax.broadcasted_iota(jnp.int32, (1, 2 * BQ), 1) == 0
    log2e = math.log2(math.e)
    fill = [jnp.where(col0, sink_ref[h] * log2e, -jnp.inf) for h in range(N_Q_HEADS)]
    vrow = lax.broadcasted_iota(jnp.int32, (4 * BQ, LANES), 0) & (2 * BQ - 1)
    ones_sel = jnp.where((lax.broadcasted_iota(jnp.int32, (4 * BQ, LANES), 0) < 2 * BQ)
                         == (_lane_block((4 * BQ, LANES), HEAD_DIM, LANES) == 0),
                         1.0, 0.0).astype(BF16)
    qscale = HEAD_DIM ** -0.5 * log2e
    n_pairs = N_Q_HEADS // 2

    kv_of = [(2 * j) // (N_Q_HEADS // N_KV_HEADS) for j in range(n_pairs)]
    units = [(qb, j) for qb in range(n_blocks) for j in range(n_pairs)]
    kcat, vcat = {}, {}
    for qb in range(n_blocks):
        krows = slice(qb * BQ, (qb + 2) * BQ)
        for g in range(N_KV_HEADS):
            kcat[qb, g] = jnp.concatenate([k_lo[g][krows], k_hi[g][krows]], axis=0)
            vcat[qb, g] = jnp.concatenate(
                [jnp.where(vrow == 0, 0.0,
                           jnp.concatenate([v_lo[g][krows], v_hi[g][krows]], axis=0)),
                 ones_sel], axis=1)
    s = [_dot_nt((q_ref[qb * BQ:(qb + 1) * BQ, j * LANES:(j + 1) * LANES] * qscale).astype(BF16),
                 kcat[qb, kv_of[j]]) for qb, j in units]
    sh = [[jnp.where(band_first if qb == 0 else band,
                     x[:, hh * 2 * BQ:(hh + 1) * 2 * BQ], fill[2 * j + hh])
           for hh in range(2)] for x, (qb, j) in zip(s, units)]
    mx = [[jnp.max(x, axis=-1, keepdims=True) for x in row] for row in sh]
    e = [jnp.concatenate([jnp.exp2(x - m).astype(BF16) for x, m in zip(xr, mr)], axis=1)
         for xr, mr in zip(sh, mx)]
    pv = [_dot(x, vcat[qb, kv_of[j]]) for x, (qb, j) in zip(e, units)]
    for x, (qb, j) in zip(pv, units):
        out_ref[qb * BQ:(qb + 1) * BQ, j * LANES:(j + 1) * LANES] = (
            x[:, :LANES] / x[:, LANES:]).astype(BF16)


def _attn(p_attn, n_batch, seq_len, sinks):
    n = p_attn.shape[0]
    rows = ATTN_ROWS
    n_blocks = rows // WINDOW
    steps = seq_len // rows
    kcol = ATTN_WIDTH // LANES
    vcol = kcol + KV_WIDTH // LANES
    cur = lambda b, i: b * steps + i
    prev = lambda b, i: (b * steps + i) * n_blocks - jnp.minimum(i, 1)
    return pl.pallas_call(
        functools.partial(_attn_kernel, n_blocks=n_blocks),
        grid=(n_batch, steps),
        in_specs=[pl.BlockSpec(memory_space=pltpu.SMEM),
                  pl.BlockSpec((rows, ATTN_WIDTH), lambda b, i: (cur(b, i), 0)),
                  pl.BlockSpec((WINDOW, KV_WIDTH), lambda b, i: (prev(b, i), kcol)),
                  pl.BlockSpec((rows, KV_WIDTH), lambda b, i: (cur(b, i), kcol)),
                  pl.BlockSpec((WINDOW, KV_WIDTH), lambda b, i: (prev(b, i), vcol)),
                  pl.BlockSpec((rows, KV_WIDTH), lambda b, i: (cur(b, i), vcol))],
        out_specs=pl.BlockSpec((rows, ATTN_WIDTH), lambda b, i: (cur(b, i), 0)),
        out_shape=jax.ShapeDtypeStruct((n, ATTN_WIDTH), BF16),
        compiler_params=pltpu.CompilerParams(
            dimension_semantics=("parallel", "arbitrary"), vmem_limit_bytes=VMEM_LIMIT),
        name="swa_attn",
    )(sinks.astype(F32), p_attn, p_attn, p_attn, p_attn, p_attn)


def kernel(x, norm_ffn1, ffn1_gate, ffn1_up, ffn1_down, norm_mix, w_in, b_in_attn, rwkv_shift_mix, rwkv_w0, rwkv_w2, rwkv_a0, rwkv_a2, rwkv_g2, rwkv_k_k, rwkv_k_a, rwkv_r_k, rwkv_ln_w, rwkv_ln_b, attn_sinks, w_out, norm_ffn2, ffn2_gate, ffn2_up, ffn2_down, norm_final):
    n_batch, seq_len, d = x.shape
    depth = w_in.shape[0]
    h = x.reshape(n_batch * seq_len, d)
    for l in range(depth):
        h = _ffn(h, norm_ffn1[l], ffn1_gate[l], ffn1_up[l], ffn1_down[l])
        p_shift, p_attn = _in_proj(h, seq_len, norm_mix[l], w_in[l], b_in_attn[l],
                                   rwkv_shift_mix[l])
        mats = _rwkv_prep(p_shift, rwkv_w0[l], rwkv_w2[l], rwkv_a0[l], rwkv_a2[l], rwkv_g2[l],
                          rwkv_k_k[l], rwkv_k_a[l], rwkv_r_k[l])
        o_rwkv = _rwkv_scan(mats, n_batch, seq_len, rwkv_ln_w[l], rwkv_ln_b[l])
        o_attn = _attn(p_attn, n_batch, seq_len, attn_sinks[l])
        h = _ffn(h, norm_ffn2[l], ffn2_gate[l], ffn2_up[l], ffn2_down[l],
                 mix=(o_rwkv, o_attn, w_out[l]),
                 final_norm=norm_final if l == depth - 1 else None)
    return h.reshape(n_batch, seq_len, d)
```

```python
import functools
import math

import jax
import jax.numpy as jnp
from jax import lax
from jax.experimental import pallas as pl
from jax.experimental.pallas import tpu as pltpu

F32 = jnp.float32
BF16 = jnp.bfloat16

HEAD_DIM = 64
LANES = 128
SLAB = 128
HEADS_PER_SLAB = SLAB // HEAD_DIM
CHUNK = 64
N_RWKV_HEADS = 8
RWKV_WIDTH = N_RWKV_HEADS * HEAD_DIM
N_SLABS = RWKV_WIDTH // SLAB
DECAY_LORA = 64
AAA_LORA = 64
GATE_LORA = 128
RWKV_COLS = 3 * RWKV_WIDTH + DECAY_LORA + AAA_LORA + GATE_LORA
N_Q_HEADS = 8
N_KV_HEADS = 2
ATTN_WIDTH = N_Q_HEADS * HEAD_DIM
KV_WIDTH = N_KV_HEADS * HEAD_DIM
ATTN_COLS = ATTN_WIDTH + 2 * KV_WIDTH
WINDOW = 128
GN_EPS = 64e-5
NORM_EPS = 1e-5
VMEM_LIMIT = 56 * 1024 * 1024

FFN_ROWS = 512
FFN_COLS = 256
INPROJ_ROWS = 1024
RWKV_ROWS = 512
SCAN_ROWS = 1024
ATTN_ROWS = 1024


def _dot(a, b):
    return jnp.dot(a, b, preferred_element_type=F32)


def _dot_nt(a, b):
    return lax.dot_general(a, b, (((1,), (1,)), ((), ())), preferred_element_type=F32)


def _rms(x, g):
    ms = jnp.mean(x * x, axis=-1, keepdims=True)
    return x * lax.rsqrt(ms + NORM_EPS) * g


def _lane_block(shape, width, span=SLAB):
    lane = lax.broadcasted_iota(jnp.int32, shape, 1)
    return (lane & (span - 1)) >> (width.bit_length() - 1)


def _block_diag(x, width=HEAD_DIM):
    blk = _lane_block(x.shape, width)
    return jnp.concatenate([jnp.where(blk == g, x, 0.0) for g in range(SLAB // width)], axis=0)


def _seg_sum(x, seg):
    xb = x.astype(BF16)
    rows, groups = x.shape[0], x.shape[1] // LANES
    y = _dot(jnp.concatenate([xb[:, j * LANES:(j + 1) * LANES] for j in range(groups)], axis=0), seg)
    return jnp.concatenate([y[j * rows:(j + 1) * rows] for j in range(groups)], axis=1)


def _resident(shape):
    return pl.BlockSpec(shape, lambda *_: (0,) * len(shape), pipeline_mode=pl.Buffered(1))


def _ffn_kernel(*refs, has_mix, final_norm, nf, tf):
    refs = list(refs)
    x_ref = refs.pop(0)
    if has_mix:
        orw_ref, oat_ref, wo_ref = refs[:3]
        refs = refs[3:]
    g_ref, wg_ref, wu_ref, wd_ref = refs[:4]
    refs = refs[4:]
    if final_norm:
        gf_ref = refs.pop(0)
    out_ref, h_s, acc = refs

    x = x_ref[...]
    if has_mix:
        x = (x + _dot(orw_ref[...], wo_ref[:RWKV_WIDTH].astype(BF16))
             + _dot(oat_ref[...], wo_ref[RWKV_WIDTH:].astype(BF16)))
        out_ref[...] = x
    h_s[...] = _rms(x, g_ref[...]).astype(BF16)

    def down(f):
        cols = pl.ds(f * tf if isinstance(f, int) else pl.multiple_of(f * tf, tf), tf)
        h = h_s[...]
        gate = _dot(h, wg_ref[:, cols].astype(BF16))
        up = _dot(h, wu_ref[:, cols].astype(BF16))
        act = (gate * jax.nn.sigmoid(gate) * up).astype(BF16)
        return _dot(act, wd_ref[cols, :].astype(BF16))

    acc[...] = down(0)

    for f in range(1, nf):
        acc[...] += down(f)
    res = out_ref[...] if has_mix else x_ref[...]
    y = res + 0.5 * acc[...]
    if final_norm:
        y = _rms(y, gf_ref[...])
    out_ref[...] = y


def _ffn(x, norm, w_gate, w_up, w_down, mix=None, final_norm=None):
    n, d = x.shape
    f = w_gate.shape[1]
    tm, tf = FFN_ROWS, FFN_COLS
    nf = f // tf
    row = lambda i: (i, 0)
    wg, wu, wd = w_gate, w_up, w_down
    args = [x]
    specs = [pl.BlockSpec((tm, d), row)]
    if mix is not None:
        o_rwkv, o_attn, w_out = mix
        args += [o_rwkv, o_attn, w_out]
        specs += [pl.BlockSpec((tm, RWKV_WIDTH), row), pl.BlockSpec((tm, ATTN_WIDTH), row),
                  _resident((RWKV_WIDTH + ATTN_WIDTH, d))]
    args += [norm.reshape(1, d), wg, wu, wd]
    specs += [_resident((1, d)), _resident((d, f)), _resident((d, f)), _resident((f, d))]
    if final_norm is not None:
        args.append(final_norm.reshape(1, d))
        specs.append(_resident((1, d)))
    return pl.pallas_call(
        functools.partial(_ffn_kernel, has_mix=mix is not None,
                          final_norm=final_norm is not None, nf=nf, tf=tf),
        grid=(n // tm,),
        in_specs=specs,
        out_specs=pl.BlockSpec((tm, d), row),
        out_shape=jax.ShapeDtypeStruct((n, d), F32),
        scratch_shapes=[pltpu.VMEM((tm, d), BF16), pltpu.VMEM((tm, d), F32)],
        compiler_params=pltpu.CompilerParams(
            dimension_semantics=("parallel",), vmem_limit_bytes=VMEM_LIMIT),
        name="ffn_mix" if mix is not None else "ffn",
    )(*args)


def _inproj_kernel(x_ref, g_ref, w_ref, b_ref, mix_ref, prw_ref, pat_ref, last_row,
                   *, tiles_per_seq):
    h = _rms(x_ref[...], g_ref[...]).astype(BF16)
    p = _dot(h, w_ref[...])
    pat_ref[...] = p[:, RWKV_COLS:] + b_ref[...]
    pr = p[:, :RWKV_COLS]
    prev_row = jnp.where(pl.program_id(0) % tiles_per_seq == 0, 0.0, last_row[0:1, :])
    row = lax.broadcasted_iota(jnp.int32, pr.shape, 0)
    p_prev = jnp.where(row == 0, prev_row, pltpu.roll(pr, 1, axis=0))
    prw_ref[...] = pr + (p_prev - pr) * mix_ref[...]
    last_row[0:1, :] = pr[pr.shape[0] - 1:, :]


def _in_proj(x, seq_len, norm, w_in, b_attn, shift_mix):
    n, d = x.shape
    cols = w_in.shape[1]
    tm = INPROJ_ROWS
    bias = b_attn.astype(F32).reshape(1, ATTN_COLS)
    row = lambda i: (i, 0)
    return pl.pallas_call(
        functools.partial(_inproj_kernel, tiles_per_seq=seq_len // tm),
        grid=(n // tm,),
        in_specs=[pl.BlockSpec((tm, d), row), _resident((1, d)), _resident((d, cols)),
                  _resident((1, ATTN_COLS)), _resident((1, RWKV_COLS))],
        out_specs=[pl.BlockSpec((tm, RWKV_COLS), row), pl.BlockSpec((tm, ATTN_COLS), row)],
        out_shape=[jax.ShapeDtypeStruct((n, RWKV_COLS), F32),
                   jax.ShapeDtypeStruct((n, ATTN_COLS), F32)],
        scratch_shapes=[pltpu.VMEM((8, RWKV_COLS), F32)],
        compiler_params=pltpu.CompilerParams(
            dimension_semantics=("arbitrary",), vmem_limit_bytes=VMEM_LIMIT),
        name="in_proj",
    )(x, norm.reshape(1, d), w_in.astype(BF16), bias, shift_mix.astype(F32).reshape(1, -1))


def _rwkv_prep_kernel(ps_ref, w0_ref, w2_ref, a0_ref, a2_ref, g2_ref, kk_ref, ka_ref, rk_ref,
                      seg_ref,
                      m_ref, n_ref, r_ref, o_ref, bonus_ref, gate_ref, gam_ref,
                      at_s, bt_s, kt_s, rt_s, v_s, bh_s, kh_s, rf_s, *, n_chunks):
    C = CHUNK
    W = RWKV_WIDTH
    ps = ps_ref[...]
    r = ps[:, 0:W]
    k = ps[:, W:2 * W]
    v = ps[:, 2 * W:3 * W]
    wa = ps[:, 3 * W:3 * W + LANES]
    gl = ps[:, 3 * W + LANES:3 * W + 2 * LANES]
    z = w0_ref[...] + _dot(jnp.tanh(wa).astype(BF16), w2_ref[...])
    lw = -math.exp(-0.5) * jax.nn.sigmoid(z)
    a = jax.nn.sigmoid(a0_ref[...] + _dot(wa.astype(BF16), a2_ref[...]))
    gate_ref[...] = _dot(jax.nn.sigmoid(gl).astype(BF16), g2_ref[...]).astype(BF16)
    seg = seg_ref[...]
    kk = k * kk_ref[...]
    kk = kk * jnp.minimum(lax.rsqrt(_seg_sum(kk * kk, seg)), 1e12)
    k = k * (1.0 + (a - 1.0) * ka_ref[...])
    b = kk * a
    bonus_ref[...] = (_seg_sum(r * k * rk_ref[...], seg) * v).astype(BF16)
    v_s[...] = v.astype(BF16)
    cs = lw
    row_in_chunk = lax.broadcasted_iota(jnp.int32, lw.shape, 0) & (C - 1)
    for shift in (1, 2, 4, 8, 16, 32):
        cs = cs + jnp.where(row_in_chunk >= shift, pltpu.roll(cs, shift, axis=0), 0.0)
    for c in range(n_chunks):
        rows = slice(c * C, (c + 1) * C)
        cs_c = cs[rows]
        gam = jnp.exp(cs[(c + 1) * C - 1:(c + 1) * C, :])
        e_neg = jnp.exp(-cs_c)
        rt = r[rows] * jnp.exp(cs_c)
        bt = b[rows] * e_neg
        kt = k[rows] * e_neg
        at_s[rows, :] = (-kk[rows] * jnp.exp(cs_c - lw[rows])).astype(BF16)
        bt_s[rows, :] = bt.astype(BF16)
        kt_s[rows, :] = kt.astype(BF16)
        rt_s[rows, :] = rt.astype(BF16)
        rf_s[rows, :] = rt
        bh_s[rows, :] = bt * gam
        kh_s[rows, :] = kt * gam
        gam_ref[c:c + 1, :] = gam

    rowi = lax.broadcasted_iota(jnp.int32, (2 * C, 2 * SLAB), 0)
    lane_s = lax.broadcasted_iota(jnp.int32, (2 * C, 2 * SLAB), 1) & (C - 1)
    keep_a1 = lane_s < (rowi & (C - 1)) + jnp.where(rowi < C, 0, 1)
    row_c = lax.broadcasted_iota(jnp.int32, (C, SLAB), 0)
    col_c = lax.broadcasted_iota(jnp.int32, (C, SLAB), 1) & (C - 1)
    eye_pack = jnp.where(col_c == row_c, 1.0, 0.0).astype(F32)
    same_head = ((lax.broadcasted_iota(jnp.int32, (SLAB, 2 * SLAB), 0) >> 6)
                 == _lane_block((SLAB, 2 * SLAB), HEAD_DIM))
    zeros_c = jnp.zeros((C, SLAB), BF16)

    def fold(x):
        out = x[:HEAD_DIM]
        for h in range(1, HEADS_PER_SLAB):
            out = out + x[h * HEAD_DIM:(h + 1) * HEAD_DIM]
        return out

    chains = [(c, s) for c in range(n_chunks) for s in range(N_SLABS)]
    ld = lambda ref: [ref[c * C:(c + 1) * C, s * SLAB:(s + 1) * SLAB] for c, s in chains]
    at, bt, kt, rt, vv = ld(at_s), ld(bt_s), ld(kt_s), ld(rt_s), ld(v_s)
    a1 = [jnp.where(keep_a1,
                    _dot_nt(jnp.concatenate([x, y], axis=0),
                            jnp.concatenate([_block_diag(p), _block_diag(q)], axis=0)), 0.0)
          for x, y, p, q in zip(at, rt, bt, kt)]
    aab = [x[:C, :SLAB] for x in a1]
    kv = [_dot(x[:, SLAB:].astype(BF16), _block_diag(y)) for x, y in zip(a1, vv)]
    akv = [x[:C].astype(BF16) for x in kv]
    arkv = [x[C:] for x in kv]
    def sub_blocks(x, s):
        return jnp.where(((row_c // s) & 1 == 1) & (col_c // s == row_c // s - 1), x, 0.0)

    tinv = [eye_pack + sub_blocks(x, 1) for x in aab]
    for s in (2, 4, 8, 16, 32):
        tb = [t.astype(BF16) for t in tinv]
        te = [_dot(t, _block_diag(sub_blocks(x, s).astype(BF16))).astype(BF16)
              for t, x in zip(tb, aab)]
        tinv = [t + _dot(x, _block_diag(y)) for t, x, y in zip(tinv, te, tb)]
    wu = [_dot(t.astype(BF16), _block_diag(jnp.concatenate([x, y], axis=1))).astype(BF16)
          for t, x, y in zip(tinv, at, akv)]
    for i, (c, s) in enumerate(chains):
        rows = slice(c * C, (c + 1) * C)
        sl = slice(s * SLAB, (s + 1) * SLAB)
        lhs_t = jnp.concatenate([bh_s[rows, sl], kh_s[rows, sl]], axis=0)
        rhs = jnp.concatenate([wu[i], jnp.concatenate([zeros_c, vv[i]], axis=1)], axis=0)
        mn = _dot(lhs_t.T.astype(BF16), rhs)
        mn = jnp.where(same_head, mn, 0.0)
        m_ref[rows, sl] = fold(mn[:, :SLAB]).astype(BF16)
        n_ref[rows, sl] = fold(mn[:, SLAB:]).astype(BF16)
        ro = _dot(a1[i][C:, :SLAB].astype(BF16), _block_diag(wu[i]))
        r_ref[rows, sl] = (rf_s[rows, sl] + ro[:, :SLAB]).astype(BF16)
        o_ref[rows, sl] = (ro[:, SLAB:] + arkv[i]).astype(BF16)


def _rwkv_prep(p_shift, w0, w2, a0, a2, g2, k_k, k_a, r_k):
    n = p_shift.shape[0]
    W = RWKV_WIDTH
    tt = RWKV_ROWS
    lane_head = jnp.arange(LANES) // HEAD_DIM
    seg = (lane_head[:, None] == lane_head[None, :]).astype(BF16)
    w2p = jnp.concatenate([w2, jnp.zeros((AAA_LORA, W), w2.dtype)], axis=0).astype(BF16)
    a2p = jnp.concatenate([jnp.zeros((DECAY_LORA, W), a2.dtype), a2], axis=0).astype(BF16)
    row = lambda i: (i, 0)
    vec = lambda t: t.astype(F32).reshape(1, -1)
    bf16_out = jax.ShapeDtypeStruct((n, W), BF16)
    return pl.pallas_call(
        functools.partial(_rwkv_prep_kernel, n_chunks=tt // CHUNK),
        grid=(n // tt,),
        in_specs=[pl.BlockSpec((tt, RWKV_COLS), row),
                  _resident((1, W)), _resident((LANES, W)), _resident((1, W)),
                  _resident((LANES, W)), _resident((GATE_LORA, W)),
                  _resident((1, W)), _resident((1, W)), _resident((1, W)),
                  _resident((LANES, LANES))],
        out_specs=[pl.BlockSpec((tt, W), row)] * 6 + [pl.BlockSpec((tt // CHUNK, W), row)],
        out_shape=[bf16_out] * 6 + [jax.ShapeDtypeStruct((n // CHUNK, W), F32)],
        scratch_shapes=[pltpu.VMEM((tt, W), BF16)] * 5 + [pltpu.VMEM((tt, W), F32)] * 3,
        compiler_params=pltpu.CompilerParams(
            dimension_semantics=("parallel",), vmem_limit_bytes=VMEM_LIMIT),
        name="rwkv_prep",
    )(p_shift, vec(w0), w2p, vec(a0), a2p, g2.astype(BF16), vec(k_k), vec(k_a), vec(r_k),
      seg)


def _rwkv_scan_kernel(m_ref, n_ref, r_ref, o_ref, bonus_ref, gate_ref, gam_ref, lnw_ref, lnb_ref,
                      seg_ref, out_ref, state, obuf, *, n_batch, n_chunks):
    @pl.when(pl.program_id(0) == 0)
    def _():
        state[...] = jnp.zeros_like(state)

    C = CHUNK
    first_head = _lane_block((C, SLAB), HEAD_DIM) == 0
    for c in range(n_chunks):
        rows = slice(c * C, (c + 1) * C)
        for b in range(n_batch):
            for s in range(N_SLABS):
                sl = slice(s * SLAB, (s + 1) * SLAB)
                s0 = state[b * N_SLABS + s]
                lhs = jnp.concatenate([m_ref[b, rows, sl], r_ref[b, rows, sl]], axis=0)
                res = _dot(lhs, _block_diag(s0.astype(BF16)))
                g_col = jnp.broadcast_to(gam_ref[b, c:c + 1, sl], (SLAB, SLAB)).T
                decay = jnp.where(first_head, g_col[:HEAD_DIM], g_col[HEAD_DIM:])
                state[b * N_SLABS + s] = res[:C] + decay * s0 + n_ref[b, rows, sl]
                obuf[b, rows, sl] = res[C:] + o_ref[b, rows, sl]

    seg_mean = seg_ref[...]
    for b in range(n_batch):
        o = obuf[b]
        d = o - _seg_sum(o, seg_mean)
        var = _seg_sum(d * d, seg_mean)
        y = d * lax.rsqrt(var + GN_EPS) * lnw_ref[...] + lnb_ref[...]
        out_ref[b] = ((y + bonus_ref[b]) * gate_ref[b]).astype(BF16)


def _rwkv_scan(mats, n_batch, seq_len, ln_w, ln_b):
    W = RWKV_WIDTH
    tt = SCAN_ROWS
    lane_head = jnp.arange(LANES) // HEAD_DIM
    seg_mean = ((lane_head[:, None] == lane_head[None, :]).astype(F32) / HEAD_DIM).astype(BF16)
    *mats, gam = mats
    mats = [t.reshape(n_batch, seq_len, W) for t in mats]
    mats.append(gam.reshape(n_batch, seq_len // CHUNK, W))
    blk = pl.BlockSpec((n_batch, tt, W), lambda i: (0, i, 0))
    gam_blk = pl.BlockSpec((n_batch, tt // CHUNK, W), lambda i: (0, i, 0))
    out = pl.pallas_call(
        functools.partial(_rwkv_scan_kernel, n_batch=n_batch, n_chunks=tt // CHUNK),
        grid=(seq_len // tt,),
        in_specs=[blk] * 6 + [gam_blk, _resident((1, W)), _resident((1, W)),
                              _resident((LANES, LANES))],
        out_specs=blk,
        out_shape=jax.ShapeDtypeStruct((n_batch, seq_len, W), BF16),
        scratch_shapes=[pltpu.VMEM((n_batch * N_SLABS, CHUNK, SLAB), F32),
                        pltpu.VMEM((n_batch, tt, W), F32)],
        compiler_params=pltpu.CompilerParams(
            dimension_semantics=("arbitrary",), vmem_limit_bytes=VMEM_LIMIT),
        name="rwkv_scan",
    )(*mats, ln_w.astype(F32).reshape(1, W), ln_b.astype(F32).reshape(1, W), seg_mean)
    return out.reshape(n_batch * seq_len, W)


def _attn_kernel(sink_ref, q_ref, kp_ref, kc_ref, vp_ref, vc_ref, out_ref, *, n_blocks):
    BQ = WINDOW
    first = pl.program_id(1) == 0
    kx = jnp.concatenate([kp_ref[...], kc_ref[...]], axis=0)
    vx = jnp.concatenate([vp_ref[...], vc_ref[...]], axis=0)
    lo = _lane_block(kx.shape, HEAD_DIM, LANES) == 0
    kxr = pltpu.roll(kx, HEAD_DIM, axis=1)
    vxr = pltpu.roll(vx, HEAD_DIM, axis=1)
    bf = lambda t: t.astype(BF16)
    k_lo = [bf(jnp.where(lo, kx, 0.0)), bf(jnp.where(lo, kxr, 0.0))]
    k_hi = [bf(jnp.where(lo, 0.0, kxr)), bf(jnp.where(lo, 0.0, kx))]
    v_lo = [bf(jnp.where(lo, vx, 0.0)), bf(jnp.where(lo, vxr, 0.0))]
    v_hi = [bf(jnp.where(lo, 0.0, vxr)), bf(jnp.where(lo, 0.0, vx))]

    qi = lax.broadcasted_iota(jnp.int32, (BQ, 2 * BQ), 0)
    kj = lax.broadcasted_iota(jnp.int32, (BQ, 2 * BQ), 1)
    dist = qi + BQ - kj
    band = (dist >= 0) & (dist < WINDOW)
    band_first = band & (kj >= jnp.where(first, BQ, 0))
    col0 = lax.broadcasted_iota(jnp.int32, (1, 2 * BQ), 1) == 0
    log2e = math.log2(math.e)
    fill = [jnp.where(col0, sink_ref[h] * log2e, -jnp.inf) for h in range(N_Q_HEADS)]
    vrow = lax.broadcasted_iota(jnp.int32, (4 * BQ, LANES), 0) & (2 * BQ - 1)
    ones_sel = jnp.where((lax.broadcasted_iota(jnp.int32, (4 * BQ, LANES), 0) < 2 * BQ)
                         == (_lane_block((4 * BQ, LANES), HEAD_DIM, LANES) == 0),
                         1.0, 0.0).astype(BF16)
    qscale = HEAD_DIM ** -0.5 * log2e
    n_pairs = N_Q_HEADS // 2

    kv_of = [(2 * j) // (N_Q_HEADS // N_KV_HEADS) for j in range(n_pairs)]
    units = [(qb, j) for qb in range(n_blocks) for j in range(n_pairs)]
    kcat, vcat = {}, {}
    for qb in range(n_blocks):
        krows = slice(qb * BQ, (qb + 2) * BQ)
        for g in range(N_KV_HEADS):
            kcat[qb, g] = jnp.concatenate([k_lo[g][krows], k_hi[g][krows]], axis=0)
            vcat[qb, g] = jnp.concatenate(
                [jnp.where(vrow == 0, 0.0,
                           jnp.concatenate([v_lo[g][krows], v_hi[g][krows]], axis=0)),
                 ones_sel], axis=1)
    s = [_dot_nt((q_ref[qb * BQ:(qb + 1) * BQ, j * LANES:(j + 1) * LANES] * qscale).astype(BF16),
                 kcat[qb, kv_of[j]]) for qb, j in units]
    sh = [[jnp.where(band_first if qb == 0 else band,
                     x[:, hh * 2 * BQ:(hh + 1) * 2 * BQ], fill[2 * j + hh])
           for hh in range(2)] for x, (qb, j) in zip(s, units)]
    mx = [[jnp.max(x, axis=-1, keepdims=True) for x in row] for row in sh]
    e = [jnp.concatenate([jnp.exp2(x - m).astype(BF16) for x, m in zip(xr, mr)], axis=1)
         for xr, mr in zip(sh, mx)]
    pv = [_dot(x, vcat[qb, kv_of[j]]) for x, (qb, j) in zip(e, units)]
    for x, (qb, j) in zip(pv, units):
        out_ref[qb * BQ:(qb + 1) * BQ, j * LANES:(j + 1) * LANES] = (
            x[:, :LANES] / x[:, LANES:]).astype(BF16)


def _attn(p_attn, n_batch, seq_len, sinks):
    n = p_attn.shape[0]
    rows = ATTN_ROWS
    n_blocks = rows // WINDOW
    steps = seq_len // rows
    kcol = ATTN_WIDTH // LANES
    vcol = kcol + KV_WIDTH // LANES
    cur = lambda b, i: b * steps + i
    prev = lambda b, i: (b * steps + i) * n_blocks - jnp.minimum(i, 1)
    return pl.pallas_call(
        functools.partial(_attn_kernel, n_blocks=n_blocks),
        grid=(n_batch, steps),
        in_specs=[pl.BlockSpec(memory_space=pltpu.SMEM),
                  pl.BlockSpec((rows, ATTN_WIDTH), lambda b, i: (cur(b, i), 0)),
                  pl.BlockSpec((WINDOW, KV_WIDTH), lambda b, i: (prev(b, i), kcol)),
                  pl.BlockSpec((rows, KV_WIDTH), lambda b, i: (cur(b, i), kcol)),
                  pl.BlockSpec((WINDOW, KV_WIDTH), lambda b, i: (prev(b, i), vcol)),
                  pl.BlockSpec((rows, KV_WIDTH), lambda b, i: (cur(b, i), vcol))],
        out_specs=pl.BlockSpec((rows, ATTN_WIDTH), lambda b, i: (cur(b, i), 0)),
        out_shape=jax.ShapeDtypeStruct((n, ATTN_WIDTH), BF16),
        compiler_params=pltpu.CompilerParams(
            dimension_semantics=("parallel", "arbitrary"), vmem_limit_bytes=VMEM_LIMIT),
        name="swa_attn",
    )(sinks.astype(F32), p_attn, p_attn, p_attn, p_attn, p_attn)


def kernel(x, norm_ffn1, ffn1_gate, ffn1_up, ffn1_down, norm_mix, w_in, b_in_attn, rwkv_shift_mix, rwkv_w0, rwkv_w2, rwkv_a0, rwkv_a2, rwkv_g2, rwkv_k_k, rwkv_k_a, rwkv_r_k, rwkv_ln_w, rwkv_ln_b, attn_sinks, w_out, norm_ffn2, ffn2_gate, ffn2_up, ffn2_down, norm_final):
    n_batch, seq_len, d = x.shape
    depth = w_in.shape[0]
    h = x.reshape(n_batch * seq_len, d)
    for l in range(depth):
        h = _ffn(h, norm_ffn1[l], ffn1_gate[l], ffn1_up[l], ffn1_down[l])
        p_shift, p_attn = _in_proj(h, seq_len, norm_mix[l], w_in[l], b_in_attn[l],
                                   rwkv_shift_mix[l])
        mats = _rwkv_prep(p_shift, rwkv_w0[l], rwkv_w2[l], rwkv_a0[l], rwkv_a2[l], rwkv_g2[l],
                          rwkv_k_k[l], rwkv_k_a[l], rwkv_r_k[l])
        o_rwkv = _rwkv_scan(mats, n_batch, seq_len, rwkv_ln_w[l], rwkv_ln_b[l])
        o_attn = _attn(p_attn, n_batch, seq_len, attn_sinks[l])
        h = _ffn(h, norm_ffn2[l], ffn2_gate[l], ffn2_up[l], ffn2_down[l],
                 mix=(o_rwkv, o_attn, w_out[l]),
                 final_norm=norm_final if l == depth - 1 else None)
    return h.reshape(n_batch, seq_len, d)
```

```python
import functools
import math

import jax
import jax.numpy as jnp
from jax import lax
from jax.experimental import pallas as pl
from jax.experimental.pallas import tpu as pltpu

F32 = jnp.float32
BF16 = jnp.bfloat16

HEAD_DIM = 64
LANES = 128
SLAB = 128
HEADS_PER_SLAB = SLAB // HEAD_DIM
CHUNK = 64
N_RWKV_HEADS = 8
RWKV_WIDTH = N_RWKV_HEADS * HEAD_DIM
N_SLABS = RWKV_WIDTH // SLAB
DECAY_LORA = 64
AAA_LORA = 64
GATE_LORA = 128
RWKV_COLS = 3 * RWKV_WIDTH + DECAY_LORA + AAA_LORA + GATE_LORA
N_Q_HEADS = 8
N_KV_HEADS = 2
ATTN_WIDTH = N_Q_HEADS * HEAD_DIM
KV_WIDTH = N_KV_HEADS * HEAD_DIM
ATTN_COLS = ATTN_WIDTH + 2 * KV_WIDTH
WINDOW = 128
GN_EPS = 64e-5
NORM_EPS = 1e-5
VMEM_LIMIT = 56 * 1024 * 1024

FFN_ROWS = 512
FFN_COLS = 256
INPROJ_ROWS = 1024
RWKV_ROWS = 512
SCAN_ROWS = 1024
ATTN_ROWS = 1024


def _dot(a, b):
    return jnp.dot(a, b, preferred_element_type=F32)


def _dot_nt(a, b):
    return lax.dot_general(a, b, (((1,), (1,)), ((), ())), preferred_element_type=F32)


def _rms(x, g):
    ms = jnp.mean(x * x, axis=-1, keepdims=True)
    return x * lax.rsqrt(ms + NORM_EPS) * g


def _lane_block(shape, width, span=SLAB):
    lane = lax.broadcasted_iota(jnp.int32, shape, 1)
    return (lane & (span - 1)) >> (width.bit_length() - 1)


def _block_diag(x, width=HEAD_DIM):
    blk = _lane_block(x.shape, width)
    return jnp.concatenate([jnp.where(blk == g, x, 0.0) for g in range(SLAB // width)], axis=0)


def _split2(x):
    hi = x.astype(BF16)
    return hi, (x - hi.astype(F32)).astype(BF16)


def _seg_sum(x, seg):
    xb = x.astype(BF16)
    rows, groups = x.shape[0], x.shape[1] // LANES
    y = _dot(jnp.concatenate([xb[:, j * LANES:(j + 1) * LANES] for j in range(groups)], axis=0), seg)
    return jnp.concatenate([y[j * rows:(j + 1) * rows] for j in range(groups)], axis=1)


def _resident(shape):
    return pl.BlockSpec(shape, lambda *_: (0,) * len(shape), pipeline_mode=pl.Buffered(1))


def _ffn_kernel(*refs, has_mix, final_norm, nf, tf):
    refs = list(refs)
    x_ref = refs.pop(0)
    if has_mix:
        orw_ref, oat_ref, wo_ref = refs[:3]
        refs = refs[3:]
    g_ref, wg_ref, wu_ref, wd_ref = refs[:4]
    refs = refs[4:]
    if final_norm:
        gf_ref = refs.pop(0)
    out_ref, h_s, acc = refs

    x = x_ref[...]
    if has_mix:
        x = (x + _dot(orw_ref[...], wo_ref[:RWKV_WIDTH].astype(BF16))
             + _dot(oat_ref[...], wo_ref[RWKV_WIDTH:].astype(BF16)))
        out_ref[...] = x
    h_s[...] = _rms(x, g_ref[...]).astype(BF16)

    def down(f):
        cols = pl.ds(f * tf if isinstance(f, int) else pl.multiple_of(f * tf, tf), tf)
        h = h_s[...]
        gate = _dot(h, wg_ref[:, cols].astype(BF16))
        up = _dot(h, wu_ref[:, cols].astype(BF16))
        act = (gate * jax.nn.sigmoid(gate) * up).astype(BF16)
        return _dot(act, wd_ref[cols, :].astype(BF16))

    acc[...] = down(0)

    for f in range(1, nf):
        acc[...] += down(f)
    res = out_ref[...] if has_mix else x_ref[...]
    y = res + 0.5 * acc[...]
    if final_norm:
        y = _rms(y, gf_ref[...])
    out_ref[...] = y


def _ffn(x, norm, w_gate, w_up, w_down, mix=None, final_norm=None):
    n, d = x.shape
    f = w_gate.shape[1]
    tm, tf = FFN_ROWS, FFN_COLS
    nf = f // tf
    row = lambda i: (i, 0)
    wg, wu, wd = w_gate, w_up, w_down
    args = [x]
    specs = [pl.BlockSpec((tm, d), row)]
    if mix is not None:
        o_rwkv, o_attn, w_out = mix
        args += [o_rwkv, o_attn, w_out]
        specs += [pl.BlockSpec((tm, RWKV_WIDTH), row), pl.BlockSpec((tm, ATTN_WIDTH), row),
                  _resident((RWKV_WIDTH + ATTN_WIDTH, d))]
    args += [norm.reshape(1, d), wg, wu, wd]
    specs += [_resident((1, d)), _resident((d, f)), _resident((d, f)), _resident((f, d))]
    if final_norm is not None:
        args.append(final_norm.reshape(1, d))
        specs.append(_resident((1, d)))
    return pl.pallas_call(
        functools.partial(_ffn_kernel, has_mix=mix is not None,
                          final_norm=final_norm is not None, nf=nf, tf=tf),
        grid=(n // tm,),
        in_specs=specs,
        out_specs=pl.BlockSpec((tm, d), row),
        out_shape=jax.ShapeDtypeStruct((n, d), F32),
        scratch_shapes=[pltpu.VMEM((tm, d), BF16), pltpu.VMEM((tm, d), F32)],
        compiler_params=pltpu.CompilerParams(
            dimension_semantics=("parallel",), vmem_limit_bytes=VMEM_LIMIT),
        name="ffn_mix" if mix is not None else "ffn",
    )(*args)


def _inproj_kernel(x_ref, g_ref, w_ref, b_ref, mix_ref, prw_ref, pat_ref, last_row,
                   *, tiles_per_seq):
    h = _rms(x_ref[...], g_ref[...]).astype(BF16)
    p = _dot(h, w_ref[...])
    pat_ref[...] = p[:, RWKV_COLS:] + b_ref[...]
    pr = p[:, :RWKV_COLS]
    prev_row = jnp.where(pl.program_id(0) % tiles_per_seq == 0, 0.0, last_row[0:1, :])
    row = lax.broadcasted_iota(jnp.int32, pr.shape, 0)
    p_prev = jnp.where(row == 0, prev_row, pltpu.roll(pr, 1, axis=0))
    prw_ref[...] = pr + (p_prev - pr) * mix_ref[...]
    last_row[0:1, :] = pr[pr.shape[0] - 1:, :]


def _in_proj(x, seq_len, norm, w_in, b_attn, shift_mix):
    n, d = x.shape
    cols = w_in.shape[1]
    tm = INPROJ_ROWS
    bias = b_attn.astype(F32).reshape(1, ATTN_COLS)
    row = lambda i: (i, 0)
    return pl.pallas_call(
        functools.partial(_inproj_kernel, tiles_per_seq=seq_len // tm),
        grid=(n // tm,),
        in_specs=[pl.BlockSpec((tm, d), row), _resident((1, d)), _resident((d, cols)),
                  _resident((1, ATTN_COLS)), _resident((1, RWKV_COLS))],
        out_specs=[pl.BlockSpec((tm, RWKV_COLS), row), pl.BlockSpec((tm, ATTN_COLS), row)],
        out_shape=[jax.ShapeDtypeStruct((n, RWKV_COLS), F32),
                   jax.ShapeDtypeStruct((n, ATTN_COLS), F32)],
        scratch_shapes=[pltpu.VMEM((8, RWKV_COLS), F32)],
        compiler_params=pltpu.CompilerParams(
            dimension_semantics=("arbitrary",), vmem_limit_bytes=VMEM_LIMIT),
        name="in_proj",
    )(x, norm.reshape(1, d), w_in.astype(BF16), bias, shift_mix.astype(F32).reshape(1, -1))


def _rwkv_prep_kernel(ps_ref, w0_ref, w2_ref, a0_ref, a2_ref, g2_ref, kk_ref, ka_ref, rk_ref,
                      seg_ref,
                      m_ref, n_ref, r_ref, o_ref, bonus_ref, gate_ref, gam_ref,
                      at_s, bt_s, kt_s, rt_s, v_s, bh_s, kh_s, rf_s, *, n_chunks):
    C = CHUNK
    W = RWKV_WIDTH
    ps = ps_ref[...]
    r = ps[:, 0:W]
    k = ps[:, W:2 * W]
    v = ps[:, 2 * W:3 * W]
    wa = ps[:, 3 * W:3 * W + LANES]
    gl = ps[:, 3 * W + LANES:3 * W + 2 * LANES]
    z = w0_ref[...] + _dot(jnp.tanh(wa).astype(BF16), w2_ref[...])
    lw = -math.exp(-0.5) * jax.nn.sigmoid(z)
    a = jax.nn.sigmoid(a0_ref[...] + _dot(wa.astype(BF16), a2_ref[...]))
    gate_ref[...] = _dot(jax.nn.sigmoid(gl).astype(BF16), g2_ref[...]).astype(BF16)
    seg = seg_ref[...]
    kk = k * kk_ref[...]
    kk = kk * jnp.minimum(lax.rsqrt(_seg_sum(kk * kk, seg)), 1e12)
    k = k * (1.0 + (a - 1.0) * ka_ref[...])
    b = kk * a
    bonus_ref[...] = (_seg_sum(r * k * rk_ref[...], seg) * v).astype(BF16)
    v_s[...] = v.astype(BF16)
    cs = lw
    row_in_chunk = lax.broadcasted_iota(jnp.int32, lw.shape, 0) & (C - 1)
    for shift in (1, 2, 4, 8, 16, 32):
        cs = cs + jnp.where(row_in_chunk >= shift, pltpu.roll(cs, shift, axis=0), 0.0)
    for c in range(n_chunks):
        rows = slice(c * C, (c + 1) * C)
        cs_c = cs[rows]
        gam = jnp.exp(cs[(c + 1) * C - 1:(c + 1) * C, :])
        e_neg = jnp.exp(-cs_c)
        rt = r[rows] * jnp.exp(cs_c)
        bt = b[rows] * e_neg
        kt = k[rows] * e_neg
        at_s[rows, :] = (-kk[rows] * jnp.exp(cs_c - lw[rows])).astype(BF16)
        bt_s[rows, :] = bt.astype(BF16)
        kt_s[rows, :] = kt.astype(BF16)
        rt_s[rows, :] = rt.astype(BF16)
        rf_s[rows, :] = rt
        bh_s[rows, :] = bt * gam
        kh_s[rows, :] = kt * gam
        gam_ref[c:c + 1, :] = gam

    rowi = lax.broadcasted_iota(jnp.int32, (2 * C, 2 * SLAB), 0)
    lane_s = lax.broadcasted_iota(jnp.int32, (2 * C, 2 * SLAB), 1) & (C - 1)
    keep_a1 = lane_s < (rowi & (C - 1)) + jnp.where(rowi < C, 0, 1)
    row_c = lax.broadcasted_iota(jnp.int32, (C, SLAB), 0)
    col_c = lax.broadcasted_iota(jnp.int32, (C, SLAB), 1) & (C - 1)
    eye_pack = jnp.where(col_c == row_c, 1.0, 0.0).astype(F32)
    same_head = ((lax.broadcasted_iota(jnp.int32, (SLAB, 2 * SLAB), 0) >> 6)
                 == _lane_block((SLAB, 2 * SLAB), HEAD_DIM))
    zeros_c = jnp.zeros((C, SLAB), BF16)

    def fold(x):
        out = x[:HEAD_DIM]
        for h in range(1, HEADS_PER_SLAB):
            out = out + x[h * HEAD_DIM:(h + 1) * HEAD_DIM]
        return out

    chains = [(c, s) for c in range(n_chunks) for s in range(N_SLABS)]
    ld = lambda ref: [ref[c * C:(c + 1) * C, s * SLAB:(s + 1) * SLAB] for c, s in chains]
    at, bt, kt, rt, vv = ld(at_s), ld(bt_s), ld(kt_s), ld(rt_s), ld(v_s)
    a1 = [jnp.where(keep_a1,
                    _dot_nt(jnp.concatenate([x, y], axis=0),
                            jnp.concatenate([_block_diag(p), _block_diag(q)], axis=0)), 0.0)
          for x, y, p, q in zip(at, rt, bt, kt)]
    aab = [x[:C, :SLAB] for x in a1]
    kv = [_dot(x[:, SLAB:].astype(BF16), _block_diag(y)) for x, y in zip(a1, vv)]
    akv = [_split2(x[:C]) for x in kv]
    arkv = [x[C:] for x in kv]
    def sub_blocks(x, s):
        return jnp.where(((row_c // s) & 1 == 1) & (col_c // s == row_c // s - 1), x, 0.0)

    tinv = [eye_pack + sub_blocks(x, 1) for x in aab]
    for s in (2, 4, 8, 16, 32):
        tb = [t.astype(BF16) for t in tinv]
        te = [_dot(t, _block_diag(sub_blocks(x, s).astype(BF16))).astype(BF16)
              for t, x in zip(tb, aab)]
        tinv = [t + _dot(x, _block_diag(y)) for t, x, y in zip(tinv, te, tb)]
    tb = [t.astype(BF16) for t in tinv]
    wu = [_dot(t, _block_diag(jnp.concatenate([x, y[0]], axis=1)))
          for t, x, y in zip(tb, at, akv)]
    wu = [_split2(jnp.concatenate([x[:, :SLAB], x[:, SLAB:] + _dot(t, _block_diag(y[1]))], axis=1))
          for x, t, y in zip(wu, tb, akv)]
    for i, (c, s) in enumerate(chains):
        rows = slice(c * C, (c + 1) * C)
        sl = slice(s * SLAB, (s + 1) * SLAB)
        lhs_t = jnp.concatenate([bh_s[rows, sl], kh_s[rows, sl]], axis=0)
        wu_hi, wu_lo = wu[i]
        rhs = jnp.concatenate([wu_hi, jnp.concatenate([zeros_c, vv[i]], axis=1)], axis=0)
        mn = _dot(lhs_t.T.astype(BF16), rhs)
        mn = jnp.where(same_head, mn, 0.0)
        m_ref[rows, sl] = fold(mn[:, :SLAB]).astype(BF16)
        n_ref[rows, sl] = fold(mn[:, SLAB:]).astype(BF16)
        arb = a1[i][C:, :SLAB].astype(BF16)
        ro = _dot(arb, _block_diag(wu_hi)) + _dot(arb, _block_diag(wu_lo))
        r_ref[rows, sl] = (rf_s[rows, sl] + ro[:, :SLAB]).astype(BF16)
        o_ref[rows, sl] = (ro[:, SLAB:] + arkv[i]).astype(BF16)


def _rwkv_prep(p_shift, w0, w2, a0, a2, g2, k_k, k_a, r_k):
    n = p_shift.shape[0]
    W = RWKV_WIDTH
    tt = RWKV_ROWS
    lane_head = jnp.arange(LANES) // HEAD_DIM
    seg = (lane_head[:, None] == lane_head[None, :]).astype(BF16)
    w2p = jnp.concatenate([w2, jnp.zeros((AAA_LORA, W), w2.dtype)], axis=0).astype(BF16)
    a2p = jnp.concatenate([jnp.zeros((DECAY_LORA, W), a2.dtype), a2], axis=0).astype(BF16)
    row = lambda i: (i, 0)
    vec = lambda t: t.astype(F32).reshape(1, -1)
    bf16_out = jax.ShapeDtypeStruct((n, W), BF16)
    return pl.pallas_call(
        functools.partial(_rwkv_prep_kernel, n_chunks=tt // CHUNK),
        grid=(n // tt,),
        in_specs=[pl.BlockSpec((tt, RWKV_COLS), row),
                  _resident((1, W)), _resident((LANES, W)), _resident((1, W)),
                  _resident((LANES, W)), _resident((GATE_LORA, W)),
                  _resident((1, W)), _resident((1, W)), _resident((1, W)),
                  _resident((LANES, LANES))],
        out_specs=[pl.BlockSpec((tt, W), row)] * 6 + [pl.BlockSpec((tt // CHUNK, W), row)],
        out_shape=[bf16_out] * 6 + [jax.ShapeDtypeStruct((n // CHUNK, W), F32)],
        scratch_shapes=[pltpu.VMEM((tt, W), BF16)] * 5 + [pltpu.VMEM((tt, W), F32)] * 3,
        compiler_params=pltpu.CompilerParams(
            dimension_semantics=("parallel",), vmem_limit_bytes=VMEM_LIMIT),
        name="rwkv_prep",
    )(p_shift, vec(w0), w2p, vec(a0), a2p, g2.astype(BF16), vec(k_k), vec(k_a), vec(r_k),
      seg)


def _rwkv_scan_kernel(m_ref, n_ref, r_ref, o_ref, bonus_ref, gate_ref, gam_ref, lnw_ref, lnb_ref,
                      seg_ref, out_ref, state, obuf, *, n_batch, n_chunks):
    @pl.when(pl.program_id(0) == 0)
    def _():
        state[...] = jnp.zeros_like(state)

    C = CHUNK
    first_head = _lane_block((C, SLAB), HEAD_DIM) == 0
    for c in range(n_chunks):
        rows = slice(c * C, (c + 1) * C)
        for b in range(n_batch):
            for s in range(N_SLABS):
                sl = slice(s * SLAB, (s + 1) * SLAB)
                s0 = state[b * N_SLABS + s]
                lhs = jnp.concatenate([m_ref[b, rows, sl], r_ref[b, rows, sl]], axis=0)
                res = _dot(lhs, _block_diag(s0.astype(BF16)))
                g_col = jnp.broadcast_to(gam_ref[b, c:c + 1, sl], (SLAB, SLAB)).T
                decay = jnp.where(first_head, g_col[:HEAD_DIM], g_col[HEAD_DIM:])
                state[b * N_SLABS + s] = res[:C] + decay * s0 + n_ref[b, rows, sl]
                obuf[b, rows, sl] = res[C:] + o_ref[b, rows, sl]

    seg_mean = seg_ref[...]
    for b in range(n_batch):
        o = obuf[b]
        d = o - _seg_sum(o, seg_mean)
        var = _seg_sum(d * d, seg_mean)
        y = d * lax.rsqrt(var + GN_EPS) * lnw_ref[...] + lnb_ref[...]
        out_ref[b] = ((y + bonus_ref[b]) * gate_ref[b]).astype(BF16)


def _rwkv_scan(mats, n_batch, seq_len, ln_w, ln_b):
    W = RWKV_WIDTH
    tt = SCAN_ROWS
    lane_head = jnp.arange(LANES) // HEAD_DIM
    seg_mean = ((lane_head[:, None] == lane_head[None, :]).astype(F32) / HEAD_DIM).astype(BF16)
    *mats, gam = mats
    mats = [t.reshape(n_batch, seq_len, W) for t in mats]
    mats.append(gam.reshape(n_batch, seq_len // CHUNK, W))
    blk = pl.BlockSpec((n_batch, tt, W), lambda i: (0, i, 0))
    gam_blk = pl.BlockSpec((n_batch, tt // CHUNK, W), lambda i: (0, i, 0))
    out = pl.pallas_call(
        functools.partial(_rwkv_scan_kernel, n_batch=n_batch, n_chunks=tt // CHUNK),
        grid=(seq_len // tt,),
        in_specs=[blk] * 6 + [gam_blk, _resident((1, W)), _resident((1, W)),
                              _resident((LANES, LANES))],
        out_specs=blk,
        out_shape=jax.ShapeDtypeStruct((n_batch, seq_len, W), BF16),
        scratch_shapes=[pltpu.VMEM((n_batch * N_SLABS, CHUNK, SLAB), F32),
                        pltpu.VMEM((n_batch, tt, W), F32)],
        compiler_params=pltpu.CompilerParams(
            dimension_semantics=("arbitrary",), vmem_limit_bytes=VMEM_LIMIT),
        name="rwkv_scan",
    )(*mats, ln_w.astype(F32).reshape(1, W), ln_b.astype(F32).reshape(1, W), seg_mean)
    return out.reshape(n_batch * seq_len, W)


def _attn_kernel(sink_ref, q_ref, kp_ref, kc_ref, vp_ref, vc_ref, out_ref, *, n_blocks):
    BQ = WINDOW
    first = pl.program_id(1) == 0
    kx = jnp.concatenate([kp_ref[...], kc_ref[...]], axis=0)
    vx = jnp.concatenate([vp_ref[...], vc_ref[...]], axis=0)
    lo = _lane_block(kx.shape, HEAD_DIM, LANES) == 0
    kxr = pltpu.roll(kx, HEAD_DIM, axis=1)
    vxr = pltpu.roll(vx, HEAD_DIM, axis=1)
    bf = lambda t: t.astype(BF16)
    k_lo = [bf(jnp.where(lo, kx, 0.0)), bf(jnp.where(lo, kxr, 0.0))]
    k_hi = [bf(jnp.where(lo, 0.0, kxr)), bf(jnp.where(lo, 0.0, kx))]
    v_lo = [bf(jnp.where(lo, vx, 0.0)), bf(jnp.where(lo, vxr, 0.0))]
    v_hi = [bf(jnp.where(lo, 0.0, vxr)), bf(jnp.where(lo, 0.0, vx))]

    qi = lax.broadcasted_iota(jnp.int32, (BQ, 2 * BQ), 0)
    kj = lax.broadcasted_iota(jnp.int32, (BQ, 2 * BQ), 1)
    dist = qi + BQ - kj
    band = (dist >= 0) & (dist < WINDOW)
    band_first = band & (kj >= jnp.where(first, BQ, 0))
    col0 = lax.broadcasted_iota(jnp.int32, (1, 2 * BQ), 1) == 0
    log2e = math.log2(math.e)
    fill = [jnp.where(col0, sink_ref[h] * log2e, -jnp.inf) for h in range(N_Q_HEADS)]
    vrow = lax.broadcasted_iota(jnp.int32, (4 * BQ, LANES), 0) & (2 * BQ - 1)
    ones_sel = jnp.where((lax.broadcasted_iota(jnp.int32, (4 * BQ, LANES), 0) < 2 * BQ)
                         == (_lane_block((4 * BQ, LANES), HEAD_DIM, LANES) == 0),
                         1.0, 0.0).astype(BF16)
    qscale = HEAD_DIM ** -0.5 * log2e
    n_pairs = N_Q_HEADS // 2

    kv_of = [(2 * j) // (N_Q_HEADS // N_KV_HEADS) for j in range(n_pairs)]
    units = [(qb, j) for qb in range(n_blocks) for j in range(n_pairs)]
    kcat, vcat = {}, {}
    for qb in range(n_blocks):
        krows = slice(qb * BQ, (qb + 2) * BQ)
        for g in range(N_KV_HEADS):
            kcat[qb, g] = jnp.concatenate([k_lo[g][krows], k_hi[g][krows]], axis=0)
            vcat[qb, g] = jnp.concatenate(
                [jnp.where(vrow == 0, 0.0,
                           jnp.concatenate([v_lo[g][krows], v_hi[g][krows]], axis=0)),
                 ones_sel], axis=1)
    s = [_dot_nt((q_ref[qb * BQ:(qb + 1) * BQ, j * LANES:(j + 1) * LANES] * qscale).astype(BF16),
                 kcat[qb, kv_of[j]]) for qb, j in units]
    sh = [[jnp.where(band_first if qb == 0 else band,
                     x[:, hh * 2 * BQ:(hh + 1) * 2 * BQ], fill[2 * j + hh])
           for hh in range(2)] for x, (qb, j) in zip(s, units)]
    mx = [[jnp.max(x, axis=-1, keepdims=True) for x in row] for row in sh]
    e = [jnp.concatenate([jnp.exp2(x - m).astype(BF16) for x, m in zip(xr, mr)], axis=1)
         for xr, mr in zip(sh, mx)]
    pv = [_dot(x, vcat[qb, kv_of[j]]) for x, (qb, j) in zip(e, units)]
    for x, (qb, j) in zip(pv, units):
        out_ref[qb * BQ:(qb + 1) * BQ, j * LANES:(j + 1) * LANES] = (
            x[:, :LANES] / x[:, LANES:]).astype(BF16)


def _attn(p_attn, n_batch, seq_len, sinks):
    n = p_attn.shape[0]
    rows = ATTN_ROWS
    n_blocks = rows // WINDOW
    steps = seq_len // rows
    kcol = ATTN_WIDTH // LANES
    vcol = kcol + KV_WIDTH // LANES
    cur = lambda b, i: b * steps + i
    prev = lambda b, i: (b * steps + i) * n_blocks - jnp.minimum(i, 1)
    return pl.pallas_call(
        functools.partial(_attn_kernel, n_blocks=n_blocks),
        grid=(n_batch, steps),
        in_specs=[pl.BlockSpec(memory_space=pltpu.SMEM),
                  pl.BlockSpec((rows, ATTN_WIDTH), lambda b, i: (cur(b, i), 0)),
                  pl.BlockSpec((WINDOW, KV_WIDTH), lambda b, i: (prev(b, i), kcol)),
                  pl.BlockSpec((rows, KV_WIDTH), lambda b, i: (cur(b, i), kcol)),
                  pl.BlockSpec((WINDOW, KV_WIDTH), lambda b, i: (prev(b, i), vcol)),
                  pl.BlockSpec((rows, KV_WIDTH), lambda b, i: (cur(b, i), vcol))],
        out_specs=pl.BlockSpec((rows, ATTN_WIDTH), lambda b, i: (cur(b, i), 0)),
        out_shape=jax.ShapeDtypeStruct((n, ATTN_WIDTH), BF16),
        compiler_params=pltpu.CompilerParams(
            dimension_semantics=("parallel", "arbitrary"), vmem_limit_bytes=VMEM_LIMIT),
        name="swa_attn",
    )(sinks.astype(F32), p_attn, p_attn, p_attn, p_attn, p_attn)


def kernel(x, norm_ffn1, ffn1_gate, ffn1_up, ffn1_down, norm_mix, w_in, b_in_attn, rwkv_shift_mix, rwkv_w0, rwkv_w2, rwkv_a0, rwkv_a2, rwkv_g2, rwkv_k_k, rwkv_k_a, rwkv_r_k, rwkv_ln_w, rwkv_ln_b, attn_sinks, w_out, norm_ffn2, ffn2_gate, ffn2_up, ffn2_down, norm_final):
    n_batch, seq_len, d = x.shape
    depth = w_in.shape[0]
    h = x.reshape(n_batch * seq_len, d)
    for l in range(depth):
        h = _ffn(h, norm_ffn1[l], ffn1_gate[l], ffn1_up[l], ffn1_down[l])
        p_shift, p_attn = _in_proj(h, seq_len, norm_mix[l], w_in[l], b_in_attn[l],
                                   rwkv_shift_mix[l])
        mats = _rwkv_prep(p_shift, rwkv_w0[l], rwkv_w2[l], rwkv_a0[l], rwkv_a2[l], rwkv_g2[l],
                          rwkv_k_k[l], rwkv_k_a[l], rwkv_r_k[l])
        o_rwkv = _rwkv_scan(mats, n_batch, seq_len, rwkv_ln_w[l], rwkv_ln_b[l])
        o_attn = _attn(p_attn, n_batch, seq_len, attn_sinks[l])
        h = _ffn(h, norm_ffn2[l], ffn2_gate[l], ffn2_up[l], ffn2_down[l],
                 mix=(o_rwkv, o_attn, w_out[l]),
                 final_norm=norm_final if l == depth - 1 else None)
    return h.reshape(n_batch, seq_len, d)
```

```python
import functools
import math

import jax
import jax.numpy as jnp
from jax import lax
from jax.experimental import pallas as pl
from jax.experimental.pallas import tpu as pltpu

F32 = jnp.float32
BF16 = jnp.bfloat16

HEAD_DIM = 64
LANES = 128
SLAB = 128
HEADS_PER_SLAB = SLAB // HEAD_DIM
CHUNK = 64
N_RWKV_HEADS = 8
RWKV_WIDTH = N_RWKV_HEADS * HEAD_DIM
N_SLABS = RWKV_WIDTH // SLAB
DECAY_LORA = 64
AAA_LORA = 64
GATE_LORA = 128
RWKV_COLS = 3 * RWKV_WIDTH + DECAY_LORA + AAA_LORA + GATE_LORA
N_Q_HEADS = 8
N_KV_HEADS = 2
ATTN_WIDTH = N_Q_HEADS * HEAD_DIM
KV_WIDTH = N_KV_HEADS * HEAD_DIM
ATTN_COLS = ATTN_WIDTH + 2 * KV_WIDTH
WINDOW = 128
GN_EPS = 64e-5
NORM_EPS = 1e-5
VMEM_LIMIT = 56 * 1024 * 1024

FFN_ROWS = 512
FFN_COLS = 256
INPROJ_ROWS = 1024
RWKV_ROWS = 512
MIX_ROWS = 512


def _dot(a, b):
    return jnp.dot(a, b, preferred_element_type=F32)


def _dot_nt(a, b):
    return lax.dot_general(a, b, (((1,), (1,)), ((), ())), preferred_element_type=F32)


def _rms(x, g):
    ms = jnp.mean(x * x, axis=-1, keepdims=True)
    return x * lax.rsqrt(ms + NORM_EPS) * g


def _lane_block(shape, width, span=SLAB):
    lane = lax.broadcasted_iota(jnp.int32, shape, 1)
    return (lane & (span - 1)) >> (width.bit_length() - 1)


def _block_diag(x, width=HEAD_DIM):
    blk = _lane_block(x.shape, width)
    return jnp.concatenate([jnp.where(blk == g, x, 0.0) for g in range(SLAB // width)], axis=0)


def _split2(x):
    hi = x.astype(BF16)
    return hi, (x - hi.astype(F32)).astype(BF16)


def _seg_sum(x, seg):
    xb = x.astype(BF16)
    rows, groups = x.shape[0], x.shape[1] // LANES
    y = _dot(jnp.concatenate([xb[:, j * LANES:(j + 1) * LANES] for j in range(groups)], axis=0), seg)
    return jnp.concatenate([y[j * rows:(j + 1) * rows] for j in range(groups)], axis=1)


def _resident(shape):
    return pl.BlockSpec(shape, lambda *_: (0,) * len(shape), pipeline_mode=pl.Buffered(1))


def _ffn_kernel(*refs, has_mix, final_norm, nf, tf):
    refs = list(refs)
    x_ref = refs.pop(0)
    if has_mix:
        orw_ref, oat_ref, wo_ref = refs[:3]
        refs = refs[3:]
    g_ref, wg_ref, wu_ref, wd_ref = refs[:4]
    refs = refs[4:]
    if final_norm:
        gf_ref = refs.pop(0)
    out_ref, h_s, acc = refs

    x = x_ref[...]
    if has_mix:
        x = (x + _dot(orw_ref[...], wo_ref[:RWKV_WIDTH].astype(BF16))
             + _dot(oat_ref[...], wo_ref[RWKV_WIDTH:].astype(BF16)))
        out_ref[...] = x
    h_s[...] = _rms(x, g_ref[...]).astype(BF16)

    def down(f):
        cols = pl.ds(f * tf if isinstance(f, int) else pl.multiple_of(f * tf, tf), tf)
        h = h_s[...]
        gate = _dot(h, wg_ref[:, cols].astype(BF16))
        up = _dot(h, wu_ref[:, cols].astype(BF16))
        act = (gate * jax.nn.sigmoid(gate) * up).astype(BF16)
        return _dot(act, wd_ref[cols, :].astype(BF16))

    acc[...] = down(0)

    for f in range(1, nf):
        acc[...] += down(f)
    res = out_ref[...] if has_mix else x_ref[...]
    y = res + 0.5 * acc[...]
    if final_norm:
        y = _rms(y, gf_ref[...])
    out_ref[...] = y


def _ffn(x, norm, w_gate, w_up, w_down, mix=None, final_norm=None):
    n, d = x.shape
    f = w_gate.shape[1]
    tm, tf = FFN_ROWS, FFN_COLS
    nf = f // tf
    row = lambda i: (i, 0)
    wg, wu, wd = w_gate, w_up, w_down
    args = [x]
    specs = [pl.BlockSpec((tm, d), row)]
    if mix is not None:
        o_rwkv, o_attn, w_out = mix
        args += [o_rwkv, o_attn, w_out]
        specs += [pl.BlockSpec((tm, RWKV_WIDTH), row), pl.BlockSpec((tm, ATTN_WIDTH), row),
                  _resident((RWKV_WIDTH + ATTN_WIDTH, d))]
    args += [norm.reshape(1, d), wg, wu, wd]
    specs += [_resident((1, d)), _resident((d, f)), _resident((d, f)), _resident((f, d))]
    if final_norm is not None:
        args.append(final_norm.reshape(1, d))
        specs.append(_resident((1, d)))
    return pl.pallas_call(
        functools.partial(_ffn_kernel, has_mix=mix is not None,
                          final_norm=final_norm is not None, nf=nf, tf=tf),
        grid=(n // tm,),
        in_specs=specs,
        out_specs=pl.BlockSpec((tm, d), row),
        out_shape=jax.ShapeDtypeStruct((n, d), F32),
        scratch_shapes=[pltpu.VMEM((tm, d), BF16), pltpu.VMEM((tm, d), F32)],
        compiler_params=pltpu.CompilerParams(
            dimension_semantics=("parallel",), vmem_limit_bytes=VMEM_LIMIT),
        name="ffn_mix" if mix is not None else "ffn",
    )(*args)


def _inproj_kernel(x_ref, g_ref, w_ref, b_ref, mix_ref, prw_ref, pat_ref, last_row,
                   *, tiles_per_seq):
    h = _rms(x_ref[...], g_ref[...]).astype(BF16)
    p = _dot(h, w_ref[...])
    pat_ref[...] = p[:, RWKV_COLS:] + b_ref[...]
    pr = p[:, :RWKV_COLS]
    prev_row = jnp.where(pl.program_id(0) % tiles_per_seq == 0, 0.0, last_row[0:1, :])
    row = lax.broadcasted_iota(jnp.int32, pr.shape, 0)
    p_prev = jnp.where(row == 0, prev_row, pltpu.roll(pr, 1, axis=0))
    prw_ref[...] = pr + (p_prev - pr) * mix_ref[...]
    last_row[0:1, :] = pr[pr.shape[0] - 1:, :]


def _in_proj(x, seq_len, norm, w_in, b_attn, shift_mix):
    n, d = x.shape
    cols = w_in.shape[1]
    tm = INPROJ_ROWS
    bias = b_attn.astype(F32).reshape(1, ATTN_COLS)
    row = lambda i: (i, 0)
    return pl.pallas_call(
        functools.partial(_inproj_kernel, tiles_per_seq=seq_len // tm),
        grid=(n // tm,),
        in_specs=[pl.BlockSpec((tm, d), row), _resident((1, d)), _resident((d, cols)),
                  _resident((1, ATTN_COLS)), _resident((1, RWKV_COLS))],
        out_specs=[pl.BlockSpec((tm, RWKV_COLS), row), pl.BlockSpec((tm, ATTN_COLS), row)],
        out_shape=[jax.ShapeDtypeStruct((n, RWKV_COLS), F32),
                   jax.ShapeDtypeStruct((n, ATTN_COLS), F32)],
        scratch_shapes=[pltpu.VMEM((8, RWKV_COLS), F32)],
        compiler_params=pltpu.CompilerParams(
            dimension_semantics=("arbitrary",), vmem_limit_bytes=VMEM_LIMIT),
        name="in_proj",
    )(x, norm.reshape(1, d), w_in.astype(BF16), bias, shift_mix.astype(F32).reshape(1, -1))


def _rwkv_prep_kernel(ps_ref, w0_ref, w2_ref, a0_ref, a2_ref, g2_ref, kk_ref, ka_ref, rk_ref,
                      seg_ref,
                      m_ref, n_ref, r_ref, o_ref, bonus_ref, gate_ref, gam_ref,
                      at_s, bt_s, kt_s, rt_s, v_s, bh_s, kh_s, rf_s, *, n_chunks):
    C = CHUNK
    W = RWKV_WIDTH
    ps = ps_ref[...]
    r = ps[:, 0:W]
    k = ps[:, W:2 * W]
    v = ps[:, 2 * W:3 * W]
    wa = ps[:, 3 * W:3 * W + LANES]
    gl = ps[:, 3 * W + LANES:3 * W + 2 * LANES]
    z = w0_ref[...] + _dot(jnp.tanh(wa).astype(BF16), w2_ref[...])
    lw = -math.exp(-0.5) * jax.nn.sigmoid(z)
    a = jax.nn.sigmoid(a0_ref[...] + _dot(wa.astype(BF16), a2_ref[...]))
    gate_ref[...] = _dot(jax.nn.sigmoid(gl).astype(BF16), g2_ref[...]).astype(BF16)
    seg = seg_ref[...]
    kk = k * kk_ref[...]
    kk = kk * jnp.minimum(lax.rsqrt(_seg_sum(kk * kk, seg)), 1e12)
    k = k * (1.0 + (a - 1.0) * ka_ref[...])
    b = kk * a
    bonus_ref[...] = (_seg_sum(r * k * rk_ref[...], seg) * v).astype(BF16)
    v_s[...] = v.astype(BF16)
    cs = lw
    row_in_chunk = lax.broadcasted_iota(jnp.int32, lw.shape, 0) & (C - 1)
    for shift in (1, 2, 4, 8, 16, 32):
        cs = cs + jnp.where(row_in_chunk >= shift, pltpu.roll(cs, shift, axis=0), 0.0)
    for c in range(n_chunks):
        rows = slice(c * C, (c + 1) * C)
        cs_c = cs[rows]
        gam = jnp.exp(cs[(c + 1) * C - 1:(c + 1) * C, :])
        e_neg = jnp.exp(-cs_c)
        rt = r[rows] * jnp.exp(cs_c)
        bt = b[rows] * e_neg
        kt = k[rows] * e_neg
        at_s[rows, :] = (-kk[rows] * jnp.exp(cs_c - lw[rows])).astype(BF16)
        bt_s[rows, :] = bt.astype(BF16)
        kt_s[rows, :] = kt.astype(BF16)
        rt_s[rows, :] = rt.astype(BF16)
        rf_s[rows, :] = rt
        bh_s[rows, :] = bt * gam
        kh_s[rows, :] = kt * gam
        gam_ref[c:c + 1, :] = gam

    rowi = lax.broadcasted_iota(jnp.int32, (2 * C, 2 * SLAB), 0)
    lane_s = lax.broadcasted_iota(jnp.int32, (2 * C, 2 * SLAB), 1) & (C - 1)
    keep_a1 = lane_s < (rowi & (C - 1)) + jnp.where(rowi < C, 0, 1)
    row_c = lax.broadcasted_iota(jnp.int32, (C, SLAB), 0)
    col_c = lax.broadcasted_iota(jnp.int32, (C, SLAB), 1) & (C - 1)
    eye_pack = jnp.where(col_c == row_c, 1.0, 0.0).astype(F32)
    same_head = ((lax.broadcasted_iota(jnp.int32, (SLAB, 2 * SLAB), 0) >> 6)
                 == _lane_block((SLAB, 2 * SLAB), HEAD_DIM))
    zeros_c = jnp.zeros((C, SLAB), BF16)

    def fold(x):
        out = x[:HEAD_DIM]
        for h in range(1, HEADS_PER_SLAB):
            out = out + x[h * HEAD_DIM:(h + 1) * HEAD_DIM]
        return out

    chains = [(c, s) for c in range(n_chunks) for s in range(N_SLABS)]
    ld = lambda ref: [ref[c * C:(c + 1) * C, s * SLAB:(s + 1) * SLAB] for c, s in chains]
    at, bt, kt, rt, vv = ld(at_s), ld(bt_s), ld(kt_s), ld(rt_s), ld(v_s)
    a1 = [jnp.where(keep_a1,
                    _dot_nt(jnp.concatenate([x, y], axis=0),
                            jnp.concatenate([_block_diag(p), _block_diag(q)], axis=0)), 0.0)
          for x, y, p, q in zip(at, rt, bt, kt)]
    aab = [x[:C, :SLAB] for x in a1]
    kv = [_dot(x[:, SLAB:].astype(BF16), _block_diag(y)) for x, y in zip(a1, vv)]
    akv = [_split2(x[:C]) for x in kv]
    arkv = [x[C:] for x in kv]
    def sub_blocks(x, s):
        return jnp.where(((row_c // s) & 1 == 1) & (col_c // s == row_c // s - 1), x, 0.0)

    tinv = [eye_pack + sub_blocks(x, 1) for x in aab]
    for s in (2, 4, 8, 16, 32):
        tb = [t.astype(BF16) for t in tinv]
        te = [_dot(t, _block_diag(sub_blocks(x, s).astype(BF16))).astype(BF16)
              for t, x in zip(tb, aab)]
        tinv = [t + _dot(x, _block_diag(y)) for t, x, y in zip(tinv, te, tb)]
    tb = [t.astype(BF16) for t in tinv]
    wu = [_dot(t, _block_diag(jnp.concatenate([x, y[0]], axis=1)))
          for t, x, y in zip(tb, at, akv)]
    wu = [_split2(jnp.concatenate([x[:, :SLAB], x[:, SLAB:] + _dot(t, _block_diag(y[1]))], axis=1))
          for x, t, y in zip(wu, tb, akv)]
    for i, (c, s) in enumerate(chains):
        rows = slice(c * C, (c + 1) * C)
        sl = slice(s * SLAB, (s + 1) * SLAB)
        lhs_t = jnp.concatenate([bh_s[rows, sl], kh_s[rows, sl]], axis=0)
        wu_hi, wu_lo = wu[i]
        rhs = jnp.concatenate([wu_hi, jnp.concatenate([zeros_c, vv[i]], axis=1)], axis=0)
        mn = _dot(lhs_t.T.astype(BF16), rhs)
        mn = jnp.where(same_head, mn, 0.0)
        m_ref[rows, sl] = fold(mn[:, :SLAB]).astype(BF16)
        n_ref[rows, sl] = fold(mn[:, SLAB:]).astype(BF16)
        arb = a1[i][C:, :SLAB].astype(BF16)
        ro = _dot(arb, _block_diag(wu_hi)) + _dot(arb, _block_diag(wu_lo))
        r_ref[rows, sl] = (rf_s[rows, sl] + ro[:, :SLAB]).astype(BF16)
        o_ref[rows, sl] = (ro[:, SLAB:] + arkv[i]).astype(BF16)


def _rwkv_prep(p_shift, w0, w2, a0, a2, g2, k_k, k_a, r_k):
    n = p_shift.shape[0]
    W = RWKV_WIDTH
    tt = RWKV_ROWS
    lane_head = jnp.arange(LANES) // HEAD_DIM
    seg = (lane_head[:, None] == lane_head[None, :]).astype(BF16)
    w2p = jnp.concatenate([w2, jnp.zeros((AAA_LORA, W), w2.dtype)], axis=0).astype(BF16)
    a2p = jnp.concatenate([jnp.zeros((DECAY_LORA, W), a2.dtype), a2], axis=0).astype(BF16)
    row = lambda i: (i, 0)
    vec = lambda t: t.astype(F32).reshape(1, -1)
    bf16_out = jax.ShapeDtypeStruct((n, W), BF16)
    return pl.pallas_call(
        functools.partial(_rwkv_prep_kernel, n_chunks=tt // CHUNK),
        grid=(n // tt,),
        in_specs=[pl.BlockSpec((tt, RWKV_COLS), row),
                  _resident((1, W)), _resident((LANES, W)), _resident((1, W)),
                  _resident((LANES, W)), _resident((GATE_LORA, W)),
                  _resident((1, W)), _resident((1, W)), _resident((1, W)),
                  _resident((LANES, LANES))],
        out_specs=[pl.BlockSpec((tt, W), row)] * 6 + [pl.BlockSpec((tt // CHUNK, W), row)],
        out_shape=[bf16_out] * 6 + [jax.ShapeDtypeStruct((n // CHUNK, W), F32)],
        scratch_shapes=[pltpu.VMEM((tt, W), BF16)] * 5 + [pltpu.VMEM((tt, W), F32)] * 3,
        compiler_params=pltpu.CompilerParams(
            dimension_semantics=("parallel",), vmem_limit_bytes=VMEM_LIMIT),
        name="rwkv_prep",
    )(p_shift, vec(w0), w2p, vec(a0), a2p, g2.astype(BF16), vec(k_k), vec(k_a), vec(r_k),
      seg)


def _scan_pieces(m_ref, n_ref, r_ref, o_ref, bonus_ref, gate_ref, gam_ref, lnw_ref, lnb_ref,
                 seg_ref, out_ref, state, obuf, *, n_batch, n_chunks):
    C = CHUNK
    first_head = _lane_block((C, SLAB), HEAD_DIM) == 0
    for c in range(n_chunks):
        rows = slice(c * C, (c + 1) * C)
        for b in range(n_batch):
            for s in range(N_SLABS):
                sl = slice(s * SLAB, (s + 1) * SLAB)
                s0 = state[b * N_SLABS + s]
                lhs = jnp.concatenate([m_ref[b, rows, sl], r_ref[b, rows, sl]], axis=0)
                res = _dot(lhs, _block_diag(s0.astype(BF16)))
                g_col = jnp.broadcast_to(gam_ref[b, c:c + 1, sl], (SLAB, SLAB)).T
                decay = jnp.where(first_head, g_col[:HEAD_DIM], g_col[HEAD_DIM:])
                state[b * N_SLABS + s] = res[:C] + decay * s0 + n_ref[b, rows, sl]
                obuf[b, rows, sl] = res[C:] + o_ref[b, rows, sl]
        yield

    seg_mean = seg_ref[...]
    for b in range(n_batch):
        o = obuf[b]
        d = o - _seg_sum(o, seg_mean)
        var = _seg_sum(d * d, seg_mean)
        y = d * lax.rsqrt(var + GN_EPS) * lnw_ref[...] + lnb_ref[...]
        out_ref[b] = ((y + bonus_ref[b]) * gate_ref[b]).astype(BF16)
        yield


def _attn_pieces(sink_ref, q_ref, kp_ref, kc_ref, vp_ref, vc_ref, out_ref, *, first, n_blocks):
    BQ = WINDOW
    kx = jnp.concatenate([kp_ref[...], kc_ref[...]], axis=0)
    vx = jnp.concatenate([vp_ref[...], vc_ref[...]], axis=0)
    lo = _lane_block(kx.shape, HEAD_DIM, LANES) == 0
    kxr = pltpu.roll(kx, HEAD_DIM, axis=1)
    vxr = pltpu.roll(vx, HEAD_DIM, axis=1)
    bf = lambda t: t.astype(BF16)
    k_lo = [bf(jnp.where(lo, kx, 0.0)), bf(jnp.where(lo, kxr, 0.0))]
    k_hi = [bf(jnp.where(lo, 0.0, kxr)), bf(jnp.where(lo, 0.0, kx))]
    v_lo = [bf(jnp.where(lo, vx, 0.0)), bf(jnp.where(lo, vxr, 0.0))]
    v_hi = [bf(jnp.where(lo, 0.0, vxr)), bf(jnp.where(lo, 0.0, vx))]

    qi = lax.broadcasted_iota(jnp.int32, (BQ, 2 * BQ), 0)
    kj = lax.broadcasted_iota(jnp.int32, (BQ, 2 * BQ), 1)
    dist = qi + BQ - kj
    band = (dist >= 0) & (dist < WINDOW)
    band_first = band & (kj >= jnp.where(first, BQ, 0))
    col0 = lax.broadcasted_iota(jnp.int32, (1, 2 * BQ), 1) == 0
    log2e = math.log2(math.e)
    fill = [jnp.where(col0, sink_ref[h] * log2e, -jnp.inf) for h in range(N_Q_HEADS)]
    vrow = lax.broadcasted_iota(jnp.int32, (4 * BQ, LANES), 0) & (2 * BQ - 1)
    ones_sel = jnp.where((lax.broadcasted_iota(jnp.int32, (4 * BQ, LANES), 0) < 2 * BQ)
                         == (_lane_block((4 * BQ, LANES), HEAD_DIM, LANES) == 0),
                         1.0, 0.0).astype(BF16)
    qscale = HEAD_DIM ** -0.5 * log2e
    n_pairs = N_Q_HEADS // 2

    kv_of = [(2 * j) // (N_Q_HEADS // N_KV_HEADS) for j in range(n_pairs)]
    units = [(qb, j) for qb in range(n_blocks) for j in range(n_pairs)]
    kcat, vcat = {}, {}
    for qb in range(n_blocks):
        krows = slice(qb * BQ, (qb + 2) * BQ)
        for g in range(N_KV_HEADS):
            kcat[qb, g] = jnp.concatenate([k_lo[g][krows], k_hi[g][krows]], axis=0)
            vcat[qb, g] = jnp.concatenate(
                [jnp.where(vrow == 0, 0.0,
                           jnp.concatenate([v_lo[g][krows], v_hi[g][krows]], axis=0)),
                 ones_sel], axis=1)
    yield
    s = [_dot_nt((q_ref[qb * BQ:(qb + 1) * BQ, j * LANES:(j + 1) * LANES] * qscale).astype(BF16),
                 kcat[qb, kv_of[j]]) for qb, j in units]
    yield
    sh = [[jnp.where(band_first if qb == 0 else band,
                     x[:, hh * 2 * BQ:(hh + 1) * 2 * BQ], fill[2 * j + hh])
           for hh in range(2)] for x, (qb, j) in zip(s, units)]
    mx = [[jnp.max(x, axis=-1, keepdims=True) for x in row] for row in sh]
    yield
    e = [jnp.concatenate([jnp.exp2(x - m).astype(BF16) for x, m in zip(xr, mr)], axis=1)
         for xr, mr in zip(sh, mx)]
    yield
    pv = [_dot(x, vcat[qb, kv_of[j]]) for x, (qb, j) in zip(e, units)]
    yield
    for x, (qb, j) in zip(pv, units):
        out_ref[qb * BQ:(qb + 1) * BQ, j * LANES:(j + 1) * LANES] = (
            x[:, :LANES] / x[:, LANES:]).astype(BF16)
    yield


def _scan_attn_kernel(m_ref, n_ref, r_ref, o_ref, bonus_ref, gate_ref, gam_ref, lnw_ref, lnb_ref,
                      seg_ref, sink_ref, q_ref, kp_ref, kc_ref, vp_ref, vc_ref,
                      orw_ref, oat_ref, state, obuf, *, n_batch, n_chunks, n_blocks):
    first = pl.program_id(0) == 0

    @pl.when(first)
    def _():
        state[...] = jnp.zeros_like(state)

    scan = _scan_pieces(m_ref, n_ref, r_ref, o_ref, bonus_ref, gate_ref, gam_ref, lnw_ref,
                        lnb_ref, seg_ref, orw_ref, state, obuf, n_batch=n_batch, n_chunks=n_chunks)
    attn = [_attn_pieces(sink_ref, q_ref.at[b], kp_ref.at[b], kc_ref.at[b], vp_ref.at[b],
                         vc_ref.at[b], oat_ref.at[b], first=first, n_blocks=n_blocks)
            for b in range(n_batch)]
    streams = [scan] + attn
    while streams:
        streams = [g for g in streams if next(g, StopIteration) is not StopIteration]


def _scan_attn(mats, p_attn, n_batch, seq_len, ln_w, ln_b, sinks):
    W = RWKV_WIDTH
    tt = MIX_ROWS
    n_blocks = tt // WINDOW
    lane_head = jnp.arange(LANES) // HEAD_DIM
    seg_mean = ((lane_head[:, None] == lane_head[None, :]).astype(F32) / HEAD_DIM).astype(BF16)
    *mats, gam = mats
    mats = [t.reshape(n_batch, seq_len, W) for t in mats]
    mats.append(gam.reshape(n_batch, seq_len // CHUNK, W))
    p3 = p_attn.reshape(n_batch, seq_len, ATTN_COLS)
    kcol = ATTN_WIDTH // LANES
    vcol = kcol + KV_WIDTH // LANES
    blk = pl.BlockSpec((n_batch, tt, W), lambda i: (0, i, 0))
    gam_blk = pl.BlockSpec((n_batch, tt // CHUNK, W), lambda i: (0, i, 0))
    cur = lambda col: pl.BlockSpec((n_batch, tt, KV_WIDTH), lambda i: (0, i, col))
    prev = lambda col: pl.BlockSpec((n_batch, WINDOW, KV_WIDTH),
                                    lambda i: (0, jnp.maximum(i * n_blocks - 1, 0), col))
    o_rwkv, o_attn = pl.pallas_call(
        functools.partial(_scan_attn_kernel, n_batch=n_batch, n_chunks=tt // CHUNK,
                          n_blocks=n_blocks),
        grid=(seq_len // tt,),
        in_specs=[blk] * 6 + [gam_blk, _resident((1, W)), _resident((1, W)),
                              _resident((LANES, LANES)),
                              pl.BlockSpec(memory_space=pltpu.SMEM),
                              pl.BlockSpec((n_batch, tt, ATTN_WIDTH), lambda i: (0, i, 0)),
                              prev(kcol), cur(kcol), prev(vcol), cur(vcol)],
        out_specs=[blk, pl.BlockSpec((n_batch, tt, ATTN_WIDTH), lambda i: (0, i, 0))],
        out_shape=[jax.ShapeDtypeStruct((n_batch, seq_len, W), BF16),
                   jax.ShapeDtypeStruct((n_batch, seq_len, ATTN_WIDTH), BF16)],
        scratch_shapes=[pltpu.VMEM((n_batch * N_SLABS, CHUNK, SLAB), F32),
                        pltpu.VMEM((n_batch, tt, W), F32)],
        compiler_params=pltpu.CompilerParams(
            dimension_semantics=("arbitrary",), vmem_limit_bytes=VMEM_LIMIT),
        name="scan_attn",
    )(*mats, ln_w.astype(F32).reshape(1, W), ln_b.astype(F32).reshape(1, W), seg_mean,
      sinks.astype(F32), p3, p3, p3, p3, p3)
    return (o_rwkv.reshape(n_batch * seq_len, W),
            o_attn.reshape(n_batch * seq_len, ATTN_WIDTH))


def kernel(x, norm_ffn1, ffn1_gate, ffn1_up, ffn1_down, norm_mix, w_in, b_in_attn, rwkv_shift_mix, rwkv_w0, rwkv_w2, rwkv_a0, rwkv_a2, rwkv_g2, rwkv_k_k, rwkv_k_a, rwkv_r_k, rwkv_ln_w, rwkv_ln_b, attn_sinks, w_out, norm_ffn2, ffn2_gate, ffn2_up, ffn2_down, norm_final):
    n_batch, seq_len, d = x.shape
    depth = w_in.shape[0]
    h = x.reshape(n_batch * seq_len, d)
    for l in range(depth):
        h = _ffn(h, norm_ffn1[l], ffn1_gate[l], ffn1_up[l], ffn1_down[l])
        p_shift, p_attn = _in_proj(h, seq_len, norm_mix[l], w_in[l], b_in_attn[l],
                                   rwkv_shift_mix[l])
        mats = _rwkv_prep(p_shift, rwkv_w0[l], rwkv_w2[l], rwkv_a0[l], rwkv_a2[l], rwkv_g2[l],
                          rwkv_k_k[l], rwkv_k_a[l], rwkv_r_k[l])
        o_rwkv, o_attn = _scan_attn(mats, p_attn, n_batch, seq_len, rwkv_ln_w[l], rwkv_ln_b[l],
                                    attn_sinks[l])
        h = _ffn(h, norm_ffn2[l], ffn2_gate[l], ffn2_up[l], ffn2_down[l],
                 mix=(o_rwkv, o_attn, w_out[l]),
                 final_norm=norm_final if l == depth - 1 else None)
    return h.reshape(n_batch, seq_len, d)
```

```python
import functools
import math

import jax
import jax.numpy as jnp
from jax import lax
from jax.experimental import pallas as pl
from jax.experimental.pallas import tpu as pltpu

F32 = jnp.float32
BF16 = jnp.bfloat16

HEAD_DIM = 64
LANES = 128
SLAB = 128
HEADS_PER_SLAB = SLAB // HEAD_DIM
CHUNK = 64
N_RWKV_HEADS = 8
RWKV_WIDTH = N_RWKV_HEADS * HEAD_DIM
N_SLABS = RWKV_WIDTH // SLAB
DECAY_LORA = 64
AAA_LORA = 64
GATE_LORA = 128
RWKV_COLS = 3 * RWKV_WIDTH + DECAY_LORA + AAA_LORA + GATE_LORA
N_Q_HEADS = 8
N_KV_HEADS = 2
ATTN_WIDTH = N_Q_HEADS * HEAD_DIM
KV_WIDTH = N_KV_HEADS * HEAD_DIM
ATTN_COLS = ATTN_WIDTH + 2 * KV_WIDTH
WINDOW = 128
GN_EPS = 64e-5
NORM_EPS = 1e-5
VMEM_LIMIT = 56 * 1024 * 1024

FFN_ROWS = 512
FFN_COLS = 256
INPROJ_ROWS = 1024
RWKV_ROWS = 512
MIX_ROWS = 512


def _dot(a, b):
    return jnp.dot(a, b, preferred_element_type=F32)


def _dot_nt(a, b):
    return lax.dot_general(a, b, (((1,), (1,)), ((), ())), preferred_element_type=F32)


def _rms(x, g):
    ms = jnp.mean(x * x, axis=-1, keepdims=True)
    return x * lax.rsqrt(ms + NORM_EPS) * g


def _lane_block(shape, width, span=SLAB):
    lane = lax.broadcasted_iota(jnp.int32, shape, 1)
    return (lane & (span - 1)) >> (width.bit_length() - 1)


def _block_diag(x, width=HEAD_DIM):
    blk = _lane_block(x.shape, width)
    return jnp.concatenate([jnp.where(blk == g, x, 0.0) for g in range(SLAB // width)], axis=0)


def _split2(x):
    hi = x.astype(BF16)
    return hi, (x - hi.astype(F32)).astype(BF16)


def _seg_sum(x, seg):
    xb = x.astype(BF16)
    rows, groups = x.shape[0], x.shape[1] // LANES
    y = _dot(jnp.concatenate([xb[:, j * LANES:(j + 1) * LANES] for j in range(groups)], axis=0), seg)
    return jnp.concatenate([y[j * rows:(j + 1) * rows] for j in range(groups)], axis=1)


def _resident(shape):
    return pl.BlockSpec(shape, lambda *_: (0,) * len(shape), pipeline_mode=pl.Buffered(1))


def _ffn_kernel(*refs, has_mix, final_norm, nf, tf):
    refs = list(refs)
    x_ref = refs.pop(0)
    if has_mix:
        orw_ref, oat_ref, wo_ref = refs[:3]
        refs = refs[3:]
    g_ref, wg_hbm, wu_hbm, wd_hbm = refs[:4]
    refs = refs[4:]
    if final_norm:
        gf_ref = refs.pop(0)
    out_ref, h_s, acc, wg_ref, wu_ref, wd_ref, sem = refs

    def copies(f):
        cols = pl.ds(f * tf, tf)
        return (pltpu.make_async_copy(wg_hbm.at[:, cols], wg_ref.at[:, cols], sem.at[0, f]),
                pltpu.make_async_copy(wu_hbm.at[:, cols], wu_ref.at[:, cols], sem.at[1, f]),
                pltpu.make_async_copy(wd_hbm.at[cols, :], wd_ref.at[cols, :], sem.at[2, f]))

    def body(streaming):
        if streaming:
            for f in range(nf):
                for cp in copies(f):
                    cp.start()
        x = x_ref[...]
        if has_mix:
            x = (x + _dot(orw_ref[...], wo_ref[:RWKV_WIDTH].astype(BF16))
                 + _dot(oat_ref[...], wo_ref[RWKV_WIDTH:].astype(BF16)))
            out_ref[...] = x
        h_s[...] = _rms(x, g_ref[...]).astype(BF16)

        def down(f):
            cols = pl.ds(f * tf, tf)
            cp_g, cp_u, cp_d = copies(f)
            h = h_s[...]
            if streaming:
                cp_g.wait()
                cp_u.wait()
            gate = _dot(h, wg_ref[:, cols].astype(BF16))
            up = _dot(h, wu_ref[:, cols].astype(BF16))
            act = (gate * jax.nn.sigmoid(gate) * up).astype(BF16)
            if streaming:
                cp_d.wait()
            return _dot(act, wd_ref[cols, :].astype(BF16))

        acc[...] = down(0)
        for f in range(1, nf):
            acc[...] += down(f)
        res = out_ref[...] if has_mix else x_ref[...]
        y = res + 0.5 * acc[...]
        if final_norm:
            y = _rms(y, gf_ref[...])
        out_ref[...] = y

    first = pl.program_id(0) == 0
    pl.when(first)(lambda: body(True))
    pl.when(jnp.logical_not(first))(lambda: body(False))


def _ffn(x, norm, w_gate, w_up, w_down, mix=None, final_norm=None):
    n, d = x.shape
    f = w_gate.shape[1]
    tm, tf = FFN_ROWS, FFN_COLS
    nf = f // tf
    row = lambda i: (i, 0)
    hbm = pl.BlockSpec(memory_space=pl.ANY)
    args = [x]
    specs = [pl.BlockSpec((tm, d), row)]
    if mix is not None:
        o_rwkv, o_attn, w_out = mix
        args += [o_rwkv, o_attn, w_out]
        specs += [pl.BlockSpec((tm, RWKV_WIDTH), row), pl.BlockSpec((tm, ATTN_WIDTH), row),
                  _resident((RWKV_WIDTH + ATTN_WIDTH, d))]
    args += [norm.reshape(1, d), w_gate, w_up, w_down]
    specs += [_resident((1, d)), hbm, hbm, hbm]
    if final_norm is not None:
        args.append(final_norm.reshape(1, d))
        specs.append(_resident((1, d)))
    return pl.pallas_call(
        functools.partial(_ffn_kernel, has_mix=mix is not None,
                          final_norm=final_norm is not None, nf=nf, tf=tf),
        grid=(n // tm,),
        in_specs=specs,
        out_specs=pl.BlockSpec((tm, d), row),
        out_shape=jax.ShapeDtypeStruct((n, d), F32),
        scratch_shapes=[pltpu.VMEM((tm, d), BF16), pltpu.VMEM((tm, d), F32),
                        pltpu.VMEM((d, f), F32), pltpu.VMEM((d, f), F32), pltpu.VMEM((f, d), F32),
                        pltpu.SemaphoreType.DMA((3, nf))],
        compiler_params=pltpu.CompilerParams(
            dimension_semantics=("arbitrary",), vmem_limit_bytes=VMEM_LIMIT),
        name="ffn_mix" if mix is not None else "ffn",
    )(*args)


def _inproj_kernel(x_ref, g_ref, w_ref, b_ref, mix_ref, prw_ref, pat_ref, last_row,
                   *, tiles_per_seq):
    h = _rms(x_ref[...], g_ref[...]).astype(BF16)
    p = _dot(h, w_ref[...])
    pat_ref[...] = p[:, RWKV_COLS:] + b_ref[...]
    pr = p[:, :RWKV_COLS]
    prev_row = jnp.where(pl.program_id(0) % tiles_per_seq == 0, 0.0, last_row[0:1, :])
    row = lax.broadcasted_iota(jnp.int32, pr.shape, 0)
    p_prev = jnp.where(row == 0, prev_row, pltpu.roll(pr, 1, axis=0))
    prw_ref[...] = pr + (p_prev - pr) * mix_ref[...]
    last_row[0:1, :] = pr[pr.shape[0] - 1:, :]


def _in_proj(x, seq_len, norm, w_in, b_attn, shift_mix):
    n, d = x.shape
    cols = w_in.shape[1]
    tm = INPROJ_ROWS
    bias = b_attn.astype(F32).reshape(1, ATTN_COLS)
    row = lambda i: (i, 0)
    return pl.pallas_call(
        functools.partial(_inproj_kernel, tiles_per_seq=seq_len // tm),
        grid=(n // tm,),
        in_specs=[pl.BlockSpec((tm, d), row), _resident((1, d)), _resident((d, cols)),
                  _resident((1, ATTN_COLS)), _resident((1, RWKV_COLS))],
        out_specs=[pl.BlockSpec((tm, RWKV_COLS), row), pl.BlockSpec((tm, ATTN_COLS), row)],
        out_shape=[jax.ShapeDtypeStruct((n, RWKV_COLS), F32),
                   jax.ShapeDtypeStruct((n, ATTN_COLS), F32)],
        scratch_shapes=[pltpu.VMEM((8, RWKV_COLS), F32)],
        compiler_params=pltpu.CompilerParams(
            dimension_semantics=("arbitrary",), vmem_limit_bytes=VMEM_LIMIT),
        name="in_proj",
    )(x, norm.reshape(1, d), w_in.astype(BF16), bias, shift_mix.astype(F32).reshape(1, -1))


def _rwkv_prep_kernel(ps_ref, w0_ref, w2_ref, a0_ref, a2_ref, g2_ref, kk_ref, ka_ref, rk_ref,
                      seg_ref,
                      m_ref, n_ref, r_ref, o_ref, bonus_ref, gate_ref, gam_ref,
                      at_s, bt_s, kt_s, rt_s, v_s, bh_s, kh_s, rf_s, *, n_chunks):
    C = CHUNK
    W = RWKV_WIDTH
    ps = ps_ref[...]
    r = ps[:, 0:W]
    k = ps[:, W:2 * W]
    v = ps[:, 2 * W:3 * W]
    wa = ps[:, 3 * W:3 * W + LANES]
    gl = ps[:, 3 * W + LANES:3 * W + 2 * LANES]
    z = w0_ref[...] + _dot(jnp.tanh(wa).astype(BF16), w2_ref[...])
    lw = -math.exp(-0.5) * jax.nn.sigmoid(z)
    a = jax.nn.sigmoid(a0_ref[...] + _dot(wa.astype(BF16), a2_ref[...]))
    gate_ref[...] = _dot(jax.nn.sigmoid(gl).astype(BF16), g2_ref[...]).astype(BF16)
    seg = seg_ref[...]
    kk = k * kk_ref[...]
    kk = kk * jnp.minimum(lax.rsqrt(_seg_sum(kk * kk, seg)), 1e12)
    k = k * (1.0 + (a - 1.0) * ka_ref[...])
    b = kk * a
    bonus_ref[...] = (_seg_sum(r * k * rk_ref[...], seg) * v).astype(BF16)
    v_s[...] = v.astype(BF16)
    cs = lw
    row_in_chunk = lax.broadcasted_iota(jnp.int32, lw.shape, 0) & (C - 1)
    for shift in (1, 2, 4, 8, 16, 32):
        cs = cs + jnp.where(row_in_chunk >= shift, pltpu.roll(cs, shift, axis=0), 0.0)
    for c in range(n_chunks):
        rows = slice(c * C, (c + 1) * C)
        cs_c = cs[rows]
        gam = jnp.exp(cs[(c + 1) * C - 1:(c + 1) * C, :])
        e_neg = jnp.exp(-cs_c)
        rt = r[rows] * jnp.exp(cs_c)
        bt = b[rows] * e_neg
        kt = k[rows] * e_neg
        at_s[rows, :] = (-kk[rows] * jnp.exp(cs_c - lw[rows])).astype(BF16)
        bt_s[rows, :] = bt.astype(BF16)
        kt_s[rows, :] = kt.astype(BF16)
        rt_s[rows, :] = rt.astype(BF16)
        rf_s[rows, :] = rt
        bh_s[rows, :] = bt * gam
        kh_s[rows, :] = kt * gam
        gam_ref[c:c + 1, :] = gam

    rowi = lax.broadcasted_iota(jnp.int32, (2 * C, 2 * SLAB), 0)
    lane_s = lax.broadcasted_iota(jnp.int32, (2 * C, 2 * SLAB), 1) & (C - 1)
    keep_a1 = lane_s < (rowi & (C - 1)) + jnp.where(rowi < C, 0, 1)
    row_c = lax.broadcasted_iota(jnp.int32, (C, SLAB), 0)
    col_c = lax.broadcasted_iota(jnp.int32, (C, SLAB), 1) & (C - 1)
    eye_pack = jnp.where(col_c == row_c, 1.0, 0.0).astype(F32)
    same_head = ((lax.broadcasted_iota(jnp.int32, (SLAB, 2 * SLAB), 0) >> 6)
                 == _lane_block((SLAB, 2 * SLAB), HEAD_DIM))
    zeros_c = jnp.zeros((C, SLAB), BF16)

    def fold(x):
        out = x[:HEAD_DIM]
        for h in range(1, HEADS_PER_SLAB):
            out = out + x[h * HEAD_DIM:(h + 1) * HEAD_DIM]
        return out

    chains = [(c, s) for c in range(n_chunks) for s in range(N_SLABS)]
    ld = lambda ref: [ref[c * C:(c + 1) * C, s * SLAB:(s + 1) * SLAB] for c, s in chains]
    at, bt, kt, rt, vv = ld(at_s), ld(bt_s), ld(kt_s), ld(rt_s), ld(v_s)
    a1 = [jnp.where(keep_a1,
                    _dot_nt(jnp.concatenate([x, y], axis=0),
                            jnp.concatenate([_block_diag(p), _block_diag(q)], axis=0)), 0.0)
          for x, y, p, q in zip(at, rt, bt, kt)]
    aab = [x[:C, :SLAB] for x in a1]
    kv = [_dot(x[:, SLAB:].astype(BF16), _block_diag(y)) for x, y in zip(a1, vv)]
    akv = [_split2(x[:C]) for x in kv]
    arkv = [x[C:] for x in kv]
    def sub_blocks(x, s):
        return jnp.where(((row_c // s) & 1 == 1) & (col_c // s == row_c // s - 1), x, 0.0)

    tinv = [eye_pack + sub_blocks(x, 1) for x in aab]
    for s in (2, 4, 8, 16, 32):
        tb = [t.astype(BF16) for t in tinv]
        te = [_dot(t, _block_diag(sub_blocks(x, s).astype(BF16))).astype(BF16)
              for t, x in zip(tb, aab)]
        tinv = [t + _dot(x, _block_diag(y)) for t, x, y in zip(tinv, te, tb)]
    tb = [t.astype(BF16) for t in tinv]
    wu = [_dot(t, _block_diag(jnp.concatenate([x, y[0]], axis=1)))
          for t, x, y in zip(tb, at, akv)]
    wu = [_split2(jnp.concatenate([x[:, :SLAB], x[:, SLAB:] + _dot(t, _block_diag(y[1]))], axis=1))
          for x, t, y in zip(wu, tb, akv)]
    for i, (c, s) in enumerate(chains):
        rows = slice(c * C, (c + 1) * C)
        sl = slice(s * SLAB, (s + 1) * SLAB)
        lhs_t = jnp.concatenate([bh_s[rows, sl], kh_s[rows, sl]], axis=0)
        wu_hi, wu_lo = wu[i]
        rhs = jnp.concatenate([wu_hi, jnp.concatenate([zeros_c, vv[i]], axis=1)], axis=0)
        mn = _dot(lhs_t.T.astype(BF16), rhs)
        mn = jnp.where(same_head, mn, 0.0)
        m_ref[rows, sl] = fold(mn[:, :SLAB]).astype(BF16)
        n_ref[rows, sl] = fold(mn[:, SLAB:]).astype(BF16)
        arb = a1[i][C:, :SLAB].astype(BF16)
        ro = _dot(arb, _block_diag(wu_hi)) + _dot(arb, _block_diag(wu_lo))
        r_ref[rows, sl] = (rf_s[rows, sl] + ro[:, :SLAB]).astype(BF16)
        o_ref[rows, sl] = (ro[:, SLAB:] + arkv[i]).astype(BF16)


def _rwkv_prep(p_shift, w0, w2, a0, a2, g2, k_k, k_a, r_k):
    n = p_shift.shape[0]
    W = RWKV_WIDTH
    tt = RWKV_ROWS
    lane_head = jnp.arange(LANES) // HEAD_DIM
    seg = (lane_head[:, None] == lane_head[None, :]).astype(BF16)
    w2p = jnp.concatenate([w2, jnp.zeros((AAA_LORA, W), w2.dtype)], axis=0).astype(BF16)
    a2p = jnp.concatenate([jnp.zeros((DECAY_LORA, W), a2.dtype), a2], axis=0).astype(BF16)
    row = lambda i: (i, 0)
    vec = lambda t: t.astype(F32).reshape(1, -1)
    bf16_out = jax.ShapeDtypeStruct((n, W), BF16)
    return pl.pallas_call(
        functools.partial(_rwkv_prep_kernel, n_chunks=tt // CHUNK),
        grid=(n // tt,),
        in_specs=[pl.BlockSpec((tt, RWKV_COLS), row),
                  _resident((1, W)), _resident((LANES, W)), _resident((1, W)),
                  _resident((LANES, W)), _resident((GATE_LORA, W)),
                  _resident((1, W)), _resident((1, W)), _resident((1, W)),
                  _resident((LANES, LANES))],
        out_specs=[pl.BlockSpec((tt, W), row)] * 6 + [pl.BlockSpec((tt // CHUNK, W), row)],
        out_shape=[bf16_out] * 6 + [jax.ShapeDtypeStruct((n // CHUNK, W), F32)],
        scratch_shapes=[pltpu.VMEM((tt, W), BF16)] * 5 + [pltpu.VMEM((tt, W), F32)] * 3,
        compiler_params=pltpu.CompilerParams(
            dimension_semantics=("parallel",), vmem_limit_bytes=VMEM_LIMIT),
        name="rwkv_prep",
    )(p_shift, vec(w0), w2p, vec(a0), a2p, g2.astype(BF16), vec(k_k), vec(k_a), vec(r_k),
      seg)


def _scan_pieces(m_ref, n_ref, r_ref, o_ref, bonus_ref, gate_ref, gam_ref, lnw_ref, lnb_ref,
                 seg_ref, out_ref, state, obuf, *, n_batch, n_chunks):
    C = CHUNK
    first_head = _lane_block((C, SLAB), HEAD_DIM) == 0
    for c in range(n_chunks):
        rows = slice(c * C, (c + 1) * C)
        for b in range(n_batch):
            for s in range(N_SLABS):
                sl = slice(s * SLAB, (s + 1) * SLAB)
                s0 = state[b * N_SLABS + s]
                lhs = jnp.concatenate([m_ref[b, rows, sl], r_ref[b, rows, sl]], axis=0)
                res = _dot(lhs, _block_diag(s0.astype(BF16)))
                g_col = jnp.broadcast_to(gam_ref[b, c:c + 1, sl], (SLAB, SLAB)).T
                decay = jnp.where(first_head, g_col[:HEAD_DIM], g_col[HEAD_DIM:])
                state[b * N_SLABS + s] = res[:C] + decay * s0 + n_ref[b, rows, sl]
                obuf[b, rows, sl] = res[C:] + o_ref[b, rows, sl]
        yield

    seg_mean = seg_ref[...]
    for b in range(n_batch):
        o = obuf[b]
        d = o - _seg_sum(o, seg_mean)
        var = _seg_sum(d * d, seg_mean)
        y = d * lax.rsqrt(var + GN_EPS) * lnw_ref[...] + lnb_ref[...]
        out_ref[b] = ((y + bonus_ref[b]) * gate_ref[b]).astype(BF16)
        yield


def _attn_pieces(sink_ref, q_ref, kp_ref, kc_ref, vp_ref, vc_ref, out_ref, *, first, n_blocks):
    BQ = WINDOW
    kx = jnp.concatenate([kp_ref[...], kc_ref[...]], axis=0)
    vx = jnp.concatenate([vp_ref[...], vc_ref[...]], axis=0)
    lo = _lane_block(kx.shape, HEAD_DIM, LANES) == 0
    kxr = pltpu.roll(kx, HEAD_DIM, axis=1)
    vxr = pltpu.roll(vx, HEAD_DIM, axis=1)
    bf = lambda t: t.astype(BF16)
    k_lo = [bf(jnp.where(lo, kx, 0.0)), bf(jnp.where(lo, kxr, 0.0))]
    k_hi = [bf(jnp.where(lo, 0.0, kxr)), bf(jnp.where(lo, 0.0, kx))]
    v_lo = [bf(jnp.where(lo, vx, 0.0)), bf(jnp.where(lo, vxr, 0.0))]
    v_hi = [bf(jnp.where(lo, 0.0, vxr)), bf(jnp.where(lo, 0.0, vx))]

    qi = lax.broadcasted_iota(jnp.int32, (BQ, 2 * BQ), 0)
    kj = lax.broadcasted_iota(jnp.int32, (BQ, 2 * BQ), 1)
    dist = qi + BQ - kj
    band = (dist >= 0) & (dist < WINDOW)
    band_first = band & (kj >= jnp.where(first, BQ, 0))
    col0 = lax.broadcasted_iota(jnp.int32, (1, 2 * BQ), 1) == 0
    log2e = math.log2(math.e)
    fill = [jnp.where(col0, sink_ref[h] * log2e, -jnp.inf) for h in range(N_Q_HEADS)]
    vrow = lax.broadcasted_iota(jnp.int32, (4 * BQ, LANES), 0) & (2 * BQ - 1)
    ones_sel = jnp.where((lax.broadcasted_iota(jnp.int32, (4 * BQ, LANES), 0) < 2 * BQ)
                         == (_lane_block((4 * BQ, LANES), HEAD_DIM, LANES) == 0),
                         1.0, 0.0).astype(BF16)
    qscale = HEAD_DIM ** -0.5 * log2e
    n_pairs = N_Q_HEADS // 2

    kv_of = [(2 * j) // (N_Q_HEADS // N_KV_HEADS) for j in range(n_pairs)]
    units = [(qb, j) for qb in range(n_blocks) for j in range(n_pairs)]
    kcat, vcat = {}, {}
    for qb in range(n_blocks):
        krows = slice(qb * BQ, (qb + 2) * BQ)
        for g in range(N_KV_HEADS):
            kcat[qb, g] = jnp.concatenate([k_lo[g][krows], k_hi[g][krows]], axis=0)
            vcat[qb, g] = jnp.concatenate(
                [jnp.where(vrow == 0, 0.0,
                           jnp.concatenate([v_lo[g][krows], v_hi[g][krows]], axis=0)),
                 ones_sel], axis=1)
    yield
    s = [_dot_nt((q_ref[qb * BQ:(qb + 1) * BQ, j * LANES:(j + 1) * LANES] * qscale).astype(BF16),
                 kcat[qb, kv_of[j]]) for qb, j in units]
    yield
    sh = [[jnp.where(band_first if qb == 0 else band,
                     x[:, hh * 2 * BQ:(hh + 1) * 2 * BQ], fill[2 * j + hh])
           for hh in range(2)] for x, (qb, j) in zip(s, units)]
    mx = [[jnp.max(x, axis=-1, keepdims=True) for x in row] for row in sh]
    yield
    e = [jnp.concatenate([jnp.exp2(x - m).astype(BF16) for x, m in zip(xr, mr)], axis=1)
         for xr, mr in zip(sh, mx)]
    yield
    pv = [_dot(x, vcat[qb, kv_of[j]]) for x, (qb, j) in zip(e, units)]
    yield
    for x, (qb, j) in zip(pv, units):
        out_ref[qb * BQ:(qb + 1) * BQ, j * LANES:(j + 1) * LANES] = (
            x[:, :LANES] / x[:, LANES:]).astype(BF16)
    yield


def _scan_attn_kernel(m_ref, n_ref, r_ref, o_ref, bonus_ref, gate_ref, gam_ref, lnw_ref, lnb_ref,
                      seg_ref, sink_ref, q_ref, kp_ref, kc_ref, vp_ref, vc_ref,
                      orw_ref, oat_ref, state, obuf, *, n_batch, n_chunks, n_blocks):
    first = pl.program_id(0) == 0

    @pl.when(first)
    def _():
        state[...] = jnp.zeros_like(state)

    scan = _scan_pieces(m_ref, n_ref, r_ref, o_ref, bonus_ref, gate_ref, gam_ref, lnw_ref,
                        lnb_ref, seg_ref, orw_ref, state, obuf, n_batch=n_batch, n_chunks=n_chunks)
    attn = [_attn_pieces(sink_ref, q_ref.at[b], kp_ref.at[b], kc_ref.at[b], vp_ref.at[b],
                         vc_ref.at[b], oat_ref.at[b], first=first, n_blocks=n_blocks)
            for b in range(n_batch)]
    streams = [scan] + attn
    while streams:
        streams = [g for g in streams if next(g, StopIteration) is not StopIteration]


def _scan_attn(mats, p_attn, n_batch, seq_len, ln_w, ln_b, sinks):
    W = RWKV_WIDTH
    tt = MIX_ROWS
    n_blocks = tt // WINDOW
    lane_head = jnp.arange(LANES) // HEAD_DIM
    seg_mean = ((lane_head[:, None] == lane_head[None, :]).astype(F32) / HEAD_DIM).astype(BF16)
    *mats, gam = mats
    mats = [t.reshape(n_batch, seq_len, W) for t in mats]
    mats.append(gam.reshape(n_batch, seq_len // CHUNK, W))
    p3 = p_attn.reshape(n_batch, seq_len, ATTN_COLS)
    kcol = ATTN_WIDTH // LANES
    vcol = kcol + KV_WIDTH // LANES
    blk = pl.BlockSpec((n_batch, tt, W), lambda i: (0, i, 0))
    gam_blk = pl.BlockSpec((n_batch, tt // CHUNK, W), lambda i: (0, i, 0))
    cur = lambda col: pl.BlockSpec((n_batch, tt, KV_WIDTH), lambda i: (0, i, col))
    prev = lambda col: pl.BlockSpec((n_batch, WINDOW, KV_WIDTH),
                                    lambda i: (0, jnp.maximum(i * n_blocks - 1, 0), col))
    o_rwkv, o_attn = pl.pallas_call(
        functools.partial(_scan_attn_kernel, n_batch=n_batch, n_chunks=tt // CHUNK,
                          n_blocks=n_blocks),
        grid=(seq_len // tt,),
        in_specs=[blk] * 6 + [gam_blk, _resident((1, W)), _resident((1, W)),
                              _resident((LANES, LANES)),
                              pl.BlockSpec(memory_space=pltpu.SMEM),
                              pl.BlockSpec((n_batch, tt, ATTN_WIDTH), lambda i: (0, i, 0)),
                              prev(kcol), cur(kcol), prev(vcol), cur(vcol)],
        out_specs=[blk, pl.BlockSpec((n_batch, tt, ATTN_WIDTH), lambda i: (0, i, 0))],
        out_shape=[jax.ShapeDtypeStruct((n_batch, seq_len, W), BF16),
                   jax.ShapeDtypeStruct((n_batch, seq_len, ATTN_WIDTH), BF16)],
        scratch_shapes=[pltpu.VMEM((n_batch * N_SLABS, CHUNK, SLAB), F32),
                        pltpu.VMEM((n_batch, tt, W), F32)],
        compiler_params=pltpu.CompilerParams(
            dimension_semantics=("arbitrary",), vmem_limit_bytes=VMEM_LIMIT),
        name="scan_attn",
    )(*mats, ln_w.astype(F32).reshape(1, W), ln_b.astype(F32).reshape(1, W), seg_mean,
      sinks.astype(F32), p3, p3, p3, p3, p3)
    return (o_rwkv.reshape(n_batch * seq_len, W),
            o_attn.reshape(n_batch * seq_len, ATTN_WIDTH))


def kernel(x, norm_ffn1, ffn1_gate, ffn1_up, ffn1_down, norm_mix, w_in, b_in_attn, rwkv_shift_mix, rwkv_w0, rwkv_w2, rwkv_a0, rwkv_a2, rwkv_g2, rwkv_k_k, rwkv_k_a, rwkv_r_k, rwkv_ln_w, rwkv_ln_b, attn_sinks, w_out, norm_ffn2, ffn2_gate, ffn2_up, ffn2_down, norm_final):
    n_batch, seq_len, d = x.shape
    depth = w_in.shape[0]
    h = x.reshape(n_batch * seq_len, d)
    for l in range(depth):
        h = _ffn(h, norm_ffn1[l], ffn1_gate[l], ffn1_up[l], ffn1_down[l])
        p_shift, p_attn = _in_proj(h, seq_len, norm_mix[l], w_in[l], b_in_attn[l],
                                   rwkv_shift_mix[l])
        mats = _rwkv_prep(p_shift, rwkv_w0[l], rwkv_w2[l], rwkv_a0[l], rwkv_a2[l], rwkv_g2[l],
                          rwkv_k_k[l], rwkv_k_a[l], rwkv_r_k[l])
        o_rwkv, o_attn = _scan_attn(mats, p_attn, n_batch, seq_len, rwkv_ln_w[l], rwkv_ln_b[l],
                                    attn_sinks[l])
        h = _ffn(h, norm_ffn2[l], ffn2_gate[l], ffn2_up[l], ffn2_down[l],
                 mix=(o_rwkv, o_attn, w_out[l]),
                 final_norm=norm_final if l == depth - 1 else None)
    return h.reshape(n_batch, seq_len, d)
```

```python
import functools
import math

import jax
import jax.numpy as jnp
from jax import lax
from jax.experimental import pallas as pl
from jax.experimental.pallas import tpu as pltpu

F32 = jnp.float32
BF16 = jnp.bfloat16

HEAD_DIM = 64
LANES = 128
SLAB = 128
HEADS_PER_SLAB = SLAB // HEAD_DIM
CHUNK = 64
N_RWKV_HEADS = 8
RWKV_WIDTH = N_RWKV_HEADS * HEAD_DIM
N_SLABS = RWKV_WIDTH // SLAB
DECAY_LORA = 64
AAA_LORA = 64
GATE_LORA = 128
RWKV_COLS = 3 * RWKV_WIDTH + DECAY_LORA + AAA_LORA + GATE_LORA
N_Q_HEADS = 8
N_KV_HEADS = 2
ATTN_WIDTH = N_Q_HEADS * HEAD_DIM
KV_WIDTH = N_KV_HEADS * HEAD_DIM
ATTN_COLS = ATTN_WIDTH + 2 * KV_WIDTH
WINDOW = 128
GN_EPS = 64e-5
NORM_EPS = 1e-5
VMEM_LIMIT = 56 * 1024 * 1024

FFN_ROWS = 1024
FFN_MIX_ROWS = 512
FFN_COLS = 256
INPROJ_ROWS = 1024
RWKV_ROWS = 512
MIX_ROWS = 512


def _dot(a, b):
    return jnp.dot(a, b, preferred_element_type=F32)


def _dot_nt(a, b):
    return lax.dot_general(a, b, (((1,), (1,)), ((), ())), preferred_element_type=F32)


def _rms(x, g):
    ms = jnp.mean(x * x, axis=-1, keepdims=True)
    return x * lax.rsqrt(ms + NORM_EPS) * g


def _lane_block(shape, width, span=SLAB):
    lane = lax.broadcasted_iota(jnp.int32, shape, 1)
    return (lane & (span - 1)) >> (width.bit_length() - 1)


def _block_diag(x, width=HEAD_DIM):
    blk = _lane_block(x.shape, width)
    return jnp.concatenate([jnp.where(blk == g, x, 0.0) for g in range(SLAB // width)], axis=0)


def _split2(x):
    hi = x.astype(BF16)
    return hi, (x - hi.astype(F32)).astype(BF16)


def _seg_sum(x, seg):
    xb = x.astype(BF16)
    rows, groups = x.shape[0], x.shape[1] // LANES
    y = _dot(jnp.concatenate([xb[:, j * LANES:(j + 1) * LANES] for j in range(groups)], axis=0), seg)
    return jnp.concatenate([y[j * rows:(j + 1) * rows] for j in range(groups)], axis=1)


def _resident(shape):
    return pl.BlockSpec(shape, lambda *_: (0,) * len(shape), pipeline_mode=pl.Buffered(1))


def _ffn_kernel(*refs, has_mix, final_norm, nf, tf):
    refs = list(refs)
    x_ref = refs.pop(0)
    if has_mix:
        orw_ref, oat_ref, wo_ref = refs[:3]
        refs = refs[3:]
    g_ref, wg_hbm, wu_hbm, wd_hbm = refs[:4]
    refs = refs[4:]
    if final_norm:
        gf_ref = refs.pop(0)
    out_ref, h_s, acc, wg_ref, wu_ref, wd_ref, stg_g, stg_u, stg_d, sem = refs

    def copies(f):
        cols = pl.ds(f * tf, tf)
        slot = f % 2
        return (pltpu.make_async_copy(wg_hbm.at[:, cols], stg_g.at[slot], sem.at[0, slot]),
                pltpu.make_async_copy(wu_hbm.at[:, cols], stg_u.at[slot], sem.at[1, slot]),
                pltpu.make_async_copy(wd_hbm.at[cols, :], stg_d.at[slot], sem.at[2, slot]))

    def body(streaming):
        if streaming:
            for f in range(min(2, nf)):
                for cp in copies(f):
                    cp.start()
        x = x_ref[...]
        if has_mix:
            x = (x + _dot(orw_ref[...], wo_ref[:RWKV_WIDTH].astype(BF16))
                 + _dot(oat_ref[...], wo_ref[RWKV_WIDTH:].astype(BF16)))
            out_ref[...] = x
        h_s[...] = _rms(x, g_ref[...]).astype(BF16)

        def down(f):
            cols = pl.ds(f * tf, tf)
            h = h_s[...]
            if streaming:
                slot = f % 2
                cp_g, cp_u, cp_d = copies(f)
                cp_g.wait()
                wg_ref[:, cols] = stg_g[slot].astype(BF16)
                cp_u.wait()
                wu_ref[:, cols] = stg_u[slot].astype(BF16)
                cp_d.wait()
                wd_ref[cols, :] = stg_d[slot].astype(BF16)
                if f + 2 < nf:
                    for cp in copies(f + 2):
                        cp.start()
            gate = _dot(h, wg_ref[:, cols])
            up = _dot(h, wu_ref[:, cols])
            act = (gate * jax.nn.sigmoid(gate) * up).astype(BF16)
            return _dot(act, wd_ref[cols, :])

        acc[...] = down(0)
        for f in range(1, nf):
            acc[...] += down(f)
        res = out_ref[...] if has_mix else x_ref[...]
        y = res + 0.5 * acc[...]
        if final_norm:
            y = _rms(y, gf_ref[...])
        out_ref[...] = y

    first = pl.program_id(0) == 0
    pl.when(first)(lambda: body(True))
    pl.when(jnp.logical_not(first))(lambda: body(False))


def _ffn(x, norm, w_gate, w_up, w_down, mix=None, final_norm=None):
    n, d = x.shape
    f = w_gate.shape[1]
    tm, tf = (FFN_MIX_ROWS if mix is not None else FFN_ROWS), FFN_COLS
    nf = f // tf
    row = lambda i: (i, 0)
    hbm = pl.BlockSpec(memory_space=pl.ANY)
    args = [x]
    specs = [pl.BlockSpec((tm, d), row)]
    if mix is not None:
        o_rwkv, o_attn, w_out = mix
        args += [o_rwkv, o_attn, w_out]
        specs += [pl.BlockSpec((tm, RWKV_WIDTH), row), pl.BlockSpec((tm, ATTN_WIDTH), row),
                  _resident((RWKV_WIDTH + ATTN_WIDTH, d))]
    args += [norm.reshape(1, d), w_gate, w_up, w_down]
    specs += [_resident((1, d)), hbm, hbm, hbm]
    if final_norm is not None:
        args.append(final_norm.reshape(1, d))
        specs.append(_resident((1, d)))
    return pl.pallas_call(
        functools.partial(_ffn_kernel, has_mix=mix is not None,
                          final_norm=final_norm is not None, nf=nf, tf=tf),
        grid=(n // tm,),
        in_specs=specs,
        out_specs=pl.BlockSpec((tm, d), row),
        out_shape=jax.ShapeDtypeStruct((n, d), F32),
        scratch_shapes=[pltpu.VMEM((tm, d), BF16), pltpu.VMEM((tm, d), F32),
                        pltpu.VMEM((d, f), BF16), pltpu.VMEM((d, f), BF16), pltpu.VMEM((f, d), BF16),
                        pltpu.VMEM((2, d, tf), F32), pltpu.VMEM((2, d, tf), F32),
                        pltpu.VMEM((2, tf, d), F32), pltpu.SemaphoreType.DMA((3, 2))],
        compiler_params=pltpu.CompilerParams(
            dimension_semantics=("arbitrary",), vmem_limit_bytes=VMEM_LIMIT),
        name="ffn_mix" if mix is not None else "ffn",
    )(*args)


def _inproj_kernel(x_ref, g_ref, w_ref, b_ref, mix_ref, prw_ref, pat_ref, last_row,
                   *, tiles_per_seq):
    h = _rms(x_ref[...], g_ref[...]).astype(BF16)
    p = _dot(h, w_ref[...])
    pat_ref[...] = p[:, RWKV_COLS:] + b_ref[...]
    pr = p[:, :RWKV_COLS]
    prev_row = jnp.where(pl.program_id(0) % tiles_per_seq == 0, 0.0, last_row[0:1, :])
    row = lax.broadcasted_iota(jnp.int32, pr.shape, 0)
    p_prev = jnp.where(row == 0, prev_row, pltpu.roll(pr, 1, axis=0))
    prw_ref[...] = pr + (p_prev - pr) * mix_ref[...]
    last_row[0:1, :] = pr[pr.shape[0] - 1:, :]


def _in_proj(x, seq_len, norm, w_in, b_attn, shift_mix):
    n, d = x.shape
    cols = w_in.shape[1]
    tm = INPROJ_ROWS
    bias = b_attn.astype(F32).reshape(1, ATTN_COLS)
    row = lambda i: (i, 0)
    return pl.pallas_call(
        functools.partial(_inproj_kernel, tiles_per_seq=seq_len // tm),
        grid=(n // tm,),
        in_specs=[pl.BlockSpec((tm, d), row), _resident((1, d)), _resident((d, cols)),
                  _resident((1, ATTN_COLS)), _resident((1, RWKV_COLS))],
        out_specs=[pl.BlockSpec((tm, RWKV_COLS), row), pl.BlockSpec((tm, ATTN_COLS), row)],
        out_shape=[jax.ShapeDtypeStruct((n, RWKV_COLS), F32),
                   jax.ShapeDtypeStruct((n, ATTN_COLS), F32)],
        scratch_shapes=[pltpu.VMEM((8, RWKV_COLS), F32)],
        compiler_params=pltpu.CompilerParams(
            dimension_semantics=("arbitrary",), vmem_limit_bytes=VMEM_LIMIT),
        name="in_proj",
    )(x, norm.reshape(1, d), w_in.astype(BF16), bias, shift_mix.astype(F32).reshape(1, -1))


def _rwkv_prep_kernel(ps_ref, w0_ref, w2_ref, a0_ref, a2_ref, g2_ref, kk_ref, ka_ref, rk_ref,
                      seg_ref,
                      m_ref, n_ref, r_ref, o_ref, bonus_ref, gate_ref, gam_ref,
                      at_s, bt_s, kt_s, rt_s, v_s, bh_s, kh_s, rf_s, *, n_chunks):
    C = CHUNK
    W = RWKV_WIDTH
    ps = ps_ref[...]
    r = ps[:, 0:W]
    k = ps[:, W:2 * W]
    v = ps[:, 2 * W:3 * W]
    wa = ps[:, 3 * W:3 * W + LANES]
    gl = ps[:, 3 * W + LANES:3 * W + 2 * LANES]
    z = w0_ref[...] + _dot(jnp.tanh(wa).astype(BF16), w2_ref[...])
    lw = -math.exp(-0.5) * jax.nn.sigmoid(z)
    a = jax.nn.sigmoid(a0_ref[...] + _dot(wa.astype(BF16), a2_ref[...]))
    gate_ref[...] = _dot(jax.nn.sigmoid(gl).astype(BF16), g2_ref[...]).astype(BF16)
    seg = seg_ref[...]
    kk = k * kk_ref[...]
    kk = kk * jnp.minimum(lax.rsqrt(_seg_sum(kk * kk, seg)), 1e12)
    k = k * (1.0 + (a - 1.0) * ka_ref[...])
    b = kk * a
    bonus_ref[...] = (_seg_sum(r * k * rk_ref[...], seg) * v).astype(BF16)
    v_s[...] = v.astype(BF16)
    cs = lw
    row_in_chunk = lax.broadcasted_iota(jnp.int32, lw.shape, 0) & (C - 1)
    for shift in (1, 2, 4, 8, 16, 32):
        cs = cs + jnp.where(row_in_chunk >= shift, pltpu.roll(cs, shift, axis=0), 0.0)
    for c in range(n_chunks):
        rows = slice(c * C, (c + 1) * C)
        cs_c = cs[rows]
        gam = jnp.exp(cs[(c + 1) * C - 1:(c + 1) * C, :])
        e_neg = jnp.exp(-cs_c)
        rt = r[rows] * jnp.exp(cs_c)
        bt = b[rows] * e_neg
        kt = k[rows] * e_neg
        at_s[rows, :] = (-kk[rows] * jnp.exp(cs_c - lw[rows])).astype(BF16)
        bt_s[rows, :] = bt.astype(BF16)
        kt_s[rows, :] = kt.astype(BF16)
        rt_s[rows, :] = rt.astype(BF16)
        rf_s[rows, :] = rt
        bh_s[rows, :] = bt * gam
        kh_s[rows, :] = kt * gam
        gam_ref[c:c + 1, :] = gam

    rowi = lax.broadcasted_iota(jnp.int32, (2 * C, 2 * SLAB), 0)
    lane_s = lax.broadcasted_iota(jnp.int32, (2 * C, 2 * SLAB), 1) & (C - 1)
    keep_a1 = lane_s < (rowi & (C - 1)) + jnp.where(rowi < C, 0, 1)
    row_c = lax.broadcasted_iota(jnp.int32, (C, SLAB), 0)
    col_c = lax.broadcasted_iota(jnp.int32, (C, SLAB), 1) & (C - 1)
    eye_pack = jnp.where(col_c == row_c, 1.0, 0.0).astype(F32)
    same_head = ((lax.broadcasted_iota(jnp.int32, (SLAB, 2 * SLAB), 0) >> 6)
                 == _lane_block((SLAB, 2 * SLAB), HEAD_DIM))
    zeros_c = jnp.zeros((C, SLAB), BF16)

    def fold(x):
        out = x[:HEAD_DIM]
        for h in range(1, HEADS_PER_SLAB):
            out = out + x[h * HEAD_DIM:(h + 1) * HEAD_DIM]
        return out

    chains = [(c, s) for c in range(n_chunks) for s in range(N_SLABS)]
    ld = lambda ref: [ref[c * C:(c + 1) * C, s * SLAB:(s + 1) * SLAB] for c, s in chains]
    at, bt, kt, rt, vv = ld(at_s), ld(bt_s), ld(kt_s), ld(rt_s), ld(v_s)
    a1 = [jnp.where(keep_a1,
                    _dot_nt(jnp.concatenate([x, y], axis=0),
                            jnp.concatenate([_block_diag(p), _block_diag(q)], axis=0)), 0.0)
          for x, y, p, q in zip(at, rt, bt, kt)]
    aab = [x[:C, :SLAB] for x in a1]
    kv = [_dot(x[:, SLAB:].astype(BF16), _block_diag(y)) for x, y in zip(a1, vv)]
    akv = [_split2(x[:C]) for x in kv]
    arkv = [x[C:] for x in kv]
    def sub_blocks(x, s):
        return jnp.where(((row_c // s) & 1 == 1) & (col_c // s == row_c // s - 1), x, 0.0)

    tinv = [eye_pack + sub_blocks(x, 1) for x in aab]
    for s in (2, 4, 8, 16, 32):
        tb = [t.astype(BF16) for t in tinv]
        te = [_dot(t, _block_diag(sub_blocks(x, s).astype(BF16))).astype(BF16)
              for t, x in zip(tb, aab)]
        tinv = [t + _dot(x, _block_diag(y)) for t, x, y in zip(tinv, te, tb)]
    tb = [t.astype(BF16) for t in tinv]
    wu = [_dot(t, _block_diag(jnp.concatenate([x, y[0]], axis=1)))
          for t, x, y in zip(tb, at, akv)]
    wu = [_split2(jnp.concatenate([x[:, :SLAB], x[:, SLAB:] + _dot(t, _block_diag(y[1]))], axis=1))
          for x, t, y in zip(wu, tb, akv)]
    for i, (c, s) in enumerate(chains):
        rows = slice(c * C, (c + 1) * C)
        sl = slice(s * SLAB, (s + 1) * SLAB)
        lhs_t = jnp.concatenate([bh_s[rows, sl], kh_s[rows, sl]], axis=0)
        wu_hi, wu_lo = wu[i]
        rhs = jnp.concatenate([wu_hi, jnp.concatenate([zeros_c, vv[i]], axis=1)], axis=0)
        mn = _dot(lhs_t.T.astype(BF16), rhs)
        mn = jnp.where(same_head, mn, 0.0)
        m_ref[rows, sl] = fold(mn[:, :SLAB]).astype(BF16)
        n_ref[rows, sl] = fold(mn[:, SLAB:]).astype(BF16)
        arb = a1[i][C:, :SLAB].astype(BF16)
        ro = _dot(arb, _block_diag(wu_hi)) + _dot(arb, _block_diag(wu_lo))
        r_ref[rows, sl] = (rf_s[rows, sl] + ro[:, :SLAB]).astype(BF16)
        o_ref[rows, sl] = (ro[:, SLAB:] + arkv[i]).astype(BF16)


def _rwkv_prep(p_shift, w0, w2, a0, a2, g2, k_k, k_a, r_k):
    n = p_shift.shape[0]
    W = RWKV_WIDTH
    tt = RWKV_ROWS
    lane_head = jnp.arange(LANES) // HEAD_DIM
    seg = (lane_head[:, None] == lane_head[None, :]).astype(BF16)
    w2p = jnp.concatenate([w2, jnp.zeros((AAA_LORA, W), w2.dtype)], axis=0).astype(BF16)
    a2p = jnp.concatenate([jnp.zeros((DECAY_LORA, W), a2.dtype), a2], axis=0).astype(BF16)
    row = lambda i: (i, 0)
    vec = lambda t: t.astype(F32).reshape(1, -1)
    bf16_out = jax.ShapeDtypeStruct((n, W), BF16)
    return pl.pallas_call(
        functools.partial(_rwkv_prep_kernel, n_chunks=tt // CHUNK),
        grid=(n // tt,),
        in_specs=[pl.BlockSpec((tt, RWKV_COLS), row),
                  _resident((1, W)), _resident((LANES, W)), _resident((1, W)),
                  _resident((LANES, W)), _resident((GATE_LORA, W)),
                  _resident((1, W)), _resident((1, W)), _resident((1, W)),
                  _resident((LANES, LANES))],
        out_specs=[pl.BlockSpec((tt, W), row)] * 6 + [pl.BlockSpec((tt // CHUNK, W), row)],
        out_shape=[bf16_out] * 6 + [jax.ShapeDtypeStruct((n // CHUNK, W), F32)],
        scratch_shapes=[pltpu.VMEM((tt, W), BF16)] * 5 + [pltpu.VMEM((tt, W), F32)] * 3,
        compiler_params=pltpu.CompilerParams(
            dimension_semantics=("parallel",), vmem_limit_bytes=VMEM_LIMIT),
        name="rwkv_prep",
    )(p_shift, vec(w0), w2p, vec(a0), a2p, g2.astype(BF16), vec(k_k), vec(k_a), vec(r_k),
      seg)


def _scan_pieces(m_ref, n_ref, r_ref, o_ref, bonus_ref, gate_ref, gam_ref, lnw_ref, lnb_ref,
                 seg_ref, out_ref, state, obuf, *, n_batch, n_chunks):
    C = CHUNK
    first_head = _lane_block((C, SLAB), HEAD_DIM) == 0
    for c in range(n_chunks):
        rows = slice(c * C, (c + 1) * C)
        for b in range(n_batch):
            for s in range(N_SLABS):
                sl = slice(s * SLAB, (s + 1) * SLAB)
                s0 = state[b * N_SLABS + s]
                lhs = jnp.concatenate([m_ref[b, rows, sl], r_ref[b, rows, sl]], axis=0)
                res = _dot(lhs, _block_diag(s0.astype(BF16)))
                g_col = jnp.broadcast_to(gam_ref[b, c:c + 1, sl], (SLAB, SLAB)).T
                decay = jnp.where(first_head, g_col[:HEAD_DIM], g_col[HEAD_DIM:])
                state[b * N_SLABS + s] = res[:C] + decay * s0 + n_ref[b, rows, sl]
                obuf[b, rows, sl] = res[C:] + o_ref[b, rows, sl]
        yield

    seg_mean = seg_ref[...]
    for b in range(n_batch):
        o = obuf[b]
        d = o - _seg_sum(o, seg_mean)
        var = _seg_sum(d * d, seg_mean)
        y = d * lax.rsqrt(var + GN_EPS) * lnw_ref[...] + lnb_ref[...]
        out_ref[b] = ((y + bonus_ref[b]) * gate_ref[b]).astype(BF16)
        yield


def _attn_pieces(sink_ref, q_ref, kp_ref, kc_ref, vp_ref, vc_ref, out_ref, *, first, n_blocks):
    BQ = WINDOW
    kx = jnp.concatenate([kp_ref[...], kc_ref[...]], axis=0)
    vx = jnp.concatenate([vp_ref[...], vc_ref[...]], axis=0)
    lo = _lane_block(kx.shape, HEAD_DIM, LANES) == 0
    kxr = pltpu.roll(kx, HEAD_DIM, axis=1)
    vxr = pltpu.roll(vx, HEAD_DIM, axis=1)
    bf = lambda t: t.astype(BF16)
    k_lo = [bf(jnp.where(lo, kx, 0.0)), bf(jnp.where(lo, kxr, 0.0))]
    k_hi = [bf(jnp.where(lo, 0.0, kxr)), bf(jnp.where(lo, 0.0, kx))]
    v_lo = [bf(jnp.where(lo, vx, 0.0)), bf(jnp.where(lo, vxr, 0.0))]
    v_hi = [bf(jnp.where(lo, 0.0, vxr)), bf(jnp.where(lo, 0.0, vx))]

    qi = lax.broadcasted_iota(jnp.int32, (BQ, 2 * BQ), 0)
    kj = lax.broadcasted_iota(jnp.int32, (BQ, 2 * BQ), 1)
    dist = qi + BQ - kj
    band = (dist >= 0) & (dist < WINDOW)
    band_first = band & (kj >= jnp.where(first, BQ, 0))
    col0 = lax.broadcasted_iota(jnp.int32, (1, 2 * BQ), 1) == 0
    log2e = math.log2(math.e)
    fill = [jnp.where(col0, sink_ref[h] * log2e, -jnp.inf) for h in range(N_Q_HEADS)]
    vrow = lax.broadcasted_iota(jnp.int32, (4 * BQ, LANES), 0) & (2 * BQ - 1)
    ones_sel = jnp.where((lax.broadcasted_iota(jnp.int32, (4 * BQ, LANES), 0) < 2 * BQ)
                         == (_lane_block((4 * BQ, LANES), HEAD_DIM, LANES) == 0),
                         1.0, 0.0).astype(BF16)
    qscale = HEAD_DIM ** -0.5 * log2e
    n_pairs = N_Q_HEADS // 2

    kv_of = [(2 * j) // (N_Q_HEADS // N_KV_HEADS) for j in range(n_pairs)]
    units = [(qb, j) for qb in range(n_blocks) for j in range(n_pairs)]
    kcat, vcat = {}, {}
    for qb in range(n_blocks):
        krows = slice(qb * BQ, (qb + 2) * BQ)
        for g in range(N_KV_HEADS):
            kcat[qb, g] = jnp.concatenate([k_lo[g][krows], k_hi[g][krows]], axis=0)
            vcat[qb, g] = jnp.concatenate(
                [jnp.where(vrow == 0, 0.0,
                           jnp.concatenate([v_lo[g][krows], v_hi[g][krows]], axis=0)),
                 ones_sel], axis=1)
    yield
    s = [_dot_nt((q_ref[qb * BQ:(qb + 1) * BQ, j * LANES:(j + 1) * LANES] * qscale).astype(BF16),
                 kcat[qb, kv_of[j]]) for qb, j in units]
    yield
    sh = [[jnp.where(band_first if qb == 0 else band,
                     x[:, hh * 2 * BQ:(hh + 1) * 2 * BQ], fill[2 * j + hh])
           for hh in range(2)] for x, (qb, j) in zip(s, units)]
    mx = [[jnp.max(x, axis=-1, keepdims=True) for x in row] for row in sh]
    yield
    e = [jnp.concatenate([jnp.exp2(x - m).astype(BF16) for x, m in zip(xr, mr)], axis=1)
         for xr, mr in zip(sh, mx)]
    yield
    pv = [_dot(x, vcat[qb, kv_of[j]]) for x, (qb, j) in zip(e, units)]
    yield
    for x, (qb, j) in zip(pv, units):
        out_ref[qb * BQ:(qb + 1) * BQ, j * LANES:(j + 1) * LANES] = (
            x[:, :LANES] / x[:, LANES:]).astype(BF16)
    yield


def _scan_attn_kernel(m_ref, n_ref, r_ref, o_ref, bonus_ref, gate_ref, gam_ref, lnw_ref, lnb_ref,
                      seg_ref, sink_ref, q_ref, kp_ref, kc_ref, vp_ref, vc_ref,
                      orw_ref, oat_ref, state, obuf, *, n_batch, n_chunks, n_blocks):
    first = pl.program_id(0) == 0

    @pl.when(first)
    def _():
        state[...] = jnp.zeros_like(state)

    scan = _scan_pieces(m_ref, n_ref, r_ref, o_ref, bonus_ref, gate_ref, gam_ref, lnw_ref,
                        lnb_ref, seg_ref, orw_ref, state, obuf, n_batch=n_batch, n_chunks=n_chunks)
    attn = [_attn_pieces(sink_ref, q_ref.at[b], kp_ref.at[b], kc_ref.at[b], vp_ref.at[b],
                         vc_ref.at[b], oat_ref.at[b], first=first, n_blocks=n_blocks)
            for b in range(n_batch)]
    streams = [scan] + attn
    while streams:
        streams = [g for g in streams if next(g, StopIteration) is not StopIteration]


def _scan_attn(mats, p_attn, n_batch, seq_len, ln_w, ln_b, sinks):
    W = RWKV_WIDTH
    tt = MIX_ROWS
    n_blocks = tt // WINDOW
    lane_head = jnp.arange(LANES) // HEAD_DIM
    seg_mean = ((lane_head[:, None] == lane_head[None, :]).astype(F32) / HEAD_DIM).astype(BF16)
    *mats, gam = mats
    mats = [t.reshape(n_batch, seq_len, W) for t in mats]
    mats.append(gam.reshape(n_batch, seq_len // CHUNK, W))
    p3 = p_attn.reshape(n_batch, seq_len, ATTN_COLS)
    kcol = ATTN_WIDTH // LANES
    vcol = kcol + KV_WIDTH // LANES
    blk = pl.BlockSpec((n_batch, tt, W), lambda i: (0, i, 0))
    gam_blk = pl.BlockSpec((n_batch, tt // CHUNK, W), lambda i: (0, i, 0))
    cur = lambda col: pl.BlockSpec((n_batch, tt, KV_WIDTH), lambda i: (0, i, col))
    prev = lambda col: pl.BlockSpec((n_batch, WINDOW, KV_WIDTH),
                                    lambda i: (0, jnp.maximum(i * n_blocks - 1, 0), col))
    o_rwkv, o_attn = pl.pallas_call(
        functools.partial(_scan_attn_kernel, n_batch=n_batch, n_chunks=tt // CHUNK,
                          n_blocks=n_blocks),
        grid=(seq_len // tt,),
        in_specs=[blk] * 6 + [gam_blk, _resident((1, W)), _resident((1, W)),
                              _resident((LANES, LANES)),
                              pl.BlockSpec(memory_space=pltpu.SMEM),
                              pl.BlockSpec((n_batch, tt, ATTN_WIDTH), lambda i: (0, i, 0)),
                              prev(kcol), cur(kcol), prev(vcol), cur(vcol)],
        out_specs=[blk, pl.BlockSpec((n_batch, tt, ATTN_WIDTH), lambda i: (0, i, 0))],
        out_shape=[jax.ShapeDtypeStruct((n_batch, seq_len, W), BF16),
                   jax.ShapeDtypeStruct((n_batch, seq_len, ATTN_WIDTH), BF16)],
        scratch_shapes=[pltpu.VMEM((n_batch * N_SLABS, CHUNK, SLAB), F32),
                        pltpu.VMEM((n_batch, tt, W), F32)],
        compiler_params=pltpu.CompilerParams(
            dimension_semantics=("arbitrary",), vmem_limit_bytes=VMEM_LIMIT),
        name="scan_attn",
    )(*mats, ln_w.astype(F32).reshape(1, W), ln_b.astype(F32).reshape(1, W), seg_mean,
      sinks.astype(F32), p3, p3, p3, p3, p3)
    return (o_rwkv.reshape(n_batch * seq_len, W),
            o_attn.reshape(n_batch * seq_len, ATTN_WIDTH))


def kernel(x, norm_ffn1, ffn1_gate, ffn1_up, ffn1_down, norm_mix, w_in, b_in_attn, rwkv_shift_mix, rwkv_w0, rwkv_w2, rwkv_a0, rwkv_a2, rwkv_g2, rwkv_k_k, rwkv_k_a, rwkv_r_k, rwkv_ln_w, rwkv_ln_b, attn_sinks, w_out, norm_ffn2, ffn2_gate, ffn2_up, ffn2_down, norm_final):
    n_batch, seq_len, d = x.shape
    depth = w_in.shape[0]
    h = x.reshape(n_batch * seq_len, d)
    for l in range(depth):
        h = _ffn(h, norm_ffn1[l], ffn1_gate[l], ffn1_up[l], ffn1_down[l])
        p_shift, p_attn = _in_proj(h, seq_len, norm_mix[l], w_in[l], b_in_attn[l],
                                   rwkv_shift_mix[l])
        mats = _rwkv_prep(p_shift, rwkv_w0[l], rwkv_w2[l], rwkv_a0[l], rwkv_a2[l], rwkv_g2[l],
                          rwkv_k_k[l], rwkv_k_a[l], rwkv_r_k[l])
        o_rwkv, o_attn = _scan_attn(mats, p_attn, n_batch, seq_len, rwkv_ln_w[l], rwkv_ln_b[l],
                                    attn_sinks[l])
        h = _ffn(h, norm_ffn2[l], ffn2_gate[l], ffn2_up[l], ffn2_down[l],
                 mix=(o_rwkv, o_attn, w_out[l]),
                 final_norm=norm_final if l == depth - 1 else None)
    return h.reshape(n_batch, seq_len, d)
```
